```python
import math
import jax, jax.numpy as jnp
from jax import lax
import numpy as np

D_MODEL = 1024
BATCH = 8
SEQ = 4096
DEPTH = 2

PLE_DIM = 256
MIX_WIDTH = D_MODEL
HGRN_WIDTH = MIX_WIDTH // 2
HGRN_KEY_DIM = 128
HGRN_HEADS = HGRN_WIDTH // HGRN_KEY_DIM
HGRN_VAL_DIM = HGRN_WIDTH // HGRN_HEADS
HGRN_CHUNK = 64
DIFF_WIDTH = MIX_WIDTH - HGRN_WIDTH
DIFF_VAL_DIM = 128
DIFF_HEADS = DIFF_WIDTH // DIFF_VAL_DIM
DIFF_HEAD_DIM = DIFF_VAL_DIM // 2
Q_BLOCK = 128
IN_COLS = 4 * HGRN_WIDTH + 3 * DIFF_WIDTH
SPLITS = (HGRN_WIDTH, 2 * HGRN_WIDTH, 3 * HGRN_WIDTH, 4 * HGRN_WIDTH,
          4 * HGRN_WIDTH + DIFF_WIDTH, 4 * HGRN_WIDTH + 2 * DIFF_WIDTH)
N_EXPERTS = 16
N_GROUPS = 4
EXPERTS_PER_GROUP = N_EXPERTS // N_GROUPS
TOP_K = 2
D_FF_EXPERT = D_MODEL // 2
MOE_BLOCK = 128
EPS = 1e-6
MIN_FORGET = 1e-6
MASK_VALUE = -1e30

kernel_name = "hymba_hgrn2_diffattn_grouped_moe_ple"


def rms_norm(x, gain):
    xf = x.astype(jnp.float32)
    y = xf * lax.rsqrt(jnp.mean(xf * xf, axis=-1, keepdims=True) + EPS)
    return (y * gain.astype(jnp.float32)).astype(x.dtype)


def hgrn2_mixer(zq, zf, zi, zg, lb, out_gain):
    b, s, _ = zq.shape
    shp = (b, s, HGRN_HEADS, HGRN_KEY_DIM)
    q = jax.nn.silu(zq.astype(jnp.float32)).reshape(shp)
    f_logit = zf.astype(jnp.float32).reshape(shp)
    lb = lb.reshape(HGRN_HEADS, HGRN_KEY_DIM)
    forget = lb + (1.0 - lb) * jax.nn.sigmoid(f_logit)
    log_f = jnp.log(jnp.maximum(forget, MIN_FORGET))
    k = (1.0 - lb) * jax.nn.sigmoid(-f_logit)
    v = zi.astype(jnp.float32).reshape(b, s, HGRN_HEADS, HGRN_VAL_DIM)
    n_chunks = s // HGRN_CHUNK

    def to_chunks(t):
        return t.reshape(b, n_chunks, HGRN_CHUNK, HGRN_HEADS, t.shape[-1]).transpose(1, 0, 3, 2, 4)

    causal = jnp.tril(jnp.ones((HGRN_CHUNK, HGRN_CHUNK), dtype=bool))[:, :, None]

    def chunk_step(state, inp):
        qc, kc, vc, lfc = inp
        cum = jnp.cumsum(lfc, axis=2)
        o_inter = jnp.einsum("bhtd,bhdv->bhtv", qc * jnp.exp(cum), state)
        rel = jnp.where(causal, cum[:, :, :, None, :] - cum[:, :, None, :, :], MASK_VALUE)
        decay = jnp.where(causal, jnp.exp(rel), 0.0)
        scores = jnp.einsum("bhtd,bhsd,bhtsd->bhts", qc, kc, decay)
        o = o_inter + jnp.einsum("bhts,bhsv->bhtv", scores, vc)
        last = cum[:, :, -1:, :]
        state = (jnp.exp(last[:, :, 0, :])[..., None] * state
                 + jnp.einsum("bhsd,bhsv->bhdv", kc * jnp.exp(last - cum), vc))
        return state, o

    state0 = jnp.zeros((b, HGRN_HEADS, HGRN_KEY_DIM, HGRN_VAL_DIM), jnp.float32)
    _, o = lax.scan(chunk_step, state0,
                    (to_chunks(q), to_chunks(k), to_chunks(v), to_chunks(log_f)))
    o = o.transpose(1, 0, 3, 2, 4).reshape(b, s, HGRN_HEADS, HGRN_VAL_DIM)
    gate = jax.nn.silu(zg.astype(jnp.float32)).reshape(b, s, HGRN_HEADS, HGRN_VAL_DIM)
    o = rms_norm(o, out_gain) * gate
    return o.reshape(b, s, HGRN_WIDTH)


def diff_attention(zq, zk, zv, q_gain, k_gain, lam_q1, lam_k1, lam_q2, lam_k2, subln, lam_init):
    b, s, _ = zq.shape
    shp = (b, s, DIFF_HEADS, 2, DIFF_HEAD_DIM)
    q = rms_norm(zq.reshape(shp), q_gain).astype(jnp.float32) * (DIFF_HEAD_DIM ** -0.5)
    k = rms_norm(zk.reshape(shp), k_gain).astype(jnp.float32)
    v = zv.astype(jnp.float32).reshape(b, s, DIFF_HEADS, DIFF_VAL_DIM)
    lam = (jnp.exp(jnp.sum(lam_q1.astype(jnp.float32) * lam_k1.astype(jnp.float32)))
           - jnp.exp(jnp.sum(lam_q2.astype(jnp.float32) * lam_k2.astype(jnp.float32)))
           + lam_init)
    n_blocks = s // Q_BLOCK
    q_blocks = q.reshape(b, n_blocks, Q_BLOCK, DIFF_HEADS, 2, DIFF_HEAD_DIM).transpose(1, 0, 2, 3, 4, 5)
    key_pos = jnp.arange(s)

    def attend_block(args):
        qb, blk = args
        scores = jnp.einsum("bqhcd,bkhcd->bhcqk", qb, k)
        q_pos = blk * Q_BLOCK + jnp.arange(Q_BLOCK)
        mask = key_pos[None, :] <= q_pos[:, None]
        probs = jax.nn.softmax(jnp.where(mask, scores, MASK_VALUE), axis=-1)
        weights = probs[:, :, 0] - lam * probs[:, :, 1]
        return jnp.einsum("bhqk,bkhv->bqhv", weights, v)

    o = lax.map(attend_block, (q_blocks, jnp.arange(n_blocks)))
    o = o.transpose(1, 0, 2, 3, 4).reshape(b, s, DIFF_HEADS, DIFF_VAL_DIM)
    o = rms_norm(o, subln) * (1.0 - lam_init)
    return o.reshape(b, s, DIFF_WIDTH)


def grouped_moe(xn, w_router, w_gate, w_up, w_down):
    b, s, d = xn.shape
    t = b * s
    xt = xn.reshape(t, d)
    aff = jax.nn.softmax(xt.astype(jnp.float32) @ w_router.astype(jnp.float32), axis=-1)
    aff_g = aff.reshape(t, N_GROUPS, EXPERTS_PER_GROUP)
    group_score = lax.top_k(aff_g, TOP_K)[0].sum(-1)
    g_sel = jnp.argmax(group_score, axis=-1)
    in_group = jnp.take_along_axis(aff_g, g_sel[:, None, None], axis=1)[:, 0]
    top_w, top_local = lax.top_k(in_group, TOP_K)
    top_e = g_sel[:, None] * EXPERTS_PER_GROUP + top_local
    top_w = top_w / jnp.sum(top_w, axis=-1, keepdims=True)
    n_assign = t * TOP_K
    flat_e = top_e.reshape(n_assign)
    order = jnp.argsort(flat_e)
    sorted_e = flat_e[order]
    sorted_tok = (order // TOP_K).astype(jnp.int32)
    sorted_w = top_w.reshape(n_assign)[order]
    counts = jnp.bincount(flat_e, length=N_EXPERTS)
    starts = jnp.cumsum(counts) - counts
    padded = (counts + MOE_BLOCK - 1) // MOE_BLOCK * MOE_BLOCK
    padded_end = jnp.cumsum(padded)
    padded_start = padded_end - padded
    dest = padded_start[sorted_e] + jnp.arange(n_assign) - starts[sorted_e]
    n_rows = n_assign + N_EXPERTS * MOE_BLOCK
    n_blk = n_rows // MOE_BLOCK
    row_tok = jnp.full((n_rows,), t, jnp.int32).at[dest].set(sorted_tok)
    row_w = jnp.zeros((n_rows,), jnp.float32).at[dest].set(sorted_w)
    blk_e = jnp.minimum(jnp.searchsorted(padded_end, jnp.arange(n_blk) * MOE_BLOCK, side="right"),
                        N_EXPERTS - 1)
    x_rows = jnp.concatenate([xt, jnp.zeros((1, d), xt.dtype)], axis=0)[row_tok]
    x_rows = x_rows.reshape(n_blk, MOE_BLOCK, d)

    def expert_block(args):
        xb, e = args
        h = jax.nn.silu(xb @ w_gate[e]) * (xb @ w_up[e])
        return h @ w_down[e]

    y_rows = lax.map(expert_block, (x_rows, blk_e)).reshape(n_rows, d)
    y = jax.ops.segment_sum(y_rows * row_w[:, None].astype(y_rows.dtype), row_tok,
                            num_segments=t + 1)[:t]
    return y.reshape(b, s, d).astype(xn.dtype)


def setup_inputs(seed: int = 0) -> dict:
    key = jax.random.key(seed)
    ks = jax.random.split(key, 22)
    f32 = jnp.float32

    def nrm(k, shape, scale):
        return jax.random.normal(k, shape, f32) * scale

    def gain(k, shape):
        return 1.0 + 0.02 * jax.random.normal(k, shape, f32)

    res_scale = (2 * DEPTH) ** -0.5
    return {
        "x": nrm(ks[0], (BATCH, SEQ, D_MODEL), 1.0),
        "p": nrm(ks[1], (DEPTH, BATCH, SEQ, PLE_DIM), 1.0),
        "mix_norm": gain(ks[2], (DEPTH, D_MODEL)),
        "w_in": nrm(ks[3], (DEPTH, D_MODEL, IN_COLS), D_MODEL ** -0.5),
        "hgrn_lb": nrm(ks[4], (DEPTH, HGRN_WIDTH), 1.0),
        "hgrn_out_norm": gain(ks[5], (DEPTH, HGRN_VAL_DIM)),
        "q_norm": gain(ks[6], (DEPTH, DIFF_HEAD_DIM)),
        "k_norm": gain(ks[7], (DEPTH, DIFF_HEAD_DIM)),
        "lam_q1": nrm(ks[8], (DEPTH, DIFF_HEAD_DIM), 0.1),
        "lam_k1": nrm(ks[9], (DEPTH, DIFF_HEAD_DIM), 0.1),
        "lam_q2": nrm(ks[10], (DEPTH, DIFF_HEAD_DIM), 0.1),
        "lam_k2": nrm(ks[11], (DEPTH, DIFF_HEAD_DIM), 0.1),
        "diff_subln": gain(ks[12], (DEPTH, DIFF_VAL_DIM)),
        "w_out": nrm(ks[13], (DEPTH, MIX_WIDTH, D_MODEL), MIX_WIDTH ** -0.5 * res_scale),
        "ffn_norm": gain(ks[14], (DEPTH, D_MODEL)),
        "w_router": nrm(ks[15], (D_MODEL, N_EXPERTS), D_MODEL ** -0.5),
        "w_gate": nrm(ks[16], (DEPTH, N_EXPERTS, D_MODEL, D_FF_EXPERT), D_MODEL ** -0.5),
        "w_up": nrm(ks[17], (DEPTH, N_EXPERTS, D_MODEL, D_FF_EXPERT), D_MODEL ** -0.5),
        "w_down": nrm(ks[18], (DEPTH, N_EXPERTS, D_FF_EXPERT, D_MODEL), D_FF_EXPERT ** -0.5 * res_scale),
        "ple_proj": nrm(ks[19], (DEPTH, PLE_DIM, D_MODEL), PLE_DIM ** -0.5),
        "ple_norm": gain(ks[20], (DEPTH, D_MODEL)),
        "ple_gate": nrm(ks[21], (DEPTH, D_MODEL, D_MODEL), D_MODEL ** -0.5),
    }


def reference(x, p, mix_norm, w_in, hgrn_lb, hgrn_out_norm, q_norm, k_norm,
              lam_q1, lam_k1, lam_q2, lam_k2, diff_subln, w_out, ffn_norm, w_router,
              w_gate, w_up, w_down, ple_proj, ple_norm, ple_gate):
    lb_soft = jax.nn.softmax(hgrn_lb.astype(jnp.float32), axis=0)
    lower_bounds = jnp.cumsum(lb_soft, axis=0) - lb_soft[0]
    h = x
    for i in range(DEPTH):
        lam_init = 0.8 - 0.6 * math.exp(-0.3 * i)
        hn = rms_norm(h, mix_norm[i])
        z = hn @ w_in[i]
        zq, zf, zi, zg, dq, dk, dv = jnp.split(z, SPLITS, axis=-1)
        o_hgrn = hgrn2_mixer(zq, zf, zi, zg, lower_bounds[i], hgrn_out_norm[i])
        o_diff = diff_attention(dq, dk, dv, q_norm[i], k_norm[i], lam_q1[i], lam_k1[i],
                                lam_q2[i], lam_k2[i], diff_subln[i], lam_init)
        mix = jnp.concatenate([o_hgrn.astype(h.dtype), o_diff.astype(h.dtype)], axis=-1)
        h = h + mix @ w_out[i]
        h = h + grouped_moe(rms_norm(h, ffn_norm[i]), w_router, w_gate[i], w_up[i], w_down[i])
        ple = rms_norm(p[i] @ ple_proj[i], ple_norm[i])
        h = h + ple * jax.nn.sigmoid(h @ ple_gate[i])
    return h
```

```python
import functools
import math

import numpy as np
import jax
import jax.numpy as jnp
from jax import lax
from jax.experimental import pallas as pl
from jax.experimental.pallas import tpu as pltpu

F32 = jnp.float32
BF16 = jnp.bfloat16

EPS = 1e-6
MIN_FORGET = 1e-6
MASK_VALUE = -1e30

D_MODEL = 1024
HGRN_WIDTH = 512
HEAD = 128
N_HEADS = 4
DIFF_HEAD_DIM = 64
PLE_DIM = 256
N_EXPERTS = 16
EXPERTS_PER_GROUP = 4
N_GROUPS = 4
TOP_K = 2
D_FF = 512
IN_COLS = 4 * HGRN_WIDTH + 3 * HGRN_WIDTH

LANES = 128
CHUNK = 128
ROW_BLOCK = 256
VMEM_LIMIT = 48 * 1024 * 1024


def _dot(a, b):
    return jnp.dot(a, b, preferred_element_type=F32)


def _dot_nt(a, b):
    return lax.dot_general(a, b, (((1,), (1,)), ((), ())), preferred_element_type=F32)


def _split_bf16(x):
    hi = x.astype(BF16)
    lo = (x - hi.astype(F32)).astype(BF16)
    return hi, lo


def _params(sem, vmem=VMEM_LIMIT):
    return pltpu.CompilerParams(dimension_semantics=sem, vmem_limit_bytes=vmem)


def _inproj_kernel(h_ref, g_ref, w_ref, gq_ref, gk_ref, grp_ref,
                   zq_ref, zf_ref, zi_ref, zg_ref, dq_ref, dk_ref, dv_ref):
    x = h_ref[...]
    ms = jnp.mean(x * x, axis=-1, keepdims=True)
    hn = (x * lax.rsqrt(ms + EPS) * g_ref[...]).astype(BF16)
    w = HGRN_WIDTH
    for j, o_ref in enumerate((zq_ref, zf_ref, zi_ref, zg_ref)):
        o_ref[...] = _dot(hn, w_ref[:, j * w:(j + 1) * w]).astype(BF16)
    for j, o_ref, gain_ref in ((4, dq_ref, gq_ref), (5, dk_ref, gk_ref)):
        z = _dot(hn, w_ref[:, j * w:(j + 1) * w])
        zz_hi, zz_lo = _split_bf16(z * z)
        ss = _dot(zz_hi, grp_ref[...]) + _dot(zz_lo, grp_ref[...])
        o_ref[...] = (z * lax.rsqrt(ss * (1.0 / DIFF_HEAD_DIM) + EPS) * gain_ref[...]).astype(BF16)
    dv_ref[...] = _dot(hn, w_ref[:, 6 * w:7 * w]).astype(BF16)


def _inproj(h, gain, w_in_bf, gq, gk, grp, tm):
    t = h.shape[0]
    w = HGRN_WIDTH
    row = lambda i: (i, 0)
    const = lambda i: (0, 0)
    out = jax.ShapeDtypeStruct((t, w), BF16)
    return pl.pallas_call(
        _inproj_kernel,
        grid=(t // tm,),
        in_specs=[
            pl.BlockSpec((tm, D_MODEL), row),
            pl.BlockSpec((1, D_MODEL), const),
            pl.BlockSpec((D_MODEL, IN_COLS), const),
            pl.BlockSpec((1, w), const),
            pl.BlockSpec((1, w), const),
            pl.BlockSpec((w, w), const),
        ],
        out_specs=[pl.BlockSpec((tm, w), row)] * 7,
        out_shape=[out] * 7,
        compiler_params=_params(("parallel",)),
        name="inproj",
    )(h, gain, w_in_bf, gq, gk, grp)


_PAIR_HALVES = (8, 16, 32, 64)
_N_EXP = 3 + len(_PAIR_HALVES)


def _hgrn_constants():
    c = CHUNK
    t = np.arange(c)[:, None]
    j = np.arange(c)[None, :]
    mats = [(j <= t).astype(np.float32), (j > t).astype(np.float32)]
    mid = t - t % 8 + 3
    m0 = np.where((t > mid) & (j > mid) & (j <= t), 1.0, 0.0) - np.where((t < mid) & (j > t) & (j <= mid), 1.0, 0.0)
    mats.append(m0.astype(np.float32))
    lvl = np.full((c, c), -1, np.int32)
    tt, ss = np.broadcast_arrays(t, j)
    lvl[(tt // 8 == ss // 8) & (ss <= tt)] = 0
    for n, m in enumerate(_PAIR_HALVES):
        b = t - t % (2 * m) + m - 1
        mats.append(np.where(t > b, (j > b) & (j <= t), (j > t) & (j <= b)).astype(np.float32))
        sel = (tt // (2 * m) == ss // (2 * m)) & (tt % (2 * m) >= m) & (ss % (2 * m) < m)
        lvl[sel] = n + 1
    return np.concatenate(mats, axis=0), lvl


def _hgrn_kernel(zq_ref, zf_ref, zi_ref, zg_ref, lb_ref, og_ref, m_ref, lvl_ref, o_ref, state_ref, *, n_chunks):
    @pl.when(pl.program_id(2) == 0)
    def _():
        state_ref[...] = jnp.zeros_like(state_ref)

    lb = lb_ref[...]
    og = og_ref[...]
    lvl = lvl_ref[...]
    c = CHUNK

    def chunk(ci, carry):
        sl = pl.ds(pl.multiple_of(ci * c, c), c)
        xq = zq_ref[0, sl, :].astype(F32)
        fl = zf_ref[0, sl, :].astype(F32)
        v_bf = zi_ref[0, sl, :]
        xg = zg_ref[0, sl, :].astype(F32)
        q = xq * jax.nn.sigmoid(xq)
        forget = lb + (1.0 - lb) * jax.nn.sigmoid(fl)
        logf = jnp.log(jnp.maximum(forget, MIN_FORGET))
        k = (1.0 - lb) * jax.nn.sigmoid(-fl)
        lf_hi, lf_lo = _split_bf16(logf)
        e_all = _dot(m_ref[...], lf_hi) + _dot(m_ref[...], lf_lo)
        e_cum = e_all[0:c]
        e_out = e_all[c:2 * c]
        e0 = e_all[2 * c:3 * c]
        s = _dot_nt((q * jnp.exp(e0)).astype(BF16), (k * jnp.exp(-e0)).astype(BF16))
        s = jnp.where(lvl == 0, s, 0.0)
        for n in range(len(_PAIR_HALVES)):
            p = jnp.exp(e_all[(3 + n) * c:(4 + n) * c])
            sn = _dot_nt((q * p).astype(BF16), (k * p).astype(BF16))
            s = jnp.where(lvl == n + 1, sn, s)
        state_t = state_ref[...]
        o = _dot_nt((q * jnp.exp(e_cum)).astype(BF16), state_t.astype(BF16))
        o = o + _dot(s.astype(BF16), v_bf)
        kb = (k * jnp.exp(e_out)).astype(BF16)
        v_t = v_bf.astype(F32).T.astype(BF16)
        state_ref[...] = state_t * jnp.exp(e_cum[c - 1:c, :]) + _dot(v_t, kb)
        ms = jnp.mean(o * o, axis=-1, keepdims=True)
        y = o * lax.rsqrt(ms + EPS) * og * (xg * jax.nn.sigmoid(xg))
        o_ref[0, sl, :] = y.astype(BF16)
        return carry

    lax.fori_loop(0, n_chunks, chunk, 0)


def _hgrn(zq, zf, zi, zg, lb, og, ts):
    b, s, _ = zq.shape
    mats, lvl = _hgrn_constants()
    blk = pl.BlockSpec((1, ts, HEAD), lambda bi, hi, si: (bi, si, hi))
    vec = pl.BlockSpec((1, HEAD), lambda bi, hi, si: (0, hi))
    const = lambda bi, hi, si: (0, 0)
    return pl.pallas_call(
        functools.partial(_hgrn_kernel, n_chunks=ts // CHUNK),
        grid=(b, N_HEADS, s // ts),
        in_specs=[blk, blk, blk, blk, vec,
                  pl.BlockSpec((1, HEAD), const),
                  pl.BlockSpec((_N_EXP * CHUNK, CHUNK), const),
                  pl.BlockSpec((CHUNK, CHUNK), const)],
        out_specs=blk,
        out_shape=jax.ShapeDtypeStruct((b, s, HGRN_WIDTH), BF16),
        scratch_shapes=[pltpu.VMEM((HEAD, HEAD), F32)],
        compiler_params=_params(("parallel", "parallel", "arbitrary")),
        name="hgrn2",
    )(zq, zf, zi, zg, lb, og, jnp.asarray(mats, BF16), jnp.asarray(lvl))


def _attn_kernel(q_ref, k_ref, v_ref, lq1_ref, lk1_ref, lq2_ref, lk2_ref, sub_ref, o_ref,
                 m_sc, l_sc, acc_sc, *, tq, lam_init):
    qi = pl.program_id(2)
    q = q_ref[0]
    lane = lax.broadcasted_iota(jnp.int32, q.shape, 1)
    zero = jnp.zeros_like(q)
    q_maps = (jnp.where(lane < DIFF_HEAD_DIM, q, zero), jnp.where(lane >= DIFF_HEAD_DIM, q, zero))
    m_sc[...] = jnp.full_like(m_sc, MASK_VALUE)
    l_sc[...] = jnp.zeros_like(l_sc)
    acc_sc[...] = jnp.zeros_like(acc_sc)

    def step(ki, masked):
        sl = pl.ds(pl.multiple_of(ki * tq, tq), tq)
        kb = k_ref[0, sl, :]
        vb = v_ref[0, sl, :]
        for c in range(2):
            s = _dot_nt(q_maps[c], kb)
            if masked:
                row = lax.broadcasted_iota(jnp.int32, s.shape, 0)
                col = lax.broadcasted_iota(jnp.int32, s.shape, 1)
                s = jnp.where(col <= row, s, MASK_VALUE)
            m_old = m_sc[c]
            m_new = jnp.maximum(m_old, jnp.max(s, axis=-1, keepdims=True))
            alpha = jnp.exp(m_old - m_new)
            p = jnp.exp(s - m_new)
            l_sc[c] = alpha * l_sc[c] + jnp.sum(p, axis=-1, keepdims=True)
            acc_sc[c] = alpha * acc_sc[c] + _dot(p.astype(BF16), vb)
            m_sc[c] = m_new

    def body(ki, carry):
        step(ki, False)
        return carry

    lax.fori_loop(0, qi, body, 0)
    step(qi, True)

    lam = (jnp.exp(jnp.sum(lq1_ref[...] * lk1_ref[...], keepdims=True))
           - jnp.exp(jnp.sum(lq2_ref[...] * lk2_ref[...], keepdims=True)) + lam_init)
    o = acc_sc[0] / l_sc[0] - lam * (acc_sc[1] / l_sc[1])
    ms = jnp.mean(o * o, axis=-1, keepdims=True)
    o_ref[0] = (o * lax.rsqrt(ms + EPS) * sub_ref[...] * (1.0 - lam_init)).astype(BF16)


def _attn(q, k, v, lq1, lk1, lq2, lk2, sub, lam_init, tq):
    b, s, _ = q.shape
    qblk = pl.BlockSpec((1, tq, HEAD), lambda bi, hi, qi: (bi, qi, hi))
    kvblk = pl.BlockSpec((1, s, HEAD), lambda bi, hi, qi: (bi, 0, hi))
    const = lambda bi, hi, qi: (0, 0)
    lamspec = pl.BlockSpec((1, DIFF_HEAD_DIM), const)
    return pl.pallas_call(
        functools.partial(_attn_kernel, tq=tq, lam_init=lam_init),
        grid=(b, N_HEADS, s // tq),
        in_specs=[qblk, kvblk, kvblk, lamspec, lamspec, lamspec, lamspec, pl.BlockSpec((1, HEAD), const)],
        out_specs=qblk,
        out_shape=jax.ShapeDtypeStruct((b, s, HGRN_WIDTH), BF16),
        scratch_shapes=[pltpu.VMEM((2, tq, 1), F32), pltpu.VMEM((2, tq, 1), F32), pltpu.VMEM((2, tq, HEAD), F32)],
        compiler_params=_params(("parallel", "parallel", "arbitrary")),
        name="diffattn",
    )(q, k, v, lq1, lk1, lq2, lk2, sub)


def _outproj_kernel(oh_ref, od_ref, h_ref, wo_ref, fg_ref, wr_hi_ref, wr_lo_ref, h1_ref, xn_ref, route_ref):
    hw = HGRN_WIDTH
    h1 = h_ref[...] + _dot(oh_ref[...], wo_ref[0:hw, :]) + _dot(od_ref[...], wo_ref[hw:2 * hw, :])
    h1_ref[...] = h1
    ms = jnp.mean(h1 * h1, axis=-1, keepdims=True)
    xn = h1 * lax.rsqrt(ms + EPS) * fg_ref[...]
    xn_ref[...] = xn
    x_hi, x_lo = _split_bf16(xn)
    logits = _dot(x_hi, wr_hi_ref[...]) + _dot(x_hi, wr_lo_ref[...]) + _dot(x_lo, wr_hi_ref[...])
    lane = lax.broadcasted_iota(jnp.int32, logits.shape, 1)
    lanef = lane.astype(F32)
    valid = lane < N_EXPERTS
    logits = jnp.where(valid, logits, MASK_VALUE)
    ex = jnp.where(valid, jnp.exp(logits - jnp.max(logits, axis=-1, keepdims=True)), 0.0)
    aff = ex / jnp.sum(ex, axis=-1, keepdims=True)
    group = lane // EXPERTS_PER_GROUP

    def top2(a):
        m1 = jnp.max(a, axis=-1, keepdims=True)
        i1 = jnp.min(jnp.where(a == m1, lanef, float(LANES)), axis=-1, keepdims=True)
        a2 = jnp.where(lanef == i1, -1.0, a)
        m2 = jnp.max(a2, axis=-1, keepdims=True)
        i2 = jnp.min(jnp.where(a2 == m2, lanef, float(LANES)), axis=-1, keepdims=True)
        return m1, i1, m2, i2

    best = None
    for g in range(N_GROUPS):
        cand = top2(jnp.where((group == g) & valid, aff, -1.0))
        score = cand[0] + cand[2]
        if best is None:
            best, best_score = cand, score
        else:
            better = score > best_score
            best = tuple(jnp.where(better, c, b) for c, b in zip(cand, best))
            best_score = jnp.where(better, score, best_score)
    m1, i1, m2, i2 = best
    denom = m1 + m2
    route = jnp.where(lane == 0, m1 / denom,
                      jnp.where(lane == 1, m2 / denom,
                                jnp.where(lane == 2, i1, jnp.where(lane == 3, i2, 0.0))))
    route_ref[...] = route


def _outproj(oh, od, h, wo_bf, fg, wr_hi, wr_lo, tm):
    t = h.shape[0]
    row = lambda i: (i, 0)
    const = lambda i: (0, 0)
    return pl.pallas_call(
        _outproj_kernel,
        grid=(t // tm,),
        in_specs=[
            pl.BlockSpec((tm, HGRN_WIDTH), row),
            pl.BlockSpec((tm, HGRN_WIDTH), row),
            pl.BlockSpec((tm, D_MODEL), row),
            pl.BlockSpec((D_MODEL, D_MODEL), const),
            pl.BlockSpec((1, D_MODEL), const),
            pl.BlockSpec((D_MODEL, LANES), const),
            pl.BlockSpec((D_MODEL, LANES), const),
        ],
        out_specs=[pl.BlockSpec((tm, D_MODEL), row), pl.BlockSpec((tm, D_MODEL), row), pl.BlockSpec((tm, LANES), row)],
        out_shape=[jax.ShapeDtypeStruct((t, D_MODEL), F32), jax.ShapeDtypeStruct((t, D_MODEL), F32),
                   jax.ShapeDtypeStruct((t, LANES), F32)],
        compiler_params=_params(("parallel",)),
        name="outproj_router",
    )(oh, od, h, wo_bf, fg, wr_hi, wr_lo)


def _row_copy(src, src_row, dst, dst_row, sem):
    return pltpu.make_async_copy(src.at[pl.ds(src_row, 1)], dst.at[pl.ds(dst_row, 1)], sem)


def _dispatch_kernel(dest_ref, xn_hbm, rows_in_hbm, rows_hbm, sem, *, tm):
    del rows_in_hbm
    base = pl.program_id(0) * tm

    def issue(j, carry):
        for kk in range(TOP_K):
            _row_copy(xn_hbm, base + j, rows_hbm, dest_ref[0, 0, TOP_K * j + kk], sem).start()
        return carry

    lax.fori_loop(0, tm, issue, 0)

    def drain(j, carry):
        for kk in range(TOP_K):
            _row_copy(xn_hbm, 0, rows_hbm, 0, sem).wait()
        return carry

    lax.fori_loop(0, tm, drain, 0)


def _dispatch(dest3, xn, n_rows, tm):
    t = xn.shape[0]
    zeros = jnp.zeros((n_rows, D_MODEL), F32)
    return pl.pallas_call(
        functools.partial(_dispatch_kernel, tm=tm),
        grid=(t // tm,),
        in_specs=[pl.BlockSpec((1, 1, TOP_K * tm), lambda i: (i, 0, 0), memory_space=pltpu.SMEM),
                  pl.BlockSpec(memory_space=pl.ANY),
                  pl.BlockSpec(memory_space=pl.ANY)],
        out_specs=pl.BlockSpec(memory_space=pl.ANY),
        out_shape=jax.ShapeDtypeStruct((n_rows, D_MODEL), F32),
        scratch_shapes=[pltpu.SemaphoreType.DMA(())],
        input_output_aliases={2: 0},
        compiler_params=pltpu.CompilerParams(dimension_semantics=("arbitrary",)),
        name="moe_dispatch",
    )(dest3, xn, zeros)


def _expert_kernel(blk_e_ref, x_ref, wg_ref, wu_ref, wd_ref, y_ref):
    del blk_e_ref
    x = x_ref[...].astype(BF16)
    g = _dot(x, wg_ref[0])
    u = _dot(x, wu_ref[0])
    mid = (g * jax.nn.sigmoid(g)) * u
    y_ref[...] = _dot(mid.astype(BF16), wd_ref[0])


def _experts(blk_e, x_rows, wg, wu, wd):
    n_rows = x_rows.shape[0]
    row = lambda i, be: (i, 0)
    wsel = lambda i, be: (be[i], 0, 0)
    return pl.pallas_call(
        _expert_kernel,
        grid_spec=pltpu.PrefetchScalarGridSpec(
            num_scalar_prefetch=1,
            grid=(n_rows // ROW_BLOCK,),
            in_specs=[pl.BlockSpec((ROW_BLOCK, D_MODEL), row),
                      pl.BlockSpec((1, D_MODEL, D_FF), wsel),
                      pl.BlockSpec((1, D_MODEL, D_FF), wsel),
                      pl.BlockSpec((1, D_FF, D_MODEL), wsel)],
            out_specs=pl.BlockSpec((ROW_BLOCK, D_MODEL), row),
        ),
        out_shape=jax.ShapeDtypeStruct((n_rows, D_MODEL), F32),
        compiler_params=_params(("arbitrary",)),
        name="moe_experts",
    )(blk_e, x_rows, wg, wu, wd)


def _combine_kernel(dest_ref, h1_ref, route_ref, p_ref, pp_ref, pn_ref, pg_ref, y_hbm, o_ref, ybuf, sem, *, tm):
    def issue(j, carry):
        for kk in range(TOP_K):
            _row_copy(y_hbm, dest_ref[0, 0, TOP_K * j + kk], ybuf.at[kk], j, sem).start()
        return carry

    lax.fori_loop(0, tm, issue, 0)

    e = _dot(p_ref[...].astype(BF16), pp_ref[...])
    ms = jnp.mean(e * e, axis=-1, keepdims=True)
    ple = e * lax.rsqrt(ms + EPS) * pn_ref[...]

    def drain(j, carry):
        for kk in range(TOP_K):
            _row_copy(y_hbm, 0, ybuf.at[kk], 0, sem).wait()
        return carry

    lax.fori_loop(0, tm, drain, 0)

    route = route_ref[...]
    h2 = h1_ref[...] + (route[:, 0:1] * ybuf[0] + route[:, 1:2] * ybuf[1])
    gate = jax.nn.sigmoid(_dot(h2.astype(BF16), pg_ref[...]))
    o_ref[...] = h2 + ple * gate


def _combine(dest3, h1, route, p, pp_bf, pn, pg_bf, y_rows, tm):
    t = h1.shape[0]
    row = lambda i: (i, 0)
    const = lambda i: (0, 0)
    return pl.pallas_call(
        functools.partial(_combine_kernel, tm=tm),
        grid=(t // tm,),
        in_specs=[pl.BlockSpec((1, 1, TOP_K * tm), lambda i: (i, 0, 0), memory_space=pltpu.SMEM),
                  pl.BlockSpec((tm, D_MODEL), row),
                  pl.BlockSpec((tm, LANES), row),
                  pl.BlockSpec((tm, PLE_DIM), row),
                  pl.BlockSpec((PLE_DIM, D_MODEL), const),
                  pl.BlockSpec((1, D_MODEL), const),
                  pl.BlockSpec((D_MODEL, D_MODEL), const),
                  pl.BlockSpec(memory_space=pl.ANY)],
        out_specs=pl.BlockSpec((tm, D_MODEL), row),
        out_shape=jax.ShapeDtypeStruct((t, D_MODEL), F32),
        scratch_shapes=[pltpu.VMEM((TOP_K, tm, D_MODEL), F32), pltpu.SemaphoreType.DMA(())],
        compiler_params=_params(("arbitrary",)),
        name="moe_combine_ple",
    )(dest3, h1, route, p, pp_bf, pn, pg_bf, y_rows)


def _routing_tables(route, n_rows):
    top_e = route[:, 2:4].astype(jnp.int32)
    flat_e = top_e.reshape(-1)
    onehot = (flat_e[:, None] == jnp.arange(N_EXPERTS, dtype=jnp.int32)[None, :]).astype(jnp.int32)
    csum = jnp.cumsum(onehot, axis=0)
    rank = jnp.sum(csum * onehot, axis=1) - 1
    counts = csum[-1]
    padded = (counts + ROW_BLOCK - 1) // ROW_BLOCK * ROW_BLOCK
    padded_end = jnp.cumsum(padded)
    padded_start = padded_end - padded
    dest = jnp.sum(onehot * padded_start[None, :], axis=1) + rank
    n_blk = n_rows // ROW_BLOCK
    blk_start = jnp.arange(n_blk, dtype=jnp.int32) * ROW_BLOCK
    blk_e = jnp.sum((blk_start[:, None] >= padded_end[None, :]).astype(jnp.int32), axis=1)
    blk_e = jnp.minimum(blk_e, N_EXPERTS - 1).astype(jnp.int32)
    return dest.astype(jnp.int32), blk_e


def _tile(n, pref):
    return pref if n % pref == 0 else n


def kernel(x, p, mix_norm, w_in, hgrn_lb, hgrn_out_norm, q_norm, k_norm, lam_q1, lam_k1, lam_q2, lam_k2,
           diff_subln, w_out, ffn_norm, w_router, w_gate, w_up, w_down, ple_proj, ple_norm, ple_gate):
    b, s, d = x.shape
    depth = w_in.shape[0]
    t = b * s
    tm = _tile(t, 512)
    ts = _tile(s, 1024)
    tq = _tile(s, 256)
    tmc = _tile(t, 256)
    n_assign = t * TOP_K
    n_rows = n_assign + N_EXPERTS * ROW_BLOCK

    lb_soft = jax.nn.softmax(hgrn_lb.astype(F32), axis=0)
    lower_bounds = jnp.cumsum(lb_soft, axis=0) - lb_soft[0]
    grp = np.kron(np.eye(HGRN_WIDTH // DIFF_HEAD_DIM), np.ones((DIFF_HEAD_DIM, DIFF_HEAD_DIM)))
    grp = jnp.asarray(grp, BF16)
    wr = jnp.pad(w_router.astype(F32), ((0, 0), (0, LANES - N_EXPERTS)))
    wr_hi = wr.astype(BF16)
    wr_lo = (wr - wr_hi.astype(F32)).astype(BF16)
    n_rep = HGRN_WIDTH // DIFF_HEAD_DIM

    h = x.reshape(t, d)
    for i in range(depth):
        lam_init = 0.8 - 0.6 * math.exp(-0.3 * i)
        gq = (jnp.tile(q_norm[i], n_rep) * DIFF_HEAD_DIM ** -0.5).reshape(1, HGRN_WIDTH)
        gk = jnp.tile(k_norm[i], n_rep).reshape(1, HGRN_WIDTH)
        zq, zf, zi, zg, dq, dk, dv = _inproj(h, mix_norm[i].reshape(1, d), w_in[i].astype(BF16), gq, gk, grp, tm)
        r3 = lambda a: a.reshape(b, s, HGRN_WIDTH)
        o_hgrn = _hgrn(r3(zq), r3(zf), r3(zi), r3(zg), lower_bounds[i].reshape(1, HGRN_WIDTH),
                       hgrn_out_norm[i].reshape(1, HEAD), ts)
        row64 = lambda a: a.reshape(1, DIFF_HEAD_DIM)
        o_diff = _attn(r3(dq), r3(dk), r3(dv), row64(lam_q1[i]), row64(lam_k1[i]), row64(lam_q2[i]),
                       row64(lam_k2[i]), diff_subln[i].reshape(1, HEAD), lam_init, tq)
        h1, xn, route = _outproj(o_hgrn.reshape(t, HGRN_WIDTH), o_diff.reshape(t, HGRN_WIDTH), h,
                                 w_out[i].astype(BF16), ffn_norm[i].reshape(1, d), wr_hi, wr_lo, tm)
        dest, blk_e = _routing_tables(route, n_rows)
        x_rows = _dispatch(dest.reshape(t // tm, 1, TOP_K * tm), xn, n_rows, tm)
        y_rows = _experts(blk_e, x_rows, w_gate[i].astype(BF16), w_up[i].astype(BF16), w_down[i].astype(BF16))
        h = _combine(dest.reshape(t // tmc, 1, TOP_K * tmc), h1, route, p[i].reshape(t, PLE_DIM),
                     ple_proj[i].astype(BF16), ple_norm[i].reshape(1, d), ple_gate[i].astype(BF16), y_rows, tmc)
    return h.reshape(b, s, d)
```

```python
import functools
import math

import numpy as np
import jax
import jax.numpy as jnp
from jax import lax
from jax.experimental import pallas as pl
from jax.experimental.pallas import tpu as pltpu

F32 = jnp.float32
BF16 = jnp.bfloat16

EPS = 1e-6
MIN_FORGET = 1e-6
MASK_VALUE = -1e30

D_MODEL = 1024
HGRN_WIDTH = 512
HEAD = 128
N_HEADS = 4
DIFF_HEAD_DIM = 64
PLE_DIM = 256
N_EXPERTS = 16
EXPERTS_PER_GROUP = 4
N_GROUPS = 4
TOP_K = 2
D_FF = 512
IN_COLS = 4 * HGRN_WIDTH + 3 * HGRN_WIDTH

LANES = 128
SUBLANES = 8
ROW_TILES = D_MODEL // LANES
CHUNK = 128
ROW_BLOCK = 256
VMEM_LIMIT = 48 * 1024 * 1024
MAX_UNSHIFTED_SCORE = 40.0

assert ROW_TILES == SUBLANES


def _dot(a, b):
    return jnp.dot(a, b, preferred_element_type=F32)


def _dot_nt(a, b):
    return lax.dot_general(a, b, (((1,), (1,)), ((), ())), preferred_element_type=F32)


def _split_bf16(x):
    hi = x.astype(BF16)
    lo = (x - hi.astype(F32)).astype(BF16)
    return hi, lo


def _params(sem, vmem=VMEM_LIMIT):
    return pltpu.CompilerParams(dimension_semantics=sem, vmem_limit_bytes=vmem)


def _store_token_tiles(ref, x, n):
    for a in range(ROW_TILES):
        ref[pl.ds(a, n, stride=ROW_TILES), :] = x[:, a * LANES:(a + 1) * LANES]


def _load_token_tiles(ref, n, base=0):
    return [ref[pl.ds(base + a, n, stride=ROW_TILES), :] for a in range(ROW_TILES)]


def _inproj_kernel(h_ref, g_ref, w_ref, gq_ref, gk_ref, grp_ref,
                   zq_ref, zf_ref, zi_ref, zg_ref, dq_ref, dk_ref, dv_ref):
    x = h_ref[...]
    ms = jnp.mean(x * x, axis=-1, keepdims=True)
    hn = (x * lax.rsqrt(ms + EPS) * g_ref[...]).astype(BF16)
    w = HGRN_WIDTH
    for j, o_ref in enumerate((zq_ref, zf_ref, zi_ref, zg_ref)):
        o_ref[...] = _dot(hn, w_ref[:, j * w:(j + 1) * w]).astype(BF16)
    for j, o_ref, gain_ref in ((4, dq_ref, gq_ref), (5, dk_ref, gk_ref)):
        z = _dot(hn, w_ref[:, j * w:(j + 1) * w])
        zz_hi, zz_lo = _split_bf16(z * z)
        ss = _dot(zz_hi, grp_ref[...]) + _dot(zz_lo, grp_ref[...])
        o_ref[...] = (z * lax.rsqrt(ss * (1.0 / DIFF_HEAD_DIM) + EPS) * gain_ref[...]).astype(BF16)
    dv_ref[...] = _dot(hn, w_ref[:, 6 * w:7 * w]).astype(BF16)


def _inproj(h, gain, w_in_bf, gq, gk, grp, tm):
    t = h.shape[0]
    w = HGRN_WIDTH
    row = lambda i: (i, 0)
    const = lambda i: (0, 0)
    out = jax.ShapeDtypeStruct((t, w), BF16)
    return pl.pallas_call(
        _inproj_kernel,
        grid=(t // tm,),
        in_specs=[
            pl.BlockSpec((tm, D_MODEL), row),
            pl.BlockSpec((1, D_MODEL), const),
            pl.BlockSpec((D_MODEL, IN_COLS), const),
            pl.BlockSpec((1, w), const),
            pl.BlockSpec((1, w), const),
            pl.BlockSpec((w, w), const),
        ],
        out_specs=[pl.BlockSpec((tm, w), row)] * 7,
        out_shape=[out] * 7,
        compiler_params=_params(("arbitrary",)),
        name="inproj",
    )(h, gain, w_in_bf, gq, gk, grp)


_PAIR_HALVES = (8, 16, 32, 64)
_N_EXP = 3 + len(_PAIR_HALVES)


def _hgrn_constants():
    c = CHUNK
    t = np.arange(c)[:, None]
    j = np.arange(c)[None, :]
    mats = [(j <= t).astype(np.float32), (j > t).astype(np.float32)]
    mid = t - t % 8 + 3
    m0 = np.where((t > mid) & (j > mid) & (j <= t), 1.0, 0.0) - np.where((t < mid) & (j > t) & (j <= mid), 1.0, 0.0)
    mats.append(m0.astype(np.float32))
    lvl = np.full((c, c), -1, np.int32)
    tt, ss = np.broadcast_arrays(t, j)
    lvl[(tt // 8 == ss // 8) & (ss <= tt)] = 0
    for n, m in enumerate(_PAIR_HALVES):
        b = t - t % (2 * m) + m - 1
        mats.append(np.where(t > b, (j > b) & (j <= t), (j > t) & (j <= b)).astype(np.float32))
        sel = (tt // (2 * m) == ss // (2 * m)) & (tt % (2 * m) >= m) & (ss % (2 * m) < m)
        lvl[sel] = n + 1
    return np.concatenate(mats, axis=0), lvl


def _hgrn_kernel(zq_ref, zf_ref, zi_ref, zg_ref, lb_ref, og_ref, m_ref, lvl_ref, o_ref, state_ref, *, n_chunks):
    @pl.when(pl.program_id(1) == 0)
    def _():
        state_ref[...] = jnp.zeros_like(state_ref)

    lb = lb_ref[...]
    og = og_ref[...]
    lvl = lvl_ref[...]
    c = CHUNK

    def chunk(ci, carry):
        sl = pl.ds(pl.multiple_of(ci * c, c), c)
        xq = zq_ref[0, sl, :].astype(F32)
        fl = zf_ref[0, sl, :].astype(F32)
        xg = zg_ref[0, sl, :].astype(F32)
        q_all = xq * jax.nn.sigmoid(xq)
        forget = lb + (1.0 - lb) * jax.nn.sigmoid(fl)
        logf = jnp.log(jnp.maximum(forget, MIN_FORGET))
        k_all = (1.0 - lb) * jax.nn.sigmoid(-fl)
        gate_all = xg * jax.nn.sigmoid(xg)
        lf_hi, lf_lo = _split_bf16(logf)
        e_all = _dot(m_ref[...], lf_hi) + _dot(m_ref[...], lf_lo)
        for hd in range(N_HEADS):
            hs = slice(hd * HEAD, (hd + 1) * HEAD)
            q = q_all[:, hs]
            k = k_all[:, hs]
            v_bf = zi_ref[0, sl, hs]
            e_cum = e_all[0:c, hs]
            e_out = e_all[c:2 * c, hs]
            e0 = e_all[2 * c:3 * c, hs]
            s = _dot_nt((q * jnp.exp(e0)).astype(BF16), (k * jnp.exp(-e0)).astype(BF16))
            s = jnp.where(lvl == 0, s, 0.0)
            for n in range(len(_PAIR_HALVES)):
                p = jnp.exp(e_all[(3 + n) * c:(4 + n) * c, hs])
                sn = _dot_nt((q * p).astype(BF16), (k * p).astype(BF16))
                s = jnp.where(lvl == n + 1, sn, s)
            state_t = state_ref[hd]
            o = _dot_nt((q * jnp.exp(e_cum)).astype(BF16), state_t.astype(BF16))
            o = o + _dot(s.astype(BF16), v_bf)
            kb = (k * jnp.exp(e_out)).astype(BF16)
            v_t = v_bf.astype(F32).T.astype(BF16)
            state_ref[hd] = state_t * jnp.exp(e_cum[c - 1:c, :]) + _dot(v_t, kb)
            ms = jnp.mean(o * o, axis=-1, keepdims=True)
            y = o * lax.rsqrt(ms + EPS) * og * gate_all[:, hs]
            o_ref[0, sl, hs] = y.astype(BF16)
        return carry

    lax.fori_loop(0, n_chunks, chunk, 0)


def _hgrn(zq, zf, zi, zg, lb, og, ts):
    b, s, _ = zq.shape
    mats, lvl = _hgrn_constants()
    blk = pl.BlockSpec((1, ts, HGRN_WIDTH), lambda bi, si: (bi, si, 0))
    const = lambda bi, si: (0, 0)
    return pl.pallas_call(
        functools.partial(_hgrn_kernel, n_chunks=ts // CHUNK),
        grid=(b, s // ts),
        in_specs=[blk, blk, blk, blk,
                  pl.BlockSpec((1, HGRN_WIDTH), const),
                  pl.BlockSpec((1, HEAD), const),
                  pl.BlockSpec((_N_EXP * CHUNK, CHUNK), const),
                  pl.BlockSpec((CHUNK, CHUNK), const)],
        out_specs=blk,
        out_shape=jax.ShapeDtypeStruct((b, s, HGRN_WIDTH), BF16),
        scratch_shapes=[pltpu.VMEM((N_HEADS, HEAD, HEAD), F32)],
        compiler_params=_params(("arbitrary", "arbitrary")),
        name="hgrn2",
    )(zq, zf, zi, zg, lb, og, jnp.asarray(mats, BF16), jnp.asarray(lvl))


def _attn_kernel(fast_ref, q_ref, k_ref, v_ref, lq1_ref, lk1_ref, lq2_ref, lk2_ref, sub_ref, o_ref,
                 m_sc, l_sc, acc_sc, *, tq, lam_init):
    qi = pl.program_id(2)
    q = q_ref[0]
    lane = lax.broadcasted_iota(jnp.int32, q.shape, 1)
    zero = jnp.zeros_like(q)
    q_maps = (jnp.where(lane < DIFF_HEAD_DIM, q, zero), jnp.where(lane >= DIFF_HEAD_DIM, q, zero))
    l_sc[...] = jnp.zeros_like(l_sc)
    acc_sc[...] = jnp.zeros_like(acc_sc)

    def lane_tile_sum(p):
        part = p[:, 0:LANES]
        for j in range(1, tq // LANES):
            part = part + p[:, j * LANES:(j + 1) * LANES]
        return part

    def scores(c, kb, masked):
        s = _dot_nt(q_maps[c], kb)
        if masked:
            row = lax.broadcasted_iota(jnp.int32, s.shape, 0)
            col = lax.broadcasted_iota(jnp.int32, s.shape, 1)
            s = jnp.where(col <= row, s, MASK_VALUE)
        return s

    def run(step):
        def body(ki, carry):
            step(ki, False)
            return carry
        lax.fori_loop(0, qi, body, 0)
        step(qi, True)

    @pl.when(fast_ref[0] == 1)
    def _():
        def step(ki, masked):
            sl = pl.ds(pl.multiple_of(ki * tq, tq), tq)
            kb = k_ref[0, sl, :]
            vb = v_ref[0, sl, :]
            for c in range(2):
                p = jnp.exp(scores(c, kb, masked))
                l_sc[c] += lane_tile_sum(p)
                acc_sc[c] += _dot(p.astype(BF16), vb)
        run(step)

    @pl.when(fast_ref[0] != 1)
    def _():
        m_sc[...] = jnp.full_like(m_sc, MASK_VALUE)

        def step(ki, masked):
            sl = pl.ds(pl.multiple_of(ki * tq, tq), tq)
            kb = k_ref[0, sl, :]
            vb = v_ref[0, sl, :]
            for c in range(2):
                s = scores(c, kb, masked)
                m_old = m_sc[c]
                m_new = jnp.maximum(m_old, jnp.max(s, axis=-1, keepdims=True))
                alpha = jnp.exp(m_old - m_new)
                p = jnp.exp(s - m_new)
                l_sc[c] = alpha * l_sc[c] + lane_tile_sum(p)
                acc_sc[c] = alpha * acc_sc[c] + _dot(p.astype(BF16), vb)
                m_sc[c] = m_new
        run(step)

    lam = (jnp.exp(jnp.sum(lq1_ref[...] * lk1_ref[...], keepdims=True))
           - jnp.exp(jnp.sum(lq2_ref[...] * lk2_ref[...], keepdims=True)) + lam_init)
    l0 = jnp.sum(l_sc[0], axis=-1, keepdims=True)
    l1 = jnp.sum(l_sc[1], axis=-1, keepdims=True)
    o = acc_sc[0] / l0 - lam * (acc_sc[1] / l1)
    ms = jnp.mean(o * o, axis=-1, keepdims=True)
    o_ref[0] = (o * lax.rsqrt(ms + EPS) * sub_ref[...] * (1.0 - lam_init)).astype(BF16)


def _attn(fast, q, k, v, lq1, lk1, lq2, lk2, sub, lam_init, tq):
    b, s, _ = q.shape
    qblk = pl.BlockSpec((1, tq, HEAD), lambda bi, hi, qi, f: (bi, qi, hi))
    kvblk = pl.BlockSpec((1, s, HEAD), lambda bi, hi, qi, f: (bi, 0, hi))
    const = lambda bi, hi, qi, f: (0, 0)
    lamspec = pl.BlockSpec((1, DIFF_HEAD_DIM), const)
    return pl.pallas_call(
        functools.partial(_attn_kernel, tq=tq, lam_init=lam_init),
        grid_spec=pltpu.PrefetchScalarGridSpec(
            num_scalar_prefetch=1,
            grid=(b, N_HEADS, s // tq),
            in_specs=[qblk, kvblk, kvblk, lamspec, lamspec, lamspec, lamspec, pl.BlockSpec((1, HEAD), const)],
            out_specs=qblk,
            scratch_shapes=[pltpu.VMEM((2, tq, 1), F32), pltpu.VMEM((2, tq, LANES), F32),
                            pltpu.VMEM((2, tq, HEAD), F32)],
        ),
        out_shape=jax.ShapeDtypeStruct((b, s, HGRN_WIDTH), BF16),
        compiler_params=_params(("arbitrary", "arbitrary", "arbitrary")),
        name="diffattn",
    )(fast, q, k, v, lq1, lk1, lq2, lk2, sub)


def _outproj_kernel(oh_ref, od_ref, h_ref, wo_ref, fg_ref, wr_hi_ref, wr_lo_ref, h1_ref, xn_ref, route_ref, *, tm):
    hw = HGRN_WIDTH
    h1 = h_ref[...] + _dot(oh_ref[...], wo_ref[0:hw, :]) + _dot(od_ref[...], wo_ref[hw:2 * hw, :])
    h1_ref[...] = h1
    ms = jnp.mean(h1 * h1, axis=-1, keepdims=True)
    xn = h1 * lax.rsqrt(ms + EPS) * fg_ref[...]
    _store_token_tiles(xn_ref, xn, tm)
    x_hi, x_lo = _split_bf16(xn)
    logits = _dot(x_hi, wr_hi_ref[...]) + _dot(x_hi, wr_lo_ref[...]) + _dot(x_lo, wr_hi_ref[...])
    lane = lax.broadcasted_iota(jnp.int32, logits.shape, 1)
    lanef = lane.astype(F32)
    valid = lane < N_EXPERTS
    logits = jnp.where(valid, logits, MASK_VALUE)
    ex = jnp.where(valid, jnp.exp(logits - jnp.max(logits, axis=-1, keepdims=True)), 0.0)
    aff = ex / jnp.sum(ex, axis=-1, keepdims=True)
    group = lane // EXPERTS_PER_GROUP

    def top2(a):
        m1 = jnp.max(a, axis=-1, keepdims=True)
        i1 = jnp.min(jnp.where(a == m1, lanef, float(LANES)), axis=-1, keepdims=True)
        a2 = jnp.where(lanef == i1, -1.0, a)
        m2 = jnp.max(a2, axis=-1, keepdims=True)
        i2 = jnp.min(jnp.where(a2 == m2, lanef, float(LANES)), axis=-1, keepdims=True)
        return m1, i1, m2, i2

    best = None
    for g in range(N_GROUPS):
        cand = top2(jnp.where((group == g) & valid, aff, -1.0))
        score = cand[0] + cand[2]
        if best is None:
            best, best_score = cand, score
        else:
            better = score > best_score
            best = tuple(jnp.where(better, c, b) for c, b in zip(cand, best))
            best_score = jnp.where(better, score, best_score)
    m1, i1, m2, i2 = best
    denom = m1 + m2
    route = jnp.where(lane == 0, m1 / denom,
                      jnp.where(lane == 1, m2 / denom,
                                jnp.where(lane == 2, i1, jnp.where(lane == 3, i2, 0.0))))
    route_ref[...] = route


def _outproj(oh, od, h, wo_bf, fg, wr_hi, wr_lo, tm):
    t = h.shape[0]
    row = lambda i: (i, 0)
    const = lambda i: (0, 0)
    return pl.pallas_call(
        functools.partial(_outproj_kernel, tm=tm),
        grid=(t // tm,),
        in_specs=[
            pl.BlockSpec((tm, HGRN_WIDTH), row),
            pl.BlockSpec((tm, HGRN_WIDTH), row),
            pl.BlockSpec((tm, D_MODEL), row),
            pl.BlockSpec((D_MODEL, D_MODEL), const),
            pl.BlockSpec((1, D_MODEL), const),
            pl.BlockSpec((D_MODEL, LANES), const),
            pl.BlockSpec((D_MODEL, LANES), const),
        ],
        out_specs=[pl.BlockSpec((tm, D_MODEL), row), pl.BlockSpec((tm * ROW_TILES, LANES), row),
                   pl.BlockSpec((tm, LANES), row)],
        out_shape=[jax.ShapeDtypeStruct((t, D_MODEL), F32), jax.ShapeDtypeStruct((t * ROW_TILES, LANES), F32),
                   jax.ShapeDtypeStruct((t, LANES), F32)],
        compiler_params=_params(("arbitrary",)),
        name="outproj_router",
    )(oh, od, h, wo_bf, fg, wr_hi, wr_lo)


def _tile_copy(src, src_row, dst, dst_row, sem):
    return pltpu.make_async_copy(src.at[pl.ds(pl.multiple_of(src_row * ROW_TILES, ROW_TILES), ROW_TILES)],
                                 dst.at[pl.ds(pl.multiple_of(dst_row * ROW_TILES, ROW_TILES), ROW_TILES)], sem)


def _gather_rows(idx_ref, n, src_hbm, dst, sem):
    def issue(j, carry):
        _tile_copy(src_hbm, idx_ref[0, 0, j], dst, j, sem).start()
        return carry
    lax.fori_loop(0, n, issue, 0, unroll=8)


def _wait_rows(n, src_hbm, dst, sem):
    pltpu.make_async_copy(src_hbm.at[pl.ds(0, n * ROW_TILES)], dst, sem).wait()


def _expert_kernel(blk_e_ref, tok_ref, tok_next_ref, xn_hbm, wg_ref, wu_ref, wd_ref, y_ref, xbuf, sem):
    del blk_e_ref
    i = pl.program_id(0)
    slot = i % 2

    @pl.when(i == 0)
    def _():
        _gather_rows(tok_ref, ROW_BLOCK, xn_hbm, xbuf.at[0], sem.at[0])

    @pl.when(i + 1 < pl.num_programs(0))
    def _():
        _gather_rows(tok_next_ref, ROW_BLOCK, xn_hbm, xbuf.at[1 - slot], sem.at[1 - slot])

    _wait_rows(ROW_BLOCK, xn_hbm, xbuf.at[slot], sem.at[slot])
    x = jnp.concatenate(_load_token_tiles(xbuf.at[slot], ROW_BLOCK), axis=1).astype(BF16)
    g = _dot(x, wg_ref[0])
    u = _dot(x, wu_ref[0])
    mid = (g * jax.nn.sigmoid(g)) * u
    _store_token_tiles(y_ref, _dot(mid.astype(BF16), wd_ref[0]), ROW_BLOCK)


def _experts(blk_e, row_tok, xn_tiles, wg, wu, wd):
    n_blk = row_tok.shape[0]
    wsel = lambda i, be: (be[i], 0, 0)
    return pl.pallas_call(
        _expert_kernel,
        grid_spec=pltpu.PrefetchScalarGridSpec(
            num_scalar_prefetch=1,
            grid=(n_blk,),
            in_specs=[pl.BlockSpec((1, 1, ROW_BLOCK), lambda i, be: (i, 0, 0), memory_space=pltpu.SMEM),
                      pl.BlockSpec((1, 1, ROW_BLOCK), lambda i, be: (jnp.minimum(i + 1, n_blk - 1), 0, 0),
                                   memory_space=pltpu.SMEM),
                      pl.BlockSpec(memory_space=pl.ANY),
                      pl.BlockSpec((1, D_MODEL, D_FF), wsel),
                      pl.BlockSpec((1, D_MODEL, D_FF), wsel),
                      pl.BlockSpec((1, D_FF, D_MODEL), wsel)],
            out_specs=pl.BlockSpec((ROW_BLOCK * ROW_TILES, LANES), lambda i, be: (i, 0)),
            scratch_shapes=[pltpu.VMEM((2, ROW_BLOCK * ROW_TILES, LANES), F32), pltpu.SemaphoreType.DMA((2,))],
        ),
        out_shape=jax.ShapeDtypeStruct((n_blk * ROW_BLOCK * ROW_TILES, LANES), F32),
        compiler_params=_params(("arbitrary",)),
        name="moe_experts",
    )(blk_e, row_tok, row_tok, xn_tiles, wg, wu, wd)


def _combine_kernel(dest_ref, dest_next_ref, h1_ref, route_ref, p_ref, pp_ref, pn_ref, pg_ref, y_hbm, o_ref,
                    ybuf, sem, *, tm):
    i = pl.program_id(0)
    slot = i % 2
    n = TOP_K * tm

    @pl.when(i == 0)
    def _():
        _gather_rows(dest_ref, n, y_hbm, ybuf.at[0], sem.at[0])

    @pl.when(i + 1 < pl.num_programs(0))
    def _():
        _gather_rows(dest_next_ref, n, y_hbm, ybuf.at[1 - slot], sem.at[1 - slot])

    e = _dot(p_ref[...].astype(BF16), pp_ref[...])
    ms = jnp.mean(e * e, axis=-1, keepdims=True)
    ple = e * lax.rsqrt(ms + EPS) * pn_ref[...]

    _wait_rows(n, y_hbm, ybuf.at[slot], sem.at[slot])
    route = route_ref[...]
    w0 = route[:, 0:1]
    w1 = route[:, 1:2]
    y0 = _load_token_tiles(ybuf.at[slot], tm)
    y1 = _load_token_tiles(ybuf.at[slot], tm, base=tm * ROW_TILES)
    moe = jnp.concatenate([w0 * a + w1 * b for a, b in zip(y0, y1)], axis=1)
    h2 = h1_ref[...] + moe
    gate = jax.nn.sigmoid(_dot(h2.astype(BF16), pg_ref[...]))
    o_ref[...] = h2 + ple * gate


def _combine(dest3, h1, route, p, pp_bf, pn, pg_bf, y_tiles, tm):
    t = h1.shape[0]
    n_tiles = t // tm
    row = lambda i: (i, 0)
    const = lambda i: (0, 0)
    n = TOP_K * tm
    return pl.pallas_call(
        functools.partial(_combine_kernel, tm=tm),
        grid=(n_tiles,),
        in_specs=[pl.BlockSpec((1, 1, n), lambda i: (i, 0, 0), memory_space=pltpu.SMEM),
                  pl.BlockSpec((1, 1, n), lambda i: (jnp.minimum(i + 1, n_tiles - 1), 0, 0), memory_space=pltpu.SMEM),
                  pl.BlockSpec((tm, D_MODEL), row),
                  pl.BlockSpec((tm, LANES), row),
                  pl.BlockSpec((tm, PLE_DIM), row),
                  pl.BlockSpec((PLE_DIM, D_MODEL), const),
                  pl.BlockSpec((1, D_MODEL), const),
                  pl.BlockSpec((D_MODEL, D_MODEL), const),
                  pl.BlockSpec(memory_space=pl.ANY)],
        out_specs=pl.BlockSpec((tm, D_MODEL), row),
        out_shape=jax.ShapeDtypeStruct((t, D_MODEL), F32),
        scratch_shapes=[pltpu.VMEM((2, n * ROW_TILES, LANES), F32), pltpu.SemaphoreType.DMA((2,))],
        compiler_params=_params(("arbitrary",)),
        name="moe_combine_ple",
    )(dest3, dest3, h1, route, p, pp_bf, pn, pg_bf, y_tiles)


def _routing_tables(route, n_rows):
    top_e = route[:, 2:2 + TOP_K].astype(jnp.int32)
    flat_e = top_e.reshape(-1)
    n_assign = flat_e.shape[0]
    experts = jnp.arange(N_EXPERTS, dtype=jnp.int32)
    onehot = (flat_e[:, None] == experts[None, :]).astype(jnp.int32)
    csum = jnp.cumsum(onehot, axis=0)
    rank = jnp.sum(csum * onehot, axis=1) - 1
    counts = csum[-1]
    starts = jnp.cumsum(counts) - counts
    padded = (counts + ROW_BLOCK - 1) // ROW_BLOCK * ROW_BLOCK
    padded_end = jnp.cumsum(padded)
    padded_start = padded_end - padded
    dest = jnp.sum(onehot * padded_start[None, :], axis=1) + rank
    n_blk = n_rows // ROW_BLOCK
    blk_start = jnp.arange(n_blk, dtype=jnp.int32) * ROW_BLOCK
    blk_e = jnp.sum((blk_start[:, None] >= padded_end[None, :]).astype(jnp.int32), axis=1)
    blk_e = jnp.minimum(blk_e, N_EXPERTS - 1).astype(jnp.int32)
    order = jnp.argsort(flat_e, stable=True).astype(jnp.int32)
    row = jnp.arange(n_rows, dtype=jnp.int32)
    row_e = jnp.repeat(blk_e, ROW_BLOCK)
    idx = row - padded_start[row_e]
    src = order[jnp.clip(starts[row_e] + idx, 0, n_assign - 1)]
    row_tok = jnp.where(idx < counts[row_e], src // TOP_K, 0)
    return dest.astype(jnp.int32), row_tok.astype(jnp.int32), blk_e


def _tile(n, pref):
    return pref if n % pref == 0 else n


def kernel(x, p, mix_norm, w_in, hgrn_lb, hgrn_out_norm, q_norm, k_norm, lam_q1, lam_k1, lam_q2, lam_k2,
           diff_subln, w_out, ffn_norm, w_router, w_gate, w_up, w_down, ple_proj, ple_norm, ple_gate):
    b, s, d = x.shape
    depth = w_in.shape[0]
    t = b * s
    tm = _tile(t, 512)
    ts = _tile(s, 1024)
    tq = _tile(s, 512)
    tmc = _tile(t, 256)
    n_assign = t * TOP_K
    n_rows = n_assign + N_EXPERTS * ROW_BLOCK

    lb_soft = jax.nn.softmax(hgrn_lb.astype(F32), axis=0)
    lower_bounds = jnp.cumsum(lb_soft, axis=0) - lb_soft[0]
    grp = np.kron(np.eye(HGRN_WIDTH // DIFF_HEAD_DIM), np.ones((DIFF_HEAD_DIM, DIFF_HEAD_DIM)))
    grp = jnp.asarray(grp, BF16)
    wr = jnp.pad(w_router.astype(F32), ((0, 0), (0, LANES - N_EXPERTS)))
    wr_hi = wr.astype(BF16)
    wr_lo = (wr - wr_hi.astype(F32)).astype(BF16)
    n_rep = HGRN_WIDTH // DIFF_HEAD_DIM

    h = x.reshape(t, d)
    for i in range(depth):
        lam_init = 0.8 - 0.6 * math.exp(-0.3 * i)
        gq = (jnp.tile(q_norm[i], n_rep) * DIFF_HEAD_DIM ** -0.5).reshape(1, HGRN_WIDTH)
        gk = jnp.tile(k_norm[i], n_rep).reshape(1, HGRN_WIDTH)
        score_bound = 1.02 * DIFF_HEAD_DIM ** 0.5 * jnp.max(jnp.abs(q_norm[i])) * jnp.max(jnp.abs(k_norm[i]))
        fast = (score_bound <= MAX_UNSHIFTED_SCORE).astype(jnp.int32).reshape(1)
        zq, zf, zi, zg, dq, dk, dv = _inproj(h, mix_norm[i].reshape(1, d), w_in[i].astype(BF16), gq, gk, grp, tm)
        r3 = lambda a: a.reshape(b, s, HGRN_WIDTH)
        o_hgrn = _hgrn(r3(zq), r3(zf), r3(zi), r3(zg), lower_bounds[i].reshape(1, HGRN_WIDTH),
                       hgrn_out_norm[i].reshape(1, HEAD), ts)
        row64 = lambda a: a.reshape(1, DIFF_HEAD_DIM)
        o_diff = _attn(fast, r3(dq), r3(dk), r3(dv), row64(lam_q1[i]), row64(lam_k1[i]), row64(lam_q2[i]),
                       row64(lam_k2[i]), diff_subln[i].reshape(1, HEAD), lam_init, tq)
        h1, xn_tiles, route = _outproj(o_hgrn.reshape(t, HGRN_WIDTH), o_diff.reshape(t, HGRN_WIDTH), h,
                                       w_out[i].astype(BF16), ffn_norm[i].reshape(1, d), wr_hi, wr_lo, tm)
        dest, row_tok, blk_e = _routing_tables(route, n_rows)
        y_tiles = _experts(blk_e, row_tok.reshape(n_rows // ROW_BLOCK, 1, ROW_BLOCK), xn_tiles,
                           w_gate[i].astype(BF16), w_up[i].astype(BF16), w_down[i].astype(BF16))
        dest3 = dest.reshape(t // tmc, tmc, TOP_K).transpose(0, 2, 1).reshape(t // tmc, 1, TOP_K * tmc)
        h = _combine(dest3, h1, route, p[i].reshape(t, PLE_DIM), ple_proj[i].astype(BF16),
                     ple_norm[i].reshape(1, d), ple_gate[i].astype(BF16), y_tiles, tmc)
    return h.reshape(b, s, d)
```

```python
import functools
import math

import numpy as np
import jax
import jax.numpy as jnp
from jax import lax
from jax.experimental import pallas as pl
from jax.experimental.pallas import tpu as pltpu

F32 = jnp.float32
BF16 = jnp.bfloat16

EPS = 1e-6
MIN_FORGET = 1e-6
MASK_VALUE = -1e30

D_MODEL = 1024
HGRN_WIDTH = 512
HEAD = 128
N_HEADS = 4
DIFF_HEAD_DIM = 64
PLE_DIM = 256
N_EXPERTS = 16
EXPERTS_PER_GROUP = 4
N_GROUPS = 4
TOP_K = 2
D_FF = 512
IN_COLS = 4 * HGRN_WIDTH + 3 * HGRN_WIDTH

LANES = 128
SUBLANES = 8
ROW_TILES = D_MODEL // LANES
CHUNK = 128
ROW_BLOCK = 256
GATHER_UNROLL = 8
VMEM_LIMIT = 48 * 1024 * 1024
MAX_UNSHIFTED_SCORE = 40.0

assert ROW_TILES == SUBLANES


def _dot(a, b):
    return jnp.dot(a, b, preferred_element_type=F32)


def _dot_nt(a, b):
    return lax.dot_general(a, b, (((1,), (1,)), ((), ())), preferred_element_type=F32)


def _split_bf16(x):
    hi = x.astype(BF16)
    lo = (x - hi.astype(F32)).astype(BF16)
    return hi, lo


def _params(sem, vmem=VMEM_LIMIT):
    return pltpu.CompilerParams(dimension_semantics=sem, vmem_limit_bytes=vmem)


def _store_token_tiles(ref, x, n):
    for a in range(ROW_TILES):
        ref[pl.ds(a, n, stride=ROW_TILES), :] = x[:, a * LANES:(a + 1) * LANES]


def _load_token_tiles(ref, n, base=0):
    return [ref[pl.ds(base + a, n, stride=ROW_TILES), :] for a in range(ROW_TILES)]


def _inproj_kernel(h_ref, g_ref, w_ref, wvt_ref, gq_ref, gk_ref, grp_ref,
                   zq_ref, zf_ref, zi_ref, zg_ref, dq_ref, dk_ref, dvt_ref):
    x = h_ref[...]
    ms = jnp.mean(x * x, axis=-1, keepdims=True)
    hn = (x * lax.rsqrt(ms + EPS) * g_ref[...]).astype(BF16)
    w = HGRN_WIDTH
    for j, o_ref in enumerate((zq_ref, zf_ref, zi_ref, zg_ref)):
        o_ref[...] = _dot(hn, w_ref[:, j * w:(j + 1) * w]).astype(BF16)
    for j, o_ref, gain_ref in ((4, dq_ref, gq_ref), (5, dk_ref, gk_ref)):
        z = _dot(hn, w_ref[:, j * w:(j + 1) * w])
        ss = _dot((z * z).astype(BF16), grp_ref[...])
        o_ref[...] = (z * lax.rsqrt(ss * (1.0 / DIFF_HEAD_DIM) + EPS) * gain_ref[...]).astype(BF16)
    dvt_ref[0] = _dot_nt(wvt_ref[...], hn).astype(BF16)


def _inproj(h, gain, w_main_bf, w_vt_bf, gq, gk, grp, tm):
    t = h.shape[0]
    w = HGRN_WIDTH
    row = lambda i: (i, 0)
    const = lambda i: (0, 0)
    out = jax.ShapeDtypeStruct((t, w), BF16)
    return pl.pallas_call(
        _inproj_kernel,
        grid=(t // tm,),
        in_specs=[
            pl.BlockSpec((tm, D_MODEL), row),
            pl.BlockSpec((1, D_MODEL), const),
            pl.BlockSpec((D_MODEL, IN_COLS - w), const),
            pl.BlockSpec((w, D_MODEL), const),
            pl.BlockSpec((1, w), const),
            pl.BlockSpec((1, w), const),
            pl.BlockSpec((w, w), const),
        ],
        out_specs=[pl.BlockSpec((tm, w), row)] * 6 + [pl.BlockSpec((1, w, tm), lambda i: (i, 0, 0))],
        out_shape=[out] * 6 + [jax.ShapeDtypeStruct((t // tm, w, tm), BF16)],
        compiler_params=_params(("arbitrary",)),
        name="inproj",
    )(h, gain, w_main_bf, w_vt_bf, gq, gk, grp)


_PAIR_HALVES = (8, 16, 32, 64)
_N_EXP = 3 + len(_PAIR_HALVES)


def _hgrn_constants():
    c = CHUNK
    t = np.arange(c)[:, None]
    j = np.arange(c)[None, :]
    mats = [(j <= t).astype(np.float32), (j > t).astype(np.float32)]
    mid = t - t % 8 + 3
    m0 = np.where((t > mid) & (j > mid) & (j <= t), 1.0, 0.0) - np.where((t < mid) & (j > t) & (j <= mid), 1.0, 0.0)
    mats.append(m0.astype(np.float32))
    lvl = np.full((c, c), -1, np.int32)
    tt, ss = np.broadcast_arrays(t, j)
    lvl[(tt // 8 == ss // 8) & (ss <= tt)] = 0
    for n, m in enumerate(_PAIR_HALVES):
        b = t - t % (2 * m) + m - 1
        mats.append(np.where(t > b, (j > b) & (j <= t), (j > t) & (j <= b)).astype(np.float32))
        sel = (tt // (2 * m) == ss // (2 * m)) & (tt % (2 * m) >= m) & (ss % (2 * m) < m)
        lvl[sel] = n + 1
    return np.concatenate(mats, axis=0), lvl


def _hgrn_kernel(zq_ref, zf_ref, zi_ref, zg_ref, lb_ref, og_ref, m_ref, lvl_ref, o_ref, state_ref, *, n_chunks):
    @pl.when(pl.program_id(1) == 0)
    def _():
        state_ref[...] = jnp.zeros_like(state_ref)

    lb = lb_ref[...]
    og = og_ref[...]
    lvl = lvl_ref[...]
    c = CHUNK

    def chunk(ci, carry):
        sl = pl.ds(pl.multiple_of(ci * c, c), c)
        xq = zq_ref[0, sl, :].astype(F32)
        fl = zf_ref[0, sl, :].astype(F32)
        xg = zg_ref[0, sl, :].astype(F32)
        q_all = xq * jax.nn.sigmoid(xq)
        forget = lb + (1.0 - lb) * jax.nn.sigmoid(fl)
        logf = jnp.log(jnp.maximum(forget, MIN_FORGET))
        k_all = (1.0 - lb) * jax.nn.sigmoid(-fl)
        gate_all = xg * jax.nn.sigmoid(xg)
        lf_hi, lf_lo = _split_bf16(logf)
        e_all = _dot(m_ref[...], lf_hi) + _dot(m_ref[...], lf_lo)
        for hd in range(N_HEADS):
            hs = slice(hd * HEAD, (hd + 1) * HEAD)
            q = q_all[:, hs]
            k = k_all[:, hs]
            v_bf = zi_ref[0, sl, hs]
            e_cum = e_all[0:c, hs]
            e_out = e_all[c:2 * c, hs]
            e0 = e_all[2 * c:3 * c, hs]
            s = _dot_nt((q * jnp.exp(e0)).astype(BF16), (k * jnp.exp(-e0)).astype(BF16))
            s = jnp.where(lvl == 0, s, 0.0)
            for n in range(len(_PAIR_HALVES)):
                p = jnp.exp(e_all[(3 + n) * c:(4 + n) * c, hs])
                sn = _dot_nt((q * p).astype(BF16), (k * p).astype(BF16))
                s = jnp.where(lvl == n + 1, sn, s)
            state_t = state_ref[hd]
            o = _dot_nt((q * jnp.exp(e_cum)).astype(BF16), state_t.astype(BF16))
            o = o + _dot(s.astype(BF16), v_bf)
            kb = (k * jnp.exp(e_out)).astype(BF16)
            v_t = v_bf.astype(F32).T.astype(BF16)
            state_ref[hd] = state_t * jnp.exp(e_cum[c - 1:c, :]) + _dot(v_t, kb)
            ms = jnp.mean(o * o, axis=-1, keepdims=True)
            y = o * lax.rsqrt(ms + EPS) * og * gate_all[:, hs]
            o_ref[0, sl, hs] = y.astype(BF16)
        return carry

    lax.fori_loop(0, n_chunks, chunk, 0)


def _hgrn(zq, zf, zi, zg, lb, og, ts):
    b, s, _ = zq.shape
    mats, lvl = _hgrn_constants()
    blk = pl.BlockSpec((1, ts, HGRN_WIDTH), lambda bi, si: (bi, si, 0))
    const = lambda bi, si: (0, 0)
    return pl.pallas_call(
        functools.partial(_hgrn_kernel, n_chunks=ts // CHUNK),
        grid=(b, s // ts),
        in_specs=[blk, blk, blk, blk,
                  pl.BlockSpec((1, HGRN_WIDTH), const),
                  pl.BlockSpec((1, HEAD), const),
                  pl.BlockSpec((_N_EXP * CHUNK, CHUNK), const),
                  pl.BlockSpec((CHUNK, CHUNK), const)],
        out_specs=blk,
        out_shape=jax.ShapeDtypeStruct((b, s, HGRN_WIDTH), BF16),
        scratch_shapes=[pltpu.VMEM((N_HEADS, HEAD, HEAD), F32)],
        compiler_params=_params(("arbitrary", "arbitrary")),
        name="hgrn2",
    )(zq, zf, zi, zg, lb, og, jnp.asarray(mats, BF16), jnp.asarray(lvl))


def _attn_kernel(fast_ref, q_ref, k_ref, vt_ref, lq1_ref, lk1_ref, lq2_ref, lk2_ref, sub_ref, o_ref,
                 m_sc, l_sc, acc_sc, *, tq, lam_init):
    qi = pl.program_id(2)
    q = q_ref[0]
    lane = lax.broadcasted_iota(jnp.int32, q.shape, 1)
    zero = jnp.zeros_like(q)
    q_maps = (jnp.where(lane < DIFF_HEAD_DIM, q, zero), jnp.where(lane >= DIFF_HEAD_DIM, q, zero))
    l_sc[...] = jnp.zeros_like(l_sc)
    acc_sc[...] = jnp.zeros_like(acc_sc)

    def sublane_tile_sum(p):
        return jnp.sum(p.reshape(tq // SUBLANES, SUBLANES, tq), axis=0)

    def scores(c, kb, masked):
        s = _dot_nt(kb, q_maps[c])
        if masked:
            key = lax.broadcasted_iota(jnp.int32, s.shape, 0)
            qry = lax.broadcasted_iota(jnp.int32, s.shape, 1)
            s = jnp.where(key <= qry, s, MASK_VALUE)
        return s

    def run(step):
        def body(ki, carry):
            step(ki, False)
            return carry
        lax.fori_loop(0, qi, body, 0)
        step(qi, True)

    @pl.when(fast_ref[0] == 1)
    def _():
        def steps(kis, masked):
            kbs = [k_ref[0, pl.ds(pl.multiple_of(ki * tq, tq), tq), :] for ki in kis]
            vts = [vt_ref[ki] for ki in kis]
            for c in range(2):
                ps = [jnp.exp(scores(c, kb, masked)) for kb in kbs]
                l_sc[c] += sum(sublane_tile_sum(p) for p in ps)
                acc_sc[c] += sum(_dot(vt, p.astype(BF16)) for vt, p in zip(vts, ps))

        def pair(i, carry):
            steps((2 * i, 2 * i + 1), False)
            return carry
        lax.fori_loop(0, qi // 2, pair, 0)

        @pl.when(qi % 2 == 1)
        def _():
            steps((qi - 1,), False)
        steps((qi,), True)

    @pl.when(fast_ref[0] != 1)
    def _():
        m_sc[...] = jnp.full_like(m_sc, MASK_VALUE)

        def step(ki, masked):
            kb = k_ref[0, pl.ds(pl.multiple_of(ki * tq, tq), tq), :]
            vt = vt_ref[ki]
            for c in range(2):
                s = scores(c, kb, masked)
                m_old = m_sc[c]
                m_new = jnp.maximum(m_old, jnp.max(s, axis=0, keepdims=True))
                alpha = jnp.exp(m_old - m_new)
                p = jnp.exp(s - m_new)
                l_sc[c] = alpha * l_sc[c] + sublane_tile_sum(p)
                acc_sc[c] = alpha * acc_sc[c] + _dot(vt, p.astype(BF16))
                m_sc[c] = m_new
        run(step)

    lam = (jnp.exp(jnp.sum(lq1_ref[...] * lk1_ref[...], keepdims=True))
           - jnp.exp(jnp.sum(lq2_ref[...] * lk2_ref[...], keepdims=True)) + lam_init)
    l0 = jnp.sum(l_sc[0], axis=0, keepdims=True)
    l1 = jnp.sum(l_sc[1], axis=0, keepdims=True)
    o_t = acc_sc[0] / l0 - lam * (acc_sc[1] / l1)
    ms = jnp.mean(o_t * o_t, axis=0, keepdims=True)
    o = (o_t * lax.rsqrt(ms + EPS)).T
    o_ref[0] = (o * sub_ref[...] * (1.0 - lam_init)).astype(BF16)


def _attn(fast, q, k, vt, lq1, lk1, lq2, lk2, sub, lam_init, tq):
    b, s, _ = q.shape
    n_kv = s // tq
    qblk = pl.BlockSpec((1, tq, HEAD), lambda bi, hi, qi, f: (bi, qi, hi))
    kvblk = pl.BlockSpec((1, s, HEAD), lambda bi, hi, qi, f: (bi, 0, hi))
    vtblk = pl.BlockSpec((n_kv, HEAD, tq), lambda bi, hi, qi, f: (bi, hi, 0))
    const = lambda bi, hi, qi, f: (0, 0)
    lamspec = pl.BlockSpec((1, DIFF_HEAD_DIM), const)
    return pl.pallas_call(
        functools.partial(_attn_kernel, tq=tq, lam_init=lam_init),
        grid_spec=pltpu.PrefetchScalarGridSpec(
            num_scalar_prefetch=1,
            grid=(b, N_HEADS, s // tq),
            in_specs=[qblk, kvblk, vtblk, lamspec, lamspec, lamspec, lamspec, pl.BlockSpec((1, HEAD), const)],
            out_specs=qblk,
            scratch_shapes=[pltpu.VMEM((2, 1, tq), F32), pltpu.VMEM((2, SUBLANES, tq), F32),
                            pltpu.VMEM((2, HEAD, tq), F32)],
        ),
        out_shape=jax.ShapeDtypeStruct((b, s, HGRN_WIDTH), BF16),
        compiler_params=_params(("arbitrary", "arbitrary", "arbitrary")),
        name="diffattn",
    )(fast, q, k, vt, lq1, lk1, lq2, lk2, sub)


def _outproj_kernel(oh_ref, od_ref, h_ref, wo_ref, fg_ref, wr_ref, h1_ref, xn_ref, route_ref, *, tm):
    n_sub = 2 if tm % (2 * LANES) == 0 else 1
    sub = tm // n_sub
    for si in range(n_sub):
        rs = slice(si * sub, (si + 1) * sub)
        _outproj_rows(oh_ref[rs, :], od_ref[rs, :], h_ref[rs, :], wo_ref, fg_ref, wr_ref,
                      h1_ref.at[rs], xn_ref.at[pl.ds(si * sub * ROW_TILES, sub * ROW_TILES)], route_ref.at[rs], sub)


def _outproj_rows(oh, od, h, wo_ref, fg_ref, wr_ref, h1_ref, xn_ref, route_ref, tm):
    hw = HGRN_WIDTH
    h1 = h + _dot(oh, wo_ref[0:hw, :]) + _dot(od, wo_ref[hw:2 * hw, :])
    h1_ref[...] = h1
    ms = jnp.mean(h1 * h1, axis=-1, keepdims=True)
    xn = h1 * lax.rsqrt(ms + EPS) * fg_ref[...]
    _store_token_tiles(xn_ref, xn, tm)
    x_hi, x_lo = _split_bf16(xn)
    a = _dot(x_hi, wr_ref[...])
    logits = a + pltpu.roll(a, LANES - N_EXPERTS, axis=1) + _dot(x_lo, wr_ref[...])
    lt = logits.T[0:N_EXPERTS, :]
    ex = jnp.exp(lt - jnp.max(lt, axis=0, keepdims=True))
    aff = ex / jnp.sum(ex, axis=0, keepdims=True)
    rows = [aff[e:e + 1, :] for e in range(N_EXPERTS)]

    def top2(a, first):
        m1, i1 = a[0], jnp.full_like(a[0], first)
        for j in range(1, len(a)):
            better = a[j] > m1
            m1 = jnp.where(better, a[j], m1)
            i1 = jnp.where(better, float(first + j), i1)
        m2, i2 = jnp.full_like(m1, -1.0), jnp.zeros_like(m1)
        for j in range(len(a)):
            cand = jnp.where(i1 == float(first + j), -1.0, a[j])
            better = cand > m2
            m2 = jnp.where(better, cand, m2)
            i2 = jnp.where(better, float(first + j), i2)
        return m1, i1, m2, i2

    best = None
    for g in range(N_GROUPS):
        first = g * EXPERTS_PER_GROUP
        cand = top2(rows[first:first + EXPERTS_PER_GROUP], first)
        score = cand[0] + cand[2]
        if best is None:
            best, best_score = cand, score
        else:
            better = score > best_score
            best = tuple(jnp.where(better, c, b) for c, b in zip(cand, best))
            best_score = jnp.where(better, score, best_score)
    m1, i1, m2, i2 = best
    denom = m1 + m2
    rid = lax.broadcasted_iota(jnp.int32, (SUBLANES, tm), 0)
    r8 = jnp.where(rid == 0, m1 / denom,
                   jnp.where(rid == 1, m2 / denom, jnp.where(rid == 2, i1, jnp.where(rid == 3, i2, 0.0))))
    route_t = jnp.concatenate([r8, jnp.zeros((LANES - SUBLANES, tm), F32)], axis=0)
    route_ref[...] = route_t.T


def _outproj(oh, od, h, wo_bf, fg, wr_pack, tm):
    t = h.shape[0]
    row = lambda i: (i, 0)
    const = lambda i: (0, 0)
    return pl.pallas_call(
        functools.partial(_outproj_kernel, tm=tm),
        grid=(t // tm,),
        in_specs=[
            pl.BlockSpec((tm, HGRN_WIDTH), row),
            pl.BlockSpec((tm, HGRN_WIDTH), row),
            pl.BlockSpec((tm, D_MODEL), row),
            pl.BlockSpec((D_MODEL, D_MODEL), const),
            pl.BlockSpec((1, D_MODEL), const),
            pl.BlockSpec((D_MODEL, LANES), const),
        ],
        out_specs=[pl.BlockSpec((tm, D_MODEL), row), pl.BlockSpec((tm * ROW_TILES, LANES), row),
                   pl.BlockSpec((tm, LANES), row)],
        out_shape=[jax.ShapeDtypeStruct((t, D_MODEL), F32), jax.ShapeDtypeStruct((t * ROW_TILES, LANES), F32),
                   jax.ShapeDtypeStruct((t, LANES), F32)],
        compiler_params=_params(("arbitrary",)),
        name="outproj_router",
    )(oh, od, h, wo_bf, fg, wr_pack)


def _tile_copy(src, src_row, dst, dst_row, sem):
    return pltpu.make_async_copy(src.at[pl.ds(pl.multiple_of(src_row * ROW_TILES, ROW_TILES), ROW_TILES)],
                                 dst.at[pl.ds(pl.multiple_of(dst_row * ROW_TILES, ROW_TILES), ROW_TILES)], sem)


def _gather_rows(idx_ref, n, src_hbm, dst, sem):
    def issue(jo, carry):
        for u in range(GATHER_UNROLL):
            j = jo * GATHER_UNROLL + u
            _tile_copy(src_hbm, idx_ref[0, 0, j], dst, j, sem).start(priority=u % 2)
        return carry
    lax.fori_loop(0, n // GATHER_UNROLL, issue, 0)


def _wait_rows(n, src_hbm, dst, sem):
    pltpu.make_async_copy(src_hbm.at[pl.ds(0, n * ROW_TILES)], dst, sem).wait()


def _expert_kernel(blk_e_ref, tok_ref, tok_next_ref, xn_hbm, wg_ref, wu_ref, wd_ref, y_ref, xbuf, sem):
    del blk_e_ref
    i = pl.program_id(0)
    slot = i % 2

    @pl.when(i == 0)
    def _():
        _gather_rows(tok_ref, ROW_BLOCK, xn_hbm, xbuf.at[0], sem.at[0])

    @pl.when(i + 1 < pl.num_programs(0))
    def _():
        _gather_rows(tok_next_ref, ROW_BLOCK, xn_hbm, xbuf.at[1 - slot], sem.at[1 - slot])

    _wait_rows(ROW_BLOCK, xn_hbm, xbuf.at[slot], sem.at[slot])
    x = jnp.concatenate(_load_token_tiles(xbuf.at[slot], ROW_BLOCK), axis=1).astype(BF16)
    g = _dot(x, wg_ref[0])
    u = _dot(x, wu_ref[0])
    mid = (g * jax.nn.sigmoid(g)) * u
    _store_token_tiles(y_ref, _dot(mid.astype(BF16), wd_ref[0]), ROW_BLOCK)


def _experts(blk_e, row_tok, xn_tiles, wg, wu, wd):
    n_blk = row_tok.shape[0]
    wsel = lambda i, be: (be[i], 0, 0)
    return pl.pallas_call(
        _expert_kernel,
        grid_spec=pltpu.PrefetchScalarGridSpec(
            num_scalar_prefetch=1,
            grid=(n_blk,),
            in_specs=[pl.BlockSpec((1, 1, ROW_BLOCK), lambda i, be: (i, 0, 0), memory_space=pltpu.SMEM),
                      pl.BlockSpec((1, 1, ROW_BLOCK), lambda i, be: (jnp.minimum(i + 1, n_blk - 1), 0, 0),
                                   memory_space=pltpu.SMEM),
                      pl.BlockSpec(memory_space=pl.ANY),
                      pl.BlockSpec((1, D_MODEL, D_FF), wsel),
                      pl.BlockSpec((1, D_MODEL, D_FF), wsel),
                      pl.BlockSpec((1, D_FF, D_MODEL), wsel)],
            out_specs=pl.BlockSpec((ROW_BLOCK * ROW_TILES, LANES), lambda i, be: (i, 0)),
            scratch_shapes=[pltpu.VMEM((2, ROW_BLOCK * ROW_TILES, LANES), F32), pltpu.SemaphoreType.DMA((2,))],
        ),
        out_shape=jax.ShapeDtypeStruct((n_blk * ROW_BLOCK * ROW_TILES, LANES), F32),
        compiler_params=_params(("arbitrary",)),
        name="moe_experts",
    )(blk_e, row_tok, row_tok, xn_tiles, wg, wu, wd)


def _combine_kernel(dest_ref, dest_next_ref, h1_ref, route_ref, p_ref, pp_ref, pn_ref, pg_ref, y_hbm, o_ref,
                    ybuf, sem, *, tm):
    i = pl.program_id(0)
    slot = i % 2
    n = TOP_K * tm

    @pl.when(i == 0)
    def _():
        _gather_rows(dest_ref, n, y_hbm, ybuf.at[0], sem.at[0])

    @pl.when(i + 1 < pl.num_programs(0))
    def _():
        _gather_rows(dest_next_ref, n, y_hbm, ybuf.at[1 - slot], sem.at[1 - slot])

    e = _dot(p_ref[...].astype(BF16), pp_ref[...])
    ms = jnp.mean(e * e, axis=-1, keepdims=True)
    ple = e * lax.rsqrt(ms + EPS) * pn_ref[...]

    _wait_rows(n, y_hbm, ybuf.at[slot], sem.at[slot])
    route = route_ref[...]
    w0 = route[:, 0:1]
    w1 = route[:, 1:2]
    y0 = _load_token_tiles(ybuf.at[slot], tm)
    y1 = _load_token_tiles(ybuf.at[slot], tm, base=tm * ROW_TILES)
    moe = jnp.concatenate([w0 * a + w1 * b for a, b in zip(y0, y1)], axis=1)
    h2 = h1_ref[...] + moe
    gate = jax.nn.sigmoid(_dot(h2.astype(BF16), pg_ref[...]))
    o_ref[...] = h2 + ple * gate


def _combine(dest3, h1, route, p, pp_bf, pn, pg_bf, y_tiles, tm):
    t = h1.shape[0]
    n_tiles = t // tm
    row = lambda i: (i, 0)
    const = lambda i: (0, 0)
    n = TOP_K * tm
    return pl.pallas_call(
        functools.partial(_combine_kernel, tm=tm),
        grid=(n_tiles,),
        in_specs=[pl.BlockSpec((1, 1, n), lambda i: (i, 0, 0), memory_space=pltpu.SMEM),
                  pl.BlockSpec((1, 1, n), lambda i: (jnp.minimum(i + 1, n_tiles - 1), 0, 0), memory_space=pltpu.SMEM),
                  pl.BlockSpec((tm, D_MODEL), row),
                  pl.BlockSpec((tm, LANES), row),
                  pl.BlockSpec((tm, PLE_DIM), row),
                  pl.BlockSpec((PLE_DIM, D_MODEL), const),
                  pl.BlockSpec((1, D_MODEL), const),
                  pl.BlockSpec((D_MODEL, D_MODEL), const),
                  pl.BlockSpec(memory_space=pl.ANY)],
        out_specs=pl.BlockSpec((tm, D_MODEL), row),
        out_shape=jax.ShapeDtypeStruct((t, D_MODEL), F32),
        scratch_shapes=[pltpu.VMEM((2, n * ROW_TILES, LANES), F32), pltpu.SemaphoreType.DMA((2,))],
        compiler_params=_params(("arbitrary",)),
        name="moe_combine_ple",
    )(dest3, dest3, h1, route, p, pp_bf, pn, pg_bf, y_tiles)


def _routing_tables(route, n_rows):
    top_e = route[:, 2:2 + TOP_K].astype(jnp.int32)
    flat_e = top_e.reshape(-1)
    n_assign = flat_e.shape[0]
    experts = jnp.arange(N_EXPERTS, dtype=jnp.int32)
    onehot = (flat_e[:, None] == experts[None, :]).astype(jnp.int32)
    csum = jnp.cumsum(onehot, axis=0)
    rank = jnp.sum(csum * onehot, axis=1) - 1
    counts = csum[-1]
    starts = jnp.cumsum(counts) - counts
    padded = (counts + ROW_BLOCK - 1) // ROW_BLOCK * ROW_BLOCK
    padded_end = jnp.cumsum(padded)
    padded_start = padded_end - padded
    dest = jnp.sum(onehot * padded_start[None, :], axis=1) + rank
    n_blk = n_rows // ROW_BLOCK
    blk_start = jnp.arange(n_blk, dtype=jnp.int32) * ROW_BLOCK
    blk_e = jnp.sum((blk_start[:, None] >= padded_end[None, :]).astype(jnp.int32), axis=1)
    blk_e = jnp.minimum(blk_e, N_EXPERTS - 1).astype(jnp.int32)
    order = jnp.argsort(flat_e, stable=True).astype(jnp.int32)
    row = jnp.arange(n_rows, dtype=jnp.int32)
    row_e = jnp.repeat(blk_e, ROW_BLOCK)
    idx = row - padded_start[row_e]
    src = order[jnp.clip(starts[row_e] + idx, 0, n_assign - 1)]
    row_tok = jnp.where(idx < counts[row_e], src // TOP_K, 0)
    return dest.astype(jnp.int32), row_tok.astype(jnp.int32), blk_e


def _tile(n, pref):
    return pref if n % pref == 0 else n


def kernel(x, p, mix_norm, w_in, hgrn_lb, hgrn_out_norm, q_norm, k_norm, lam_q1, lam_k1, lam_q2, lam_k2,
           diff_subln, w_out, ffn_norm, w_router, w_gate, w_up, w_down, ple_proj, ple_norm, ple_gate):
    b, s, d = x.shape
    depth = w_in.shape[0]
    t = b * s
    tm = _tile(t, 512)
    ts = _tile(s, 1024)
    tq = _tile(s, 512)
    tmc = _tile(t, 256)
    n_assign = t * TOP_K
    n_rows = n_assign + N_EXPERTS * ROW_BLOCK

    lb_soft = jax.nn.softmax(hgrn_lb.astype(F32), axis=0)
    lower_bounds = jnp.cumsum(lb_soft, axis=0) - lb_soft[0]
    grp = np.kron(np.eye(HGRN_WIDTH // DIFF_HEAD_DIM), np.ones((DIFF_HEAD_DIM, DIFF_HEAD_DIM)))
    grp = jnp.asarray(grp, BF16)
    wr_hi, wr_lo = _split_bf16(w_router.astype(F32))
    wr_pack = jnp.pad(jnp.concatenate([wr_hi, wr_lo], axis=1), ((0, 0), (0, LANES - 2 * N_EXPERTS)))
    n_rep = HGRN_WIDTH // DIFF_HEAD_DIM

    h = x.reshape(t, d)
    for i in range(depth):
        lam_init = 0.8 - 0.6 * math.exp(-0.3 * i)
        gq = (jnp.tile(q_norm[i], n_rep) * DIFF_HEAD_DIM ** -0.5).reshape(1, HGRN_WIDTH)
        gk = jnp.tile(k_norm[i], n_rep).reshape(1, HGRN_WIDTH)
        score_bound = 1.02 * DIFF_HEAD_DIM ** 0.5 * jnp.max(jnp.abs(q_norm[i])) * jnp.max(jnp.abs(k_norm[i]))
        fast = (score_bound <= MAX_UNSHIFTED_SCORE).astype(jnp.int32).reshape(1)
        w_bf = w_in[i].astype(BF16)
        n_main = IN_COLS - HGRN_WIDTH
        zq, zf, zi, zg, dq, dk, dvt = _inproj(h, mix_norm[i].reshape(1, d), w_bf[:, :n_main], w_bf[:, n_main:].T,
                                              gq, gk, grp, tq)
        r3 = lambda a: a.reshape(b, s, HGRN_WIDTH)
        o_hgrn = _hgrn(r3(zq), r3(zf), r3(zi), r3(zg), lower_bounds[i].reshape(1, HGRN_WIDTH),
                       hgrn_out_norm[i].reshape(1, HEAD), ts)
        row64 = lambda a: a.reshape(1, DIFF_HEAD_DIM)
        o_diff = _attn(fast, r3(dq), r3(dk), dvt, row64(lam_q1[i]), row64(lam_k1[i]), row64(lam_q2[i]),
                       row64(lam_k2[i]), diff_subln[i].reshape(1, HEAD), lam_init, tq)
        h1, xn_tiles, route = _outproj(o_hgrn.reshape(t, HGRN_WIDTH), o_diff.reshape(t, HGRN_WIDTH), h,
                                       w_out[i].astype(BF16), ffn_norm[i].reshape(1, d), wr_pack, tm)
        dest, row_tok, blk_e = _routing_tables(route, n_rows)
        y_tiles = _experts(blk_e, row_tok.reshape(n_rows // ROW_BLOCK, 1, ROW_BLOCK), xn_tiles,
                           w_gate[i].astype(BF16), w_up[i].astype(BF16), w_down[i].astype(BF16))
        dest3 = dest.reshape(t // tmc, tmc, TOP_K).transpose(0, 2, 1).reshape(t // tmc, 1, TOP_K * tmc)
        h = _combine(dest3, h1, route, p[i].reshape(t, PLE_DIM), ple_proj[i].astype(BF16),
                     ple_norm[i].reshape(1, d), ple_gate[i].astype(BF16), y_tiles, tmc)
    return h.reshape(b, s, d)
```

```python
import functools
import math

import numpy as np
import jax
import jax.numpy as jnp
from jax import lax
from jax.experimental import pallas as pl
from jax.experimental.pallas import tpu as pltpu

F32 = jnp.float32
BF16 = jnp.bfloat16

EPS = 1e-6
MIN_FORGET = 1e-6
MASK_VALUE = -1e30

D_MODEL = 1024
HGRN_WIDTH = 512
HEAD = 128
N_HEADS = 4
DIFF_HEAD_DIM = 64
PLE_DIM = 256
N_EXPERTS = 16
EXPERTS_PER_GROUP = 4
N_GROUPS = 4
TOP_K = 2
D_FF = 512
IN_COLS = 4 * HGRN_WIDTH + 3 * HGRN_WIDTH

LANES = 128
SUBLANES = 8
ROW_TILES = D_MODEL // LANES
CHUNK = 128
ROW_BLOCK = 256
GATHER_UNROLL = 8
VMEM_LIMIT = 48 * 1024 * 1024
MAX_UNSHIFTED_SCORE = 40.0

assert ROW_TILES == SUBLANES


def _dot(a, b):
    return jnp.dot(a, b, preferred_element_type=F32)


def _dot_nt(a, b):
    return lax.dot_general(a, b, (((1,), (1,)), ((), ())), preferred_element_type=F32)


def _split_bf16(x):
    hi = x.astype(BF16)
    lo = (x - hi.astype(F32)).astype(BF16)
    return hi, lo


def _params(sem, vmem=VMEM_LIMIT):
    return pltpu.CompilerParams(dimension_semantics=sem, vmem_limit_bytes=vmem)


def _store_token_tiles(ref, x, n):
    for a in range(ROW_TILES):
        ref[pl.ds(a, n, stride=ROW_TILES), :] = x[:, a * LANES:(a + 1) * LANES]


def _load_token_tiles(ref, n, base=0):
    return [ref[pl.ds(base + a, n, stride=ROW_TILES), :] for a in range(ROW_TILES)]


def _inproj_kernel(h_ref, g_ref, w_ref, wvt_ref, gq_ref, gk_ref, grp_ref,
                   zq_ref, zf_ref, zi_ref, zg_ref, dq_ref, dk_ref, dvt_ref):
    x = h_ref[...]
    ms = jnp.mean(x * x, axis=-1, keepdims=True)
    hn = (x * lax.rsqrt(ms + EPS) * g_ref[...]).astype(BF16)
    w = HGRN_WIDTH
    for j, o_ref in enumerate((zq_ref, zf_ref, zi_ref, zg_ref)):
        o_ref[...] = _dot(hn, w_ref[:, j * w:(j + 1) * w]).astype(BF16)
    for j, o_ref, gain_ref in ((4, dq_ref, gq_ref), (5, dk_ref, gk_ref)):
        z = _dot(hn, w_ref[:, j * w:(j + 1) * w])
        ss = _dot((z * z).astype(BF16), grp_ref[...])
        o_ref[...] = (z * lax.rsqrt(ss * (1.0 / DIFF_HEAD_DIM) + EPS) * gain_ref[...]).astype(BF16)
    dvt_ref[0] = _dot_nt(wvt_ref[...], hn).astype(BF16)


def _inproj(h, gain, w_main_bf, w_vt_bf, gq, gk, grp, tm):
    t = h.shape[0]
    w = HGRN_WIDTH
    row = lambda i: (i, 0)
    const = lambda i: (0, 0)
    out = jax.ShapeDtypeStruct((t, w), BF16)
    return pl.pallas_call(
        _inproj_kernel,
        grid=(t // tm,),
        in_specs=[
            pl.BlockSpec((tm, D_MODEL), row),
            pl.BlockSpec((1, D_MODEL), const),
            pl.BlockSpec((D_MODEL, IN_COLS - w), const),
            pl.BlockSpec((w, D_MODEL), const),
            pl.BlockSpec((1, w), const),
            pl.BlockSpec((1, w), const),
            pl.BlockSpec((w, w), const),
        ],
        out_specs=[pl.BlockSpec((tm, w), row)] * 6 + [pl.BlockSpec((1, w, tm), lambda i: (i, 0, 0))],
        out_shape=[out] * 6 + [jax.ShapeDtypeStruct((t // tm, w, tm), BF16)],
        compiler_params=_params(("arbitrary",)),
        name="inproj",
    )(h, gain, w_main_bf, w_vt_bf, gq, gk, grp)


_PAIR_HALVES = (8, 16, 32, 64)
_N_EXP = 3 + len(_PAIR_HALVES)


def _hgrn_constants():
    c = CHUNK
    t = np.arange(c)[:, None]
    j = np.arange(c)[None, :]
    mats = [(j <= t).astype(np.float32), (j > t).astype(np.float32)]
    mid = t - t % 8 + 3
    m0 = np.where((t > mid) & (j > mid) & (j <= t), 1.0, 0.0) - np.where((t < mid) & (j > t) & (j <= mid), 1.0, 0.0)
    mats.append(m0.astype(np.float32))
    lvl = np.full((c, c), -1, np.int32)
    tt, ss = np.broadcast_arrays(t, j)
    lvl[(tt // 8 == ss // 8) & (ss <= tt)] = 0
    for n, m in enumerate(_PAIR_HALVES):
        b = t - t % (2 * m) + m - 1
        mats.append(np.where(t > b, (j > b) & (j <= t), (j > t) & (j <= b)).astype(np.float32))
        sel = (tt // (2 * m) == ss // (2 * m)) & (tt % (2 * m) >= m) & (ss % (2 * m) < m)
        lvl[sel] = n + 1
    return np.concatenate(mats, axis=0), lvl


def _hgrn_kernel(zq_ref, zf_ref, zi_ref, zg_ref, lb_ref, og_ref, m_ref, lvl_ref, o_ref, state_ref, *, n_chunks):
    @pl.when(pl.program_id(1) == 0)
    def _():
        state_ref[...] = jnp.zeros_like(state_ref)

    lb = lb_ref[...]
    og = og_ref[...]
    lvl = lvl_ref[...]
    c = CHUNK

    def chunk(ci, carry):
        sl = pl.ds(pl.multiple_of(ci * c, c), c)
        xq = zq_ref[0, sl, :].astype(F32)
        fl = zf_ref[0, sl, :].astype(F32)
        xg = zg_ref[0, sl, :].astype(F32)
        q_all = xq * jax.nn.sigmoid(xq)
        forget = lb + (1.0 - lb) * jax.nn.sigmoid(fl)
        logf = jnp.log(jnp.maximum(forget, MIN_FORGET))
        k_all = (1.0 - lb) * jax.nn.sigmoid(-fl)
        gate_all = xg * jax.nn.sigmoid(xg)
        lf_hi, lf_lo = _split_bf16(logf)
        e_all = _dot(m_ref[...], lf_hi) + _dot(m_ref[...], lf_lo)
        for hd in range(N_HEADS):
            hs = slice(hd * HEAD, (hd + 1) * HEAD)
            q = q_all[:, hs]
            k = k_all[:, hs]
            v_bf = zi_ref[0, sl, hs]
            e_cum = e_all[0:c, hs]
            e_out = e_all[c:2 * c, hs]
            e0 = e_all[2 * c:3 * c, hs]
            s = _dot_nt((q * jnp.exp(e0)).astype(BF16), (k * jnp.exp(-e0)).astype(BF16))
            s = jnp.where(lvl == 0, s, 0.0)
            for n in range(len(_PAIR_HALVES)):
                p = jnp.exp(e_all[(3 + n) * c:(4 + n) * c, hs])
                sn = _dot_nt((q * p).astype(BF16), (k * p).astype(BF16))
                s = jnp.where(lvl == n + 1, sn, s)
            state_t = state_ref[hd]
            o = _dot_nt((q * jnp.exp(e_cum)).astype(BF16), state_t.astype(BF16))
            o = o + _dot(s.astype(BF16), v_bf)
            kb = (k * jnp.exp(e_out)).astype(BF16)
            v_t = v_bf.astype(F32).T.astype(BF16)
            state_ref[hd] = state_t * jnp.exp(e_cum[c - 1:c, :]) + _dot(v_t, kb)
            ms = jnp.mean(o * o, axis=-1, keepdims=True)
            y = o * lax.rsqrt(ms + EPS) * og * gate_all[:, hs]
            o_ref[0, sl, hs] = y.astype(BF16)
        return carry

    lax.fori_loop(0, n_chunks, chunk, 0)


def _hgrn(zq, zf, zi, zg, lb, og, ts):
    b, s, _ = zq.shape
    mats, lvl = _hgrn_constants()
    blk = pl.BlockSpec((1, ts, HGRN_WIDTH), lambda bi, si: (bi, si, 0))
    const = lambda bi, si: (0, 0)
    return pl.pallas_call(
        functools.partial(_hgrn_kernel, n_chunks=ts // CHUNK),
        grid=(b, s // ts),
        in_specs=[blk, blk, blk, blk,
                  pl.BlockSpec((1, HGRN_WIDTH), const),
                  pl.BlockSpec((1, HEAD), const),
                  pl.BlockSpec((_N_EXP * CHUNK, CHUNK), const),
                  pl.BlockSpec((CHUNK, CHUNK), const)],
        out_specs=blk,
        out_shape=jax.ShapeDtypeStruct((b, s, HGRN_WIDTH), BF16),
        scratch_shapes=[pltpu.VMEM((N_HEADS, HEAD, HEAD), F32)],
        compiler_params=_params(("arbitrary", "arbitrary")),
        name="hgrn2",
    )(zq, zf, zi, zg, lb, og, jnp.asarray(mats, BF16), jnp.asarray(lvl))


def _attn_kernel(fast_ref, q_ref, k_ref, vt_ref, lq1_ref, lk1_ref, lq2_ref, lk2_ref, sub_ref, o_ref,
                 m_sc, l_sc, acc_sc, *, tq, lam_init):
    qi = pl.program_id(1)
    lane = lax.broadcasted_iota(jnp.int32, (tq, HEAD), 1)
    q_maps = []
    for hd in range(N_HEADS):
        q = q_ref[0, :, hd * HEAD:(hd + 1) * HEAD]
        zero = jnp.zeros_like(q)
        q_maps.append((jnp.where(lane < DIFF_HEAD_DIM, q, zero), jnp.where(lane >= DIFF_HEAD_DIM, q, zero)))
    l_sc[...] = jnp.zeros_like(l_sc)
    acc_sc[...] = jnp.zeros_like(acc_sc)

    def sublane_tile_sum(p):
        return jnp.sum(p.reshape(tq // SUBLANES, SUBLANES, tq), axis=0)

    def scores(q_map, kb, masked):
        s = _dot_nt(kb, q_map)
        if masked:
            key = lax.broadcasted_iota(jnp.int32, s.shape, 0)
            qry = lax.broadcasted_iota(jnp.int32, s.shape, 1)
            s = jnp.where(key <= qry, s, MASK_VALUE)
        return s

    def run(update):
        def step(ki, masked):
            k_all = k_ref[0, pl.ds(pl.multiple_of(ki * tq, tq), tq), :]
            vt_all = vt_ref[ki]
            for hd in range(N_HEADS):
                kb = k_all[:, hd * HEAD:(hd + 1) * HEAD]
                vt = vt_all[hd * HEAD:(hd + 1) * HEAD, :]
                for c in range(2):
                    update(hd, c, scores(q_maps[hd][c], kb, masked), vt)

        def body(ki, carry):
            step(ki, False)
            return carry
        lax.fori_loop(0, qi, body, 0)
        step(qi, True)

    @pl.when(fast_ref[0] == 1)
    def _():
        def step(ki, masked):
            k_all = k_ref[0, pl.ds(pl.multiple_of(ki * tq, tq), tq), :]
            vt_all = vt_ref[ki]
            chains = [(hd, c) for hd in range(N_HEADS) for c in range(2)]
            ps = [jnp.exp(scores(q_maps[hd][c], k_all[:, hd * HEAD:(hd + 1) * HEAD], masked)) for hd, c in chains]
            for (hd, c), p in zip(chains, ps):
                l_sc[hd, c] += sublane_tile_sum(p)
                acc_sc[hd, c] += _dot(vt_all[hd * HEAD:(hd + 1) * HEAD, :], p.astype(BF16))

        def body(ki, carry):
            step(ki, False)
            return carry
        lax.fori_loop(0, qi, body, 0)
        step(qi, True)

    @pl.when(fast_ref[0] != 1)
    def _():
        m_sc[...] = jnp.full_like(m_sc, MASK_VALUE)

        def update(hd, c, s, vt):
            m_old = m_sc[hd, c]
            m_new = jnp.maximum(m_old, jnp.max(s, axis=0, keepdims=True))
            alpha = jnp.exp(m_old - m_new)
            p = jnp.exp(s - m_new)
            l_sc[hd, c] = alpha * l_sc[hd, c] + sublane_tile_sum(p)
            acc_sc[hd, c] = alpha * acc_sc[hd, c] + _dot(vt, p.astype(BF16))
            m_sc[hd, c] = m_new
        run(update)

    lam = (jnp.exp(jnp.sum(lq1_ref[...] * lk1_ref[...], keepdims=True))
           - jnp.exp(jnp.sum(lq2_ref[...] * lk2_ref[...], keepdims=True)) + lam_init)
    for hd in range(N_HEADS):
        l0 = jnp.sum(l_sc[hd, 0], axis=0, keepdims=True)
        l1 = jnp.sum(l_sc[hd, 1], axis=0, keepdims=True)
        o_t = acc_sc[hd, 0] / l0 - lam * (acc_sc[hd, 1] / l1)
        ms = jnp.mean(o_t * o_t, axis=0, keepdims=True)
        o = (o_t * lax.rsqrt(ms + EPS)).T
        o_ref[0, :, hd * HEAD:(hd + 1) * HEAD] = (o * sub_ref[...] * (1.0 - lam_init)).astype(BF16)


def _attn(fast, q, k, vt, lq1, lk1, lq2, lk2, sub, lam_init, tq):
    b, s, w = q.shape
    n_kv = s // tq
    qblk = pl.BlockSpec((1, tq, w), lambda bi, qi, f: (bi, qi, 0))
    kblk = pl.BlockSpec((1, s, w), lambda bi, qi, f: (bi, 0, 0))
    vtblk = pl.BlockSpec((n_kv, w, tq), lambda bi, qi, f: (bi, 0, 0))
    const = lambda bi, qi, f: (0, 0)
    lamspec = pl.BlockSpec((1, DIFF_HEAD_DIM), const)
    return pl.pallas_call(
        functools.partial(_attn_kernel, tq=tq, lam_init=lam_init),
        grid_spec=pltpu.PrefetchScalarGridSpec(
            num_scalar_prefetch=1,
            grid=(b, s // tq),
            in_specs=[qblk, kblk, vtblk, lamspec, lamspec, lamspec, lamspec, pl.BlockSpec((1, HEAD), const)],
            out_specs=qblk,
            scratch_shapes=[pltpu.VMEM((N_HEADS, 2, 1, tq), F32), pltpu.VMEM((N_HEADS, 2, SUBLANES, tq), F32),
                            pltpu.VMEM((N_HEADS, 2, HEAD, tq), F32)],
        ),
        out_shape=jax.ShapeDtypeStruct((b, s, w), BF16),
        compiler_params=_params(("arbitrary", "arbitrary")),
        name="diffattn",
    )(fast, q, k, vt, lq1, lk1, lq2, lk2, sub)


def _outproj_kernel(oh_ref, od_ref, h_ref, wo_ref, fg_ref, wr_ref, h1_ref, xn_ref, route_ref, *, tm):
    n_sub = 2 if tm % (2 * LANES) == 0 else 1
    sub = tm // n_sub
    for si in range(n_sub):
        rs = slice(si * sub, (si + 1) * sub)
        _outproj_rows(oh_ref[rs, :], od_ref[rs, :], h_ref[rs, :], wo_ref, fg_ref, wr_ref,
                      h1_ref.at[rs], xn_ref.at[pl.ds(si * sub * ROW_TILES, sub * ROW_TILES)], route_ref.at[rs], sub)


def _outproj_rows(oh, od, h, wo_ref, fg_ref, wr_ref, h1_ref, xn_ref, route_ref, tm):
    hw = HGRN_WIDTH
    h1 = h + _dot(oh, wo_ref[0:hw, :]) + _dot(od, wo_ref[hw:2 * hw, :])
    h1_ref[...] = h1
    ms = jnp.mean(h1 * h1, axis=-1, keepdims=True)
    xn = h1 * lax.rsqrt(ms + EPS) * fg_ref[...]
    _store_token_tiles(xn_ref, xn, tm)
    x_hi, x_lo = _split_bf16(xn)
    a = _dot(x_hi, wr_ref[...])
    logits = a + pltpu.roll(a, LANES - N_EXPERTS, axis=1) + _dot(x_lo, wr_ref[...])
    lt = logits.T[0:N_EXPERTS, :]
    ex = jnp.exp(lt - jnp.max(lt, axis=0, keepdims=True))
    aff = ex / jnp.sum(ex, axis=0, keepdims=True)
    rows = [aff[e:e + 1, :] for e in range(N_EXPERTS)]

    def top2(a, first):
        m1, i1 = a[0], jnp.full_like(a[0], first)
        for j in range(1, len(a)):
            better = a[j] > m1
            m1 = jnp.where(better, a[j], m1)
            i1 = jnp.where(better, float(first + j), i1)
        m2, i2 = jnp.full_like(m1, -1.0), jnp.zeros_like(m1)
        for j in range(len(a)):
            cand = jnp.where(i1 == float(first + j), -1.0, a[j])
            better = cand > m2
            m2 = jnp.where(better, cand, m2)
            i2 = jnp.where(better, float(first + j), i2)
        return m1, i1, m2, i2

    best = None
    for g in range(N_GROUPS):
        first = g * EXPERTS_PER_GROUP
        cand = top2(rows[first:first + EXPERTS_PER_GROUP], first)
        score = cand[0] + cand[2]
        if best is None:
            best, best_score = cand, score
        else:
            better = score > best_score
            best = tuple(jnp.where(better, c, b) for c, b in zip(cand, best))
            best_score = jnp.where(better, score, best_score)
    m1, i1, m2, i2 = best
    denom = m1 + m2
    rid = lax.broadcasted_iota(jnp.int32, (SUBLANES, tm), 0)
    r8 = jnp.where(rid == 0, m1 / denom,
                   jnp.where(rid == 1, m2 / denom, jnp.where(rid == 2, i1, jnp.where(rid == 3, i2, 0.0))))
    route_t = jnp.concatenate([r8, jnp.zeros((LANES - SUBLANES, tm), F32)], axis=0)
    route_ref[...] = route_t.T


def _outproj(oh, od, h, wo_bf, fg, wr_pack, tm):
    t = h.shape[0]
    row = lambda i: (i, 0)
    const = lambda i: (0, 0)
    return pl.pallas_call(
        functools.partial(_outproj_kernel, tm=tm),
        grid=(t // tm,),
        in_specs=[
            pl.BlockSpec((tm, HGRN_WIDTH), row),
            pl.BlockSpec((tm, HGRN_WIDTH), row),
            pl.BlockSpec((tm, D_MODEL), row),
            pl.BlockSpec((D_MODEL, D_MODEL), const),
            pl.BlockSpec((1, D_MODEL), const),
            pl.BlockSpec((D_MODEL, LANES), const),
        ],
        out_specs=[pl.BlockSpec((tm, D_MODEL), row), pl.BlockSpec((tm * ROW_TILES, LANES), row),
                   pl.BlockSpec((tm, LANES), row)],
        out_shape=[jax.ShapeDtypeStruct((t, D_MODEL), F32), jax.ShapeDtypeStruct((t * ROW_TILES, LANES), F32),
                   jax.ShapeDtypeStruct((t, LANES), F32)],
        compiler_params=_params(("arbitrary",)),
        name="outproj_router",
    )(oh, od, h, wo_bf, fg, wr_pack)


def _tile_copy(src, src_row, dst, dst_row, sem):
    return pltpu.make_async_copy(src.at[pl.ds(pl.multiple_of(src_row * ROW_TILES, ROW_TILES), ROW_TILES)],
                                 dst.at[pl.ds(pl.multiple_of(dst_row * ROW_TILES, ROW_TILES), ROW_TILES)], sem)


def _gather_rows(idx_ref, n, src_hbm, dst, sem):
    def issue(jo, carry):
        for u in range(GATHER_UNROLL):
            j = jo * GATHER_UNROLL + u
            _tile_copy(src_hbm, idx_ref[0, 0, j], dst, j, sem).start(priority=u % 2)
        return carry
    lax.fori_loop(0, n // GATHER_UNROLL, issue, 0)


def _wait_rows(n, src_hbm, dst, sem):
    pltpu.make_async_copy(src_hbm.at[pl.ds(0, n * ROW_TILES)], dst, sem).wait()


def _expert_kernel(blk_e_ref, tok_ref, tok_next_ref, xn_hbm, wg_ref, wu_ref, wd_ref, y_ref, xbuf, sem):
    del blk_e_ref
    i = pl.program_id(0)
    slot = i % 2

    @pl.when(i == 0)
    def _():
        _gather_rows(tok_ref, ROW_BLOCK, xn_hbm, xbuf.at[0], sem.at[0])

    @pl.when(i + 1 < pl.num_programs(0))
    def _():
        _gather_rows(tok_next_ref, ROW_BLOCK, xn_hbm, xbuf.at[1 - slot], sem.at[1 - slot])

    _wait_rows(ROW_BLOCK, xn_hbm, xbuf.at[slot], sem.at[slot])
    x = jnp.concatenate(_load_token_tiles(xbuf.at[slot], ROW_BLOCK), axis=1).astype(BF16)
    g = _dot(x, wg_ref[0])
    u = _dot(x, wu_ref[0])
    mid = (g * jax.nn.sigmoid(g)) * u
    _store_token_tiles(y_ref, _dot(mid.astype(BF16), wd_ref[0]), ROW_BLOCK)


def _experts(blk_e, row_tok, xn_tiles, wg, wu, wd):
    n_blk = row_tok.shape[0]
    wsel = lambda i, be: (be[i], 0, 0)
    return pl.pallas_call(
        _expert_kernel,
        grid_spec=pltpu.PrefetchScalarGridSpec(
            num_scalar_prefetch=1,
            grid=(n_blk,),
            in_specs=[pl.BlockSpec((1, 1, ROW_BLOCK), lambda i, be: (i, 0, 0), memory_space=pltpu.SMEM),
                      pl.BlockSpec((1, 1, ROW_BLOCK), lambda i, be: (jnp.minimum(i + 1, n_blk - 1), 0, 0),
                                   memory_space=pltpu.SMEM),
                      pl.BlockSpec(memory_space=pl.ANY),
                      pl.BlockSpec((1, D_MODEL, D_FF), wsel),
                      pl.BlockSpec((1, D_MODEL, D_FF), wsel),
                      pl.BlockSpec((1, D_FF, D_MODEL), wsel)],
            out_specs=pl.BlockSpec((ROW_BLOCK * ROW_TILES, LANES), lambda i, be: (i, 0)),
            scratch_shapes=[pltpu.VMEM((2, ROW_BLOCK * ROW_TILES, LANES), F32), pltpu.SemaphoreType.DMA((2,))],
        ),
        out_shape=jax.ShapeDtypeStruct((n_blk * ROW_BLOCK * ROW_TILES, LANES), F32),
        compiler_params=_params(("arbitrary",)),
        name="moe_experts",
    )(blk_e, row_tok, row_tok, xn_tiles, wg, wu, wd)


def _combine_kernel(dest_ref, dest_next_ref, h1_ref, route_ref, p_ref, pp_ref, pn_ref, pg_ref, y_hbm, o_ref,
                    ybuf, sem, *, tm):
    i = pl.program_id(0)
    slot = i % 2
    n = TOP_K * tm

    @pl.when(i == 0)
    def _():
        _gather_rows(dest_ref, n, y_hbm, ybuf.at[0], sem.at[0])

    @pl.when(i + 1 < pl.num_programs(0))
    def _():
        _gather_rows(dest_next_ref, n, y_hbm, ybuf.at[1 - slot], sem.at[1 - slot])

    e = _dot(p_ref[...].astype(BF16), pp_ref[...])
    ms = jnp.mean(e * e, axis=-1, keepdims=True)
    ple = e * lax.rsqrt(ms + EPS) * pn_ref[...]

    _wait_rows(n, y_hbm, ybuf.at[slot], sem.at[slot])
    route = route_ref[...]
    w0 = route[:, 0:1]
    w1 = route[:, 1:2]
    y0 = _load_token_tiles(ybuf.at[slot], tm)
    y1 = _load_token_tiles(ybuf.at[slot], tm, base=tm * ROW_TILES)
    moe = jnp.concatenate([w0 * a + w1 * b for a, b in zip(y0, y1)], axis=1)
    h2 = h1_ref[...] + moe
    gate = jax.nn.sigmoid(_dot(h2.astype(BF16), pg_ref[...]))
    o_ref[...] = h2 + ple * gate


def _combine(dest3, h1, route, p, pp_bf, pn, pg_bf, y_tiles, tm):
    t = h1.shape[0]
    n_tiles = t // tm
    row = lambda i: (i, 0)
    const = lambda i: (0, 0)
    n = TOP_K * tm
    return pl.pallas_call(
        functools.partial(_combine_kernel, tm=tm),
        grid=(n_tiles,),
        in_specs=[pl.BlockSpec((1, 1, n), lambda i: (i, 0, 0), memory_space=pltpu.SMEM),
                  pl.BlockSpec((1, 1, n), lambda i: (jnp.minimum(i + 1, n_tiles - 1), 0, 0), memory_space=pltpu.SMEM),
                  pl.BlockSpec((tm, D_MODEL), row),
                  pl.BlockSpec((tm, LANES), row),
                  pl.BlockSpec((tm, PLE_DIM), row),
                  pl.BlockSpec((PLE_DIM, D_MODEL), const),
                  pl.BlockSpec((1, D_MODEL), const),
                  pl.BlockSpec((D_MODEL, D_MODEL), const),
                  pl.BlockSpec(memory_space=pl.ANY)],
        out_specs=pl.BlockSpec((tm, D_MODEL), row),
        out_shape=jax.ShapeDtypeStruct((t, D_MODEL), F32),
        scratch_shapes=[pltpu.VMEM((2, n * ROW_TILES, LANES), F32), pltpu.SemaphoreType.DMA((2,))],
        compiler_params=_params(("arbitrary",)),
        name="moe_combine_ple",
    )(dest3, dest3, h1, route, p, pp_bf, pn, pg_bf, y_tiles)


def _routing_tables(route, n_rows):
    top_e = route[:, 2:2 + TOP_K].astype(jnp.int32)
    flat_e = top_e.reshape(-1)
    n_assign = flat_e.shape[0]
    experts = jnp.arange(N_EXPERTS, dtype=jnp.int32)
    onehot = (flat_e[:, None] == experts[None, :]).astype(jnp.int32)
    csum = jnp.cumsum(onehot, axis=0)
    rank = jnp.sum(csum * onehot, axis=1) - 1
    counts = csum[-1]
    starts = jnp.cumsum(counts) - counts
    padded = (counts + ROW_BLOCK - 1) // ROW_BLOCK * ROW_BLOCK
    padded_end = jnp.cumsum(padded)
    padded_start = padded_end - padded
    dest = jnp.sum(onehot * padded_start[None, :], axis=1) + rank
    n_blk = n_rows // ROW_BLOCK
    blk_start = jnp.arange(n_blk, dtype=jnp.int32) * ROW_BLOCK
    blk_e = jnp.sum((blk_start[:, None] >= padded_end[None, :]).astype(jnp.int32), axis=1)
    blk_e = jnp.minimum(blk_e, N_EXPERTS - 1).astype(jnp.int32)
    order = jnp.argsort(flat_e, stable=True).astype(jnp.int32)
    row = jnp.arange(n_rows, dtype=jnp.int32)
    row_e = jnp.repeat(blk_e, ROW_BLOCK)
    idx = row - padded_start[row_e]
    src = order[jnp.clip(starts[row_e] + idx, 0, n_assign - 1)]
    row_tok = jnp.where(idx < counts[row_e], src // TOP_K, 0)
    return dest.astype(jnp.int32), row_tok.astype(jnp.int32), blk_e


def _tile(n, pref):
    return pref if n % pref == 0 else n


def kernel(x, p, mix_norm, w_in, hgrn_lb, hgrn_out_norm, q_norm, k_norm, lam_q1, lam_k1, lam_q2, lam_k2,
           diff_subln, w_out, ffn_norm, w_router, w_gate, w_up, w_down, ple_proj, ple_norm, ple_gate):
    b, s, d = x.shape
    depth = w_in.shape[0]
    t = b * s
    tm = _tile(t, 512)
    ts = _tile(s, 1024)
    tq = _tile(s, 512)
    tmc = _tile(t, 256)
    n_assign = t * TOP_K
    n_rows = n_assign + N_EXPERTS * ROW_BLOCK

    lb_soft = jax.nn.softmax(hgrn_lb.astype(F32), axis=0)
    lower_bounds = jnp.cumsum(lb_soft, axis=0) - lb_soft[0]
    grp = np.kron(np.eye(HGRN_WIDTH // DIFF_HEAD_DIM), np.ones((DIFF_HEAD_DIM, DIFF_HEAD_DIM)))
    grp = jnp.asarray(grp, BF16)
    wr_hi, wr_lo = _split_bf16(w_router.astype(F32))
    wr_pack = jnp.pad(jnp.concatenate([wr_hi, wr_lo], axis=1), ((0, 0), (0, LANES - 2 * N_EXPERTS)))
    n_rep = HGRN_WIDTH // DIFF_HEAD_DIM

    h = x.reshape(t, d)
    for i in range(depth):
        lam_init = 0.8 - 0.6 * math.exp(-0.3 * i)
        gq = (jnp.tile(q_norm[i], n_rep) * DIFF_HEAD_DIM ** -0.5).reshape(1, HGRN_WIDTH)
        gk = jnp.tile(k_norm[i], n_rep).reshape(1, HGRN_WIDTH)
        score_bound = 1.02 * DIFF_HEAD_DIM ** 0.5 * jnp.max(jnp.abs(q_norm[i])) * jnp.max(jnp.abs(k_norm[i]))
        fast = (score_bound <= MAX_UNSHIFTED_SCORE).astype(jnp.int32).reshape(1)
        w_bf = w_in[i].astype(BF16)
        n_main = IN_COLS - HGRN_WIDTH
        zq, zf, zi, zg, dq, dk, dvt = _inproj(h, mix_norm[i].reshape(1, d), w_bf[:, :n_main], w_bf[:, n_main:].T,
                                              gq, gk, grp, tq)
        r3 = lambda a: a.reshape(b, s, HGRN_WIDTH)
        o_hgrn = _hgrn(r3(zq), r3(zf), r3(zi), r3(zg), lower_bounds[i].reshape(1, HGRN_WIDTH),
                       hgrn_out_norm[i].reshape(1, HEAD), ts)
        row64 = lambda a: a.reshape(1, DIFF_HEAD_DIM)
        o_diff = _attn(fast, r3(dq), r3(dk), dvt, row64(lam_q1[i]), row64(lam_k1[i]), row64(lam_q2[i]),
                       row64(lam_k2[i]), diff_subln[i].reshape(1, HEAD), lam_init, tq)
        h1, xn_tiles, route = _outproj(o_hgrn.reshape(t, HGRN_WIDTH), o_diff.reshape(t, HGRN_WIDTH), h,
                                       w_out[i].astype(BF16), ffn_norm[i].reshape(1, d), wr_pack, tm)
        dest, row_tok, blk_e = _routing_tables(route, n_rows)
        y_tiles = _experts(blk_e, row_tok.reshape(n_rows // ROW_BLOCK, 1, ROW_BLOCK), xn_tiles,
                           w_gate[i].astype(BF16), w_up[i].astype(BF16), w_down[i].astype(BF16))
        dest3 = dest.reshape(t // tmc, tmc, TOP_K).transpose(0, 2, 1).reshape(t // tmc, 1, TOP_K * tmc)
        h = _combine(dest3, h1, route, p[i].reshape(t, PLE_DIM), ple_proj[i].astype(BF16),
                     ple_norm[i].reshape(1, d), ple_gate[i].astype(BF16), y_tiles, tmc)
    return h.reshape(b, s, d)
```

```python
import functools
import math

import numpy as np
import jax
import jax.numpy as jnp
from jax import lax
from jax.experimental import pallas as pl
from jax.experimental.pallas import tpu as pltpu

F32 = jnp.float32
BF16 = jnp.bfloat16

EPS = 1e-6
MIN_FORGET = 1e-6
MASK_VALUE = -1e30

D_MODEL = 1024
HGRN_WIDTH = 512
HEAD = 128
N_HEADS = 4
DIFF_HEAD_DIM = 64
PLE_DIM = 256
N_EXPERTS = 16
EXPERTS_PER_GROUP = 4
N_GROUPS = 4
TOP_K = 2
D_FF = 512
IN_COLS = 4 * HGRN_WIDTH + 3 * HGRN_WIDTH

LANES = 128
SUBLANES = 8
ROW_TILES = D_MODEL // LANES
CHUNK = 128
ROW_BLOCK = 256
GATHER_UNROLL = 8
GATHER_SLOTS = 3
VMEM_LIMIT = 48 * 1024 * 1024
MAX_UNSHIFTED_SCORE = 40.0

assert ROW_TILES == SUBLANES


def _dot(a, b):
    return jnp.dot(a, b, preferred_element_type=F32)


def _dot_nt(a, b):
    return lax.dot_general(a, b, (((1,), (1,)), ((), ())), preferred_element_type=F32)


def _split_bf16(x):
    hi = x.astype(BF16)
    lo = (x - hi.astype(F32)).astype(BF16)
    return hi, lo


def _params(sem, vmem=VMEM_LIMIT):
    return pltpu.CompilerParams(dimension_semantics=sem, vmem_limit_bytes=vmem)


def _store_token_tiles(ref, x, n):
    for a in range(ROW_TILES):
        ref[pl.ds(a, n, stride=ROW_TILES), :] = x[:, a * LANES:(a + 1) * LANES]


def _load_token_tiles(ref, n, base=0):
    return [ref[pl.ds(base + a, n, stride=ROW_TILES), :] for a in range(ROW_TILES)]


def _inproj_kernel(h_ref, g_ref, w_ref, wvt_ref, gq_ref, gk_ref, grp_ref,
                   zq_ref, zf_ref, zi_ref, zg_ref, dq_ref, dk_ref, dvt_ref):
    x = h_ref[...]
    ms = jnp.mean(x * x, axis=-1, keepdims=True)
    hn = (x * lax.rsqrt(ms + EPS) * g_ref[...]).astype(BF16)
    w = HGRN_WIDTH
    for j, o_ref in enumerate((zq_ref, zf_ref, zi_ref, zg_ref)):
        o_ref[...] = _dot(hn, w_ref[:, j * w:(j + 1) * w]).astype(BF16)
    for j, o_ref, gain_ref in ((4, dq_ref, gq_ref), (5, dk_ref, gk_ref)):
        z = _dot(hn, w_ref[:, j * w:(j + 1) * w])
        ss = _dot((z * z).astype(BF16), grp_ref[...])
        o_ref[...] = (z * lax.rsqrt(ss * (1.0 / DIFF_HEAD_DIM) + EPS) * gain_ref[...]).astype(BF16)
    dvt_ref[0] = _dot_nt(wvt_ref[...], hn).astype(BF16)


def _inproj(h, gain, w_main_bf, w_vt_bf, gq, gk, grp, tm):
    t = h.shape[0]
    w = HGRN_WIDTH
    row = lambda i: (i, 0)
    const = lambda i: (0, 0)
    out = jax.ShapeDtypeStruct((t, w), BF16)
    return pl.pallas_call(
        _inproj_kernel,
        grid=(t // tm,),
        in_specs=[
            pl.BlockSpec((tm, D_MODEL), row),
            pl.BlockSpec((1, D_MODEL), const),
            pl.BlockSpec((D_MODEL, IN_COLS - w), const),
            pl.BlockSpec((w, D_MODEL), const),
            pl.BlockSpec((1, w), const),
            pl.BlockSpec((1, w), const),
            pl.BlockSpec((w, w), const),
        ],
        out_specs=[pl.BlockSpec((tm, w), row)] * 6 + [pl.BlockSpec((1, w, tm), lambda i: (i, 0, 0))],
        out_shape=[out] * 6 + [jax.ShapeDtypeStruct((t // tm, w, tm), BF16)],
        compiler_params=_params(("arbitrary",)),
        name="inproj",
    )(h, gain, w_main_bf, w_vt_bf, gq, gk, grp)


_PAIR_HALVES = (8, 16, 32, 64)
_N_EXP = 3 + len(_PAIR_HALVES)


def _hgrn_constants():
    c = CHUNK
    t = np.arange(c)[:, None]
    j = np.arange(c)[None, :]
    mats = [(j <= t).astype(np.float32), (j > t).astype(np.float32)]
    mid = t - t % 8 + 3
    m0 = np.where((t > mid) & (j > mid) & (j <= t), 1.0, 0.0) - np.where((t < mid) & (j > t) & (j <= mid), 1.0, 0.0)
    mats.append(m0.astype(np.float32))
    lvl = np.full((c, c), -1, np.int32)
    tt, ss = np.broadcast_arrays(t, j)
    lvl[(tt // 8 == ss // 8) & (ss <= tt)] = 0
    for n, m in enumerate(_PAIR_HALVES):
        b = t - t % (2 * m) + m - 1
        mats.append(np.where(t > b, (j > b) & (j <= t), (j > t) & (j <= b)).astype(np.float32))
        sel = (tt // (2 * m) == ss // (2 * m)) & (tt % (2 * m) >= m) & (ss % (2 * m) < m)
        lvl[sel] = n + 1
    return np.concatenate(mats, axis=0), lvl


def _hgrn_kernel(zq_ref, zf_ref, zi_ref, zg_ref, lb_ref, og_ref, m_ref, lvl_ref, o_ref, state_ref, *, n_chunks):
    @pl.when(pl.program_id(1) == 0)
    def _():
        state_ref[...] = jnp.zeros_like(state_ref)

    lb = lb_ref[...]
    og = og_ref[...]
    lvl = lvl_ref[...]
    level_masks = [lvl == n for n in range(1 + len(_PAIR_HALVES))]
    c = CHUNK

    def chunk(ci, carry):
        sl = pl.ds(pl.multiple_of(ci * c, c), c)
        xq = zq_ref[0, sl, :].astype(F32)
        fl = zf_ref[0, sl, :].astype(F32)
        xg = zg_ref[0, sl, :].astype(F32)
        q_all = xq * jax.nn.sigmoid(xq)
        forget = lb + (1.0 - lb) * jax.nn.sigmoid(fl)
        logf = jnp.log(jnp.maximum(forget, MIN_FORGET))
        k_all = 1.0 - forget
        gate_all = xg * jax.nn.sigmoid(xg)
        lf_hi, lf_lo = _split_bf16(logf)
        e_all = _dot(m_ref[...], lf_hi) + _dot(m_ref[...], lf_lo)
        e0 = e_all[2 * c:3 * c]
        q_lv = [(q_all * jnp.exp(e0)).astype(BF16)]
        k_lv = [(k_all * jnp.exp(-e0)).astype(BF16)]
        for n in range(len(_PAIR_HALVES)):
            p = jnp.exp(e_all[(3 + n) * c:(4 + n) * c])
            q_lv.append((q_all * p).astype(BF16))
            k_lv.append((k_all * p).astype(BF16))
        p_cum = jnp.exp(e_all[0:c])
        q_in = (q_all * p_cum).astype(BF16)
        k_out = (k_all * jnp.exp(e_all[c:2 * c])).astype(BF16)
        heads = [slice(hd * HEAD, (hd + 1) * HEAD) for hd in range(N_HEADS)]
        sc = [[_dot_nt(q_lv[n][:, hs], k_lv[n][:, hs]) for n in range(len(q_lv))] for hs in heads]
        s_bf = []
        for per_level in sc:
            s = jnp.where(level_masks[0], per_level[0], 0.0)
            for n in range(1, len(per_level)):
                s = jnp.where(level_masks[n], per_level[n], s)
            s_bf.append(s.astype(BF16))
        v_bf = [zi_ref[0, sl, hs] for hs in heads]
        states = [state_ref[hd] for hd in range(N_HEADS)]
        outs = [_dot_nt(q_in[:, hs], states[hd].astype(BF16)) + _dot(s_bf[hd], v_bf[hd])
                for hd, hs in enumerate(heads)]
        for hd, hs in enumerate(heads):
            v_t = v_bf[hd].astype(F32).T.astype(BF16)
            state_ref[hd] = states[hd] * p_cum[c - 1:c, hs] + _dot(v_t, k_out[:, hs])
        for hd, hs in enumerate(heads):
            o = outs[hd]
            ms = jnp.mean(o * o, axis=-1, keepdims=True)
            y = o * lax.rsqrt(ms + EPS) * og * gate_all[:, hs]
            o_ref[0, sl, hs] = y.astype(BF16)
        return carry

    lax.fori_loop(0, n_chunks, chunk, 0, unroll=2)


def _hgrn(zq, zf, zi, zg, lb, og, ts):
    b, s, _ = zq.shape
    mats, lvl = _hgrn_constants()
    blk = pl.BlockSpec((1, ts, HGRN_WIDTH), lambda bi, si: (bi, si, 0))
    const = lambda bi, si: (0, 0)
    return pl.pallas_call(
        functools.partial(_hgrn_kernel, n_chunks=ts // CHUNK),
        grid=(b, s // ts),
        in_specs=[blk, blk, blk, blk,
                  pl.BlockSpec((1, HGRN_WIDTH), const),
                  pl.BlockSpec((1, HEAD), const),
                  pl.BlockSpec((_N_EXP * CHUNK, CHUNK), const),
                  pl.BlockSpec((CHUNK, CHUNK), const)],
        out_specs=blk,
        out_shape=jax.ShapeDtypeStruct((b, s, HGRN_WIDTH), BF16),
        scratch_shapes=[pltpu.VMEM((N_HEADS, HEAD, HEAD), F32)],
        compiler_params=_params(("arbitrary", "arbitrary")),
        name="hgrn2",
    )(zq, zf, zi, zg, lb, og, jnp.asarray(mats, BF16), jnp.asarray(lvl))


def _attn_kernel(fast_ref, q_ref, k_ref, vt_ref, lq1_ref, lk1_ref, lq2_ref, lk2_ref, sub_ref, o_ref,
                 m_sc, l_sc, acc_sc, *, tq, lam_init):
    qi = pl.program_id(1)
    lane = lax.broadcasted_iota(jnp.int32, (tq, HEAD), 1)
    q_maps = []
    for hd in range(N_HEADS):
        q = q_ref[0, :, hd * HEAD:(hd + 1) * HEAD]
        zero = jnp.zeros_like(q)
        q_maps.append((jnp.where(lane < DIFF_HEAD_DIM, q, zero), jnp.where(lane >= DIFF_HEAD_DIM, q, zero)))
    l_sc[...] = jnp.zeros_like(l_sc)
    acc_sc[...] = jnp.zeros_like(acc_sc)

    def sublane_tile_sum(p):
        return jnp.sum(p.reshape(tq // SUBLANES, SUBLANES, tq), axis=0)

    def scores(q_map, kb, masked):
        s = _dot_nt(kb, q_map)
        if masked:
            key = lax.broadcasted_iota(jnp.int32, s.shape, 0)
            qry = lax.broadcasted_iota(jnp.int32, s.shape, 1)
            s = jnp.where(key <= qry, s, MASK_VALUE)
        return s

    def run(update):
        def step(ki, masked):
            k_all = k_ref[0, pl.ds(pl.multiple_of(ki * tq, tq), tq), :]
            vt_all = vt_ref[ki]
            for hd in range(N_HEADS):
                kb = k_all[:, hd * HEAD:(hd + 1) * HEAD]
                vt = vt_all[hd * HEAD:(hd + 1) * HEAD, :]
                for c in range(2):
                    update(hd, c, scores(q_maps[hd][c], kb, masked), vt)

        def body(ki, carry):
            step(ki, False)
            return carry
        lax.fori_loop(0, qi, body, 0)
        step(qi, True)

    @pl.when(fast_ref[0] == 1)
    def _():
        def step(ki, masked):
            k_all = k_ref[0, pl.ds(pl.multiple_of(ki * tq, tq), tq), :]
            vt_all = vt_ref[ki]
            chains = [(hd, c) for hd in range(N_HEADS) for c in range(2)]
            ps = [jnp.exp(scores(q_maps[hd][c], k_all[:, hd * HEAD:(hd + 1) * HEAD], masked)) for hd, c in chains]
            for (hd, c), p in zip(chains, ps):
                l_sc[hd, c] += sublane_tile_sum(p)
                acc_sc[hd, c] += _dot(vt_all[hd * HEAD:(hd + 1) * HEAD, :], p.astype(BF16))

        def body(ki, carry):
            step(ki, False)
            return carry
        lax.fori_loop(0, qi, body, 0)
        step(qi, True)

    @pl.when(fast_ref[0] != 1)
    def _():
        m_sc[...] = jnp.full_like(m_sc, MASK_VALUE)

        def update(hd, c, s, vt):
            m_old = m_sc[hd, c]
            m_new = jnp.maximum(m_old, jnp.max(s, axis=0, keepdims=True))
            alpha = jnp.exp(m_old - m_new)
            p = jnp.exp(s - m_new)
            l_sc[hd, c] = alpha * l_sc[hd, c] + sublane_tile_sum(p)
            acc_sc[hd, c] = alpha * acc_sc[hd, c] + _dot(vt, p.astype(BF16))
            m_sc[hd, c] = m_new
        run(update)

    lam = (jnp.exp(jnp.sum(lq1_ref[...] * lk1_ref[...], keepdims=True))
           - jnp.exp(jnp.sum(lq2_ref[...] * lk2_ref[...], keepdims=True)) + lam_init)
    for hd in range(N_HEADS):
        l0 = jnp.sum(l_sc[hd, 0], axis=0, keepdims=True)
        l1 = jnp.sum(l_sc[hd, 1], axis=0, keepdims=True)
        o_t = acc_sc[hd, 0] / l0 - lam * (acc_sc[hd, 1] / l1)
        ms = jnp.mean(o_t * o_t, axis=0, keepdims=True)
        o = (o_t * lax.rsqrt(ms + EPS)).T
        o_ref[0, :, hd * HEAD:(hd + 1) * HEAD] = (o * sub_ref[...] * (1.0 - lam_init)).astype(BF16)


def _attn(fast, q, k, vt, lq1, lk1, lq2, lk2, sub, lam_init, tq):
    b, s, w = q.shape
    n_kv = s // tq
    qblk = pl.BlockSpec((1, tq, w), lambda bi, qi, f: (bi, qi, 0))
    kblk = pl.BlockSpec((1, s, w), lambda bi, qi, f: (bi, 0, 0))
    vtblk = pl.BlockSpec((n_kv, w, tq), lambda bi, qi, f: (bi, 0, 0))
    const = lambda bi, qi, f: (0, 0)
    lamspec = pl.BlockSpec((1, DIFF_HEAD_DIM), const)
    return pl.pallas_call(
        functools.partial(_attn_kernel, tq=tq, lam_init=lam_init),
        grid_spec=pltpu.PrefetchScalarGridSpec(
            num_scalar_prefetch=1,
            grid=(b, s // tq),
            in_specs=[qblk, kblk, vtblk, lamspec, lamspec, lamspec, lamspec, pl.BlockSpec((1, HEAD), const)],
            out_specs=qblk,
            scratch_shapes=[pltpu.VMEM((N_HEADS, 2, 1, tq), F32), pltpu.VMEM((N_HEADS, 2, SUBLANES, tq), F32),
                            pltpu.VMEM((N_HEADS, 2, HEAD, tq), F32)],
        ),
        out_shape=jax.ShapeDtypeStruct((b, s, w), BF16),
        compiler_params=_params(("arbitrary", "arbitrary")),
        name="diffattn",
    )(fast, q, k, vt, lq1, lk1, lq2, lk2, sub)


def _outproj_kernel(oh_ref, od_ref, h_ref, wo_ref, fg_ref, wr_ref, h1_ref, xn_ref, route_ref, *, tm):
    n_sub = 2 if tm % (2 * LANES) == 0 else 1
    sub = tm // n_sub
    for si in range(n_sub):
        rs = slice(si * sub, (si + 1) * sub)
        _outproj_rows(oh_ref[rs, :], od_ref[rs, :], h_ref[rs, :], wo_ref, fg_ref, wr_ref,
                      h1_ref.at[rs], xn_ref.at[pl.ds(si * sub * ROW_TILES, sub * ROW_TILES)], route_ref.at[rs], sub)


def _outproj_rows(oh, od, h, wo_ref, fg_ref, wr_ref, h1_ref, xn_ref, route_ref, tm):
    hw = HGRN_WIDTH
    h1 = h + _dot(oh, wo_ref[0:hw, :]) + _dot(od, wo_ref[hw:2 * hw, :])
    h1_ref[...] = h1
    ms = jnp.mean(h1 * h1, axis=-1, keepdims=True)
    xn = h1 * lax.rsqrt(ms + EPS) * fg_ref[...]
    _store_token_tiles(xn_ref, xn, tm)
    x_hi, x_lo = _split_bf16(xn)
    a = _dot(x_hi, wr_ref[...])
    logits = a + pltpu.roll(a, LANES - N_EXPERTS, axis=1) + _dot(x_lo, wr_ref[...])
    lt = logits.T[0:N_EXPERTS, :]
    ex = jnp.exp(lt - jnp.max(lt, axis=0, keepdims=True))
    aff = ex / jnp.sum(ex, axis=0, keepdims=True)
    rows = [aff[e:e + 1, :] for e in range(N_EXPERTS)]

    def top2(a, first):
        m1, i1 = a[0], jnp.full_like(a[0], first)
        for j in range(1, len(a)):
            better = a[j] > m1
            m1 = jnp.where(better, a[j], m1)
            i1 = jnp.where(better, float(first + j), i1)
        m2, i2 = jnp.full_like(m1, -1.0), jnp.zeros_like(m1)
        for j in range(len(a)):
            cand = jnp.where(i1 == float(first + j), -1.0, a[j])
            better = cand > m2
            m2 = jnp.where(better, cand, m2)
            i2 = jnp.where(better, float(first + j), i2)
        return m1, i1, m2, i2

    best = None
    for g in range(N_GROUPS):
        first = g * EXPERTS_PER_GROUP
        cand = top2(rows[first:first + EXPERTS_PER_GROUP], first)
        score = cand[0] + cand[2]
        if best is None:
            best, best_score = cand, score
        else:
            better = score > best_score
            best = tuple(jnp.where(better, c, b) for c, b in zip(cand, best))
            best_score = jnp.where(better, score, best_score)
    m1, i1, m2, i2 = best
    denom = m1 + m2
    rid = lax.broadcasted_iota(jnp.int32, (SUBLANES, tm), 0)
    r8 = jnp.where(rid == 0, m1 / denom,
                   jnp.where(rid == 1, m2 / denom, jnp.where(rid == 2, i1, jnp.where(rid == 3, i2, 0.0))))
    route_t = jnp.concatenate([r8, jnp.zeros((LANES - SUBLANES, tm), F32)], axis=0)
    route_ref[...] = route_t.T


def _outproj(oh, od, h, wo_bf, fg, wr_pack, tm):
    t = h.shape[0]
    row = lambda i: (i, 0)
    const = lambda i: (0, 0)
    return pl.pallas_call(
        functools.partial(_outproj_kernel, tm=tm),
        grid=(t // tm,),
        in_specs=[
            pl.BlockSpec((tm, HGRN_WIDTH), row),
            pl.BlockSpec((tm, HGRN_WIDTH), row),
            pl.BlockSpec((tm, D_MODEL), row),
            pl.BlockSpec((D_MODEL, D_MODEL), const),
            pl.BlockSpec((1, D_MODEL), const),
            pl.BlockSpec((D_MODEL, LANES), const),
        ],
        out_specs=[pl.BlockSpec((tm, D_MODEL), row), pl.BlockSpec((tm * ROW_TILES, LANES), row),
                   pl.BlockSpec((tm, LANES), row)],
        out_shape=[jax.ShapeDtypeStruct((t, D_MODEL), F32), jax.ShapeDtypeStruct((t * ROW_TILES, LANES), F32),
                   jax.ShapeDtypeStruct((t, LANES), F32)],
        compiler_params=_params(("arbitrary",)),
        name="outproj_router",
    )(oh, od, h, wo_bf, fg, wr_pack)


def _tile_copy(src, src_row, dst, dst_row, sem):
    return pltpu.make_async_copy(src.at[pl.ds(pl.multiple_of(src_row * ROW_TILES, ROW_TILES), ROW_TILES)],
                                 dst.at[pl.ds(pl.multiple_of(dst_row * ROW_TILES, ROW_TILES), ROW_TILES)], sem)


def _gather_rows(idx_ref, n, src_hbm, dst, sem):
    def issue(jo, carry):
        for u in range(GATHER_UNROLL):
            j = jo * GATHER_UNROLL + u
            _tile_copy(src_hbm, idx_ref[0, 0, j], dst, j, sem).start(priority=u % 2)
        return carry
    lax.fori_loop(0, n // GATHER_UNROLL, issue, 0)


def _wait_rows(n, src_hbm, dst, sem):
    pltpu.make_async_copy(src_hbm.at[pl.ds(0, n * ROW_TILES)], dst, sem).wait()


def _expert_kernel(blk_e_ref, tok0_ref, tok1_ref, tok2_ref, xn_hbm, wg_ref, wu_ref, wd_ref, y_ref, xbuf, sem):
    del blk_e_ref
    i = pl.program_id(0)
    slot = i % GATHER_SLOTS

    @pl.when(i == 0)
    def _():
        _gather_rows(tok0_ref, ROW_BLOCK, xn_hbm, xbuf.at[0], sem.at[0])

    @pl.when((i == 0) & (pl.num_programs(0) > 1))
    def _():
        _gather_rows(tok1_ref, ROW_BLOCK, xn_hbm, xbuf.at[1], sem.at[1])

    @pl.when(i + 2 < pl.num_programs(0))
    def _():
        ahead = (i + 2) % GATHER_SLOTS
        _gather_rows(tok2_ref, ROW_BLOCK, xn_hbm, xbuf.at[ahead], sem.at[ahead])

    _wait_rows(ROW_BLOCK, xn_hbm, xbuf.at[slot], sem.at[slot])
    x = jnp.concatenate(_load_token_tiles(xbuf.at[slot], ROW_BLOCK), axis=1).astype(BF16)
    g = _dot(x, wg_ref[0])
    u = _dot(x, wu_ref[0])
    mid = (g * jax.nn.sigmoid(g)) * u
    _store_token_tiles(y_ref, _dot(mid.astype(BF16), wd_ref[0]), ROW_BLOCK)


def _experts(blk_e, row_tok, xn_tiles, wg, wu, wd):
    n_blk = row_tok.shape[0]
    wsel = lambda i, be: (be[i], 0, 0)

    def tok_spec(ahead):
        return pl.BlockSpec((1, 1, ROW_BLOCK), lambda i, be: (jnp.minimum(i + ahead, n_blk - 1), 0, 0),
                            memory_space=pltpu.SMEM)

    return pl.pallas_call(
        _expert_kernel,
        grid_spec=pltpu.PrefetchScalarGridSpec(
            num_scalar_prefetch=1,
            grid=(n_blk,),
            in_specs=[tok_spec(0), tok_spec(1), tok_spec(2),
                      pl.BlockSpec(memory_space=pl.ANY),
                      pl.BlockSpec((1, D_MODEL, D_FF), wsel),
                      pl.BlockSpec((1, D_MODEL, D_FF), wsel),
                      pl.BlockSpec((1, D_FF, D_MODEL), wsel)],
            out_specs=pl.BlockSpec((ROW_BLOCK * ROW_TILES, LANES), lambda i, be: (i, 0)),
            scratch_shapes=[pltpu.VMEM((GATHER_SLOTS, ROW_BLOCK * ROW_TILES, LANES), F32),
                            pltpu.SemaphoreType.DMA((GATHER_SLOTS,))],
        ),
        out_shape=jax.ShapeDtypeStruct((n_blk * ROW_BLOCK * ROW_TILES, LANES), F32),
        compiler_params=_params(("arbitrary",)),
        name="moe_experts",
    )(blk_e, row_tok, row_tok, row_tok, xn_tiles, wg, wu, wd)


def _combine_kernel(dest_ref, dest_next_ref, h1_ref, route_ref, p_ref, pp_ref, pn_ref, pg_ref, y_hbm, o_ref,
                    ybuf, sem, *, tm):
    i = pl.program_id(0)
    slot = i % 2
    n = TOP_K * tm

    @pl.when(i == 0)
    def _():
        _gather_rows(dest_ref, n, y_hbm, ybuf.at[0], sem.at[0])

    @pl.when(i + 1 < pl.num_programs(0))
    def _():
        _gather_rows(dest_next_ref, n, y_hbm, ybuf.at[1 - slot], sem.at[1 - slot])

    e = _dot(p_ref[...].astype(BF16), pp_ref[...])
    ms = jnp.mean(e * e, axis=-1, keepdims=True)
    ple = e * lax.rsqrt(ms + EPS) * pn_ref[...]

    _wait_rows(n, y_hbm, ybuf.at[slot], sem.at[slot])
    route = route_ref[...]
    w0 = route[:, 0:1]
    w1 = route[:, 1:2]
    y0 = _load_token_tiles(ybuf.at[slot], tm)
    y1 = _load_token_tiles(ybuf.at[slot], tm, base=tm * ROW_TILES)
    moe = jnp.concatenate([w0 * a + w1 * b for a, b in zip(y0, y1)], axis=1)
    h2 = h1_ref[...] + moe
    gate = jax.nn.sigmoid(_dot(h2.astype(BF16), pg_ref[...]))
    o_ref[...] = h2 + ple * gate


def _combine(dest3, h1, route, p, pp_bf, pn, pg_bf, y_tiles, tm):
    t = h1.shape[0]
    n_tiles = t // tm
    row = lambda i: (i, 0)
    const = lambda i: (0, 0)
    n = TOP_K * tm
    return pl.pallas_call(
        functools.partial(_combine_kernel, tm=tm),
        grid=(n_tiles,),
        in_specs=[pl.BlockSpec((1, 1, n), lambda i: (i, 0, 0), memory_space=pltpu.SMEM),
                  pl.BlockSpec((1, 1, n), lambda i: (jnp.minimum(i + 1, n_tiles - 1), 0, 0), memory_space=pltpu.SMEM),
                  pl.BlockSpec((tm, D_MODEL), row),
                  pl.BlockSpec((tm, LANES), row),
                  pl.BlockSpec((tm, PLE_DIM), row),
                  pl.BlockSpec((PLE_DIM, D_MODEL), const),
                  pl.BlockSpec((1, D_MODEL), const),
                  pl.BlockSpec((D_MODEL, D_MODEL), const),
                  pl.BlockSpec(memory_space=pl.ANY)],
        out_specs=pl.BlockSpec((tm, D_MODEL), row),
        out_shape=jax.ShapeDtypeStruct((t, D_MODEL), F32),
        scratch_shapes=[pltpu.VMEM((2, n * ROW_TILES, LANES), F32), pltpu.SemaphoreType.DMA((2,))],
        compiler_params=_params(("arbitrary",)),
        name="moe_combine_ple",
    )(dest3, dest3, h1, route, p, pp_bf, pn, pg_bf, y_tiles)


def _routing_tables(route, n_rows):
    top_e = route[:, 2:2 + TOP_K].astype(jnp.int32)
    flat_e = top_e.reshape(-1)
    n_assign = flat_e.shape[0]
    experts = jnp.arange(N_EXPERTS, dtype=jnp.int32)
    onehot = (flat_e[:, None] == experts[None, :]).astype(jnp.int32)
    csum = jnp.cumsum(onehot, axis=0)
    rank = jnp.sum(csum * onehot, axis=1) - 1
    counts = csum[-1]
    starts = jnp.cumsum(counts) - counts
    padded = (counts + ROW_BLOCK - 1) // ROW_BLOCK * ROW_BLOCK
    padded_end = jnp.cumsum(padded)
    padded_start = padded_end - padded
    dest = jnp.sum(onehot * padded_start[None, :], axis=1) + rank
    n_blk = n_rows // ROW_BLOCK
    blk_start = jnp.arange(n_blk, dtype=jnp.int32) * ROW_BLOCK
    blk_e = jnp.sum((blk_start[:, None] >= padded_end[None, :]).astype(jnp.int32), axis=1)
    blk_e = jnp.minimum(blk_e, N_EXPERTS - 1).astype(jnp.int32)
    order = jnp.argsort(flat_e, stable=True).astype(jnp.int32)
    row = jnp.arange(n_rows, dtype=jnp.int32)
    row_e = jnp.repeat(blk_e, ROW_BLOCK)
    idx = row - padded_start[row_e]
    src = order[jnp.clip(starts[row_e] + idx, 0, n_assign - 1)]
    row_tok = jnp.where(idx < counts[row_e], src // TOP_K, 0)
    return dest.astype(jnp.int32), row_tok.astype(jnp.int32), blk_e


def _tile(n, pref):
    return pref if n % pref == 0 else n


def kernel(x, p, mix_norm, w_in, hgrn_lb, hgrn_out_norm, q_norm, k_norm, lam_q1, lam_k1, lam_q2, lam_k2,
           diff_subln, w_out, ffn_norm, w_router, w_gate, w_up, w_down, ple_proj, ple_norm, ple_gate):
    b, s, d = x.shape
    depth = w_in.shape[0]
    t = b * s
    tm = _tile(t, 512)
    ts = _tile(s, 1024)
    tq = _tile(s, 512)
    tmc = _tile(t, 256)
    n_assign = t * TOP_K
    n_rows = n_assign + N_EXPERTS * ROW_BLOCK

    lb_soft = jax.nn.softmax(hgrn_lb.astype(F32), axis=0)
    lower_bounds = jnp.cumsum(lb_soft, axis=0) - lb_soft[0]
    grp = np.kron(np.eye(HGRN_WIDTH // DIFF_HEAD_DIM), np.ones((DIFF_HEAD_DIM, DIFF_HEAD_DIM)))
    grp = jnp.asarray(grp, BF16)
    wr_hi, wr_lo = _split_bf16(w_router.astype(F32))
    wr_pack = jnp.pad(jnp.concatenate([wr_hi, wr_lo], axis=1), ((0, 0), (0, LANES - 2 * N_EXPERTS)))
    n_rep = HGRN_WIDTH // DIFF_HEAD_DIM

    h = x.reshape(t, d)
    for i in range(depth):
        lam_init = 0.8 - 0.6 * math.exp(-0.3 * i)
        gq = (jnp.tile(q_norm[i], n_rep) * DIFF_HEAD_DIM ** -0.5).reshape(1, HGRN_WIDTH)
        gk = jnp.tile(k_norm[i], n_rep).reshape(1, HGRN_WIDTH)
        score_bound = 1.02 * DIFF_HEAD_DIM ** 0.5 * jnp.max(jnp.abs(q_norm[i])) * jnp.max(jnp.abs(k_norm[i]))
        fast = (score_bound <= MAX_UNSHIFTED_SCORE).astype(jnp.int32).reshape(1)
        w_bf = w_in[i].astype(BF16)
        n_main = IN_COLS - HGRN_WIDTH
        zq, zf, zi, zg, dq, dk, dvt = _inproj(h, mix_norm[i].reshape(1, d), w_bf[:, :n_main], w_bf[:, n_main:].T,
                                              gq, gk, grp, tq)
        r3 = lambda a: a.reshape(b, s, HGRN_WIDTH)
        o_hgrn = _hgrn(r3(zq), r3(zf), r3(zi), r3(zg), lower_bounds[i].reshape(1, HGRN_WIDTH),
                       hgrn_out_norm[i].reshape(1, HEAD), ts)
        row64 = lambda a: a.reshape(1, DIFF_HEAD_DIM)
        o_diff = _attn(fast, r3(dq), r3(dk), dvt, row64(lam_q1[i]), row64(lam_k1[i]), row64(lam_q2[i]),
                       row64(lam_k2[i]), diff_subln[i].reshape(1, HEAD), lam_init, tq)
        h1, xn_tiles, route = _outproj(o_hgrn.reshape(t, HGRN_WIDTH), o_diff.reshape(t, HGRN_WIDTH), h,
                                       w_out[i].astype(BF16), ffn_norm[i].reshape(1, d), wr_pack, tm)
        dest, row_tok, blk_e = _routing_tables(route, n_rows)
        y_tiles = _experts(blk_e, row_tok.reshape(n_rows // ROW_BLOCK, 1, ROW_BLOCK), xn_tiles,
                           w_gate[i].astype(BF16), w_up[i].astype(BF16), w_down[i].astype(BF16))
        dest3 = dest.reshape(t // tmc, tmc, TOP_K).transpose(0, 2, 1).reshape(t // tmc, 1, TOP_K * tmc)
        h = _combine(dest3, h1, route, p[i].reshape(t, PLE_DIM), ple_proj[i].astype(BF16),
                     ple_norm[i].reshape(1, d), ple_gate[i].astype(BF16), y_tiles, tmc)
    return h.reshape(b, s, d)
```

```python
import functools
import math

import numpy as np
import jax
import jax.numpy as jnp
from jax import lax
from jax.experimental import pallas as pl
from jax.experimental.pallas import tpu as pltpu

F32 = jnp.float32
BF16 = jnp.bfloat16

EPS = 1e-6
MIN_FORGET = 1e-6
MASK_VALUE = -1e30

D_MODEL = 1024
HGRN_WIDTH = 512
HEAD = 128
N_HEADS = 4
DIFF_HEAD_DIM = 64
PLE_DIM = 256
N_EXPERTS = 16
EXPERTS_PER_GROUP = 4
N_GROUPS = 4
TOP_K = 2
D_FF = 512
IN_COLS = 4 * HGRN_WIDTH + 3 * HGRN_WIDTH

LANES = 128
SUBLANES = 8
ROW_TILES = D_MODEL // LANES
CHUNK = 128
ROW_BLOCK = 256
GATHER_UNROLL = 8
GATHER_SLOTS = 3
VMEM_LIMIT = 48 * 1024 * 1024
MAX_UNSHIFTED_SCORE = 40.0

assert ROW_TILES == SUBLANES


def _dot(a, b):
    return jnp.dot(a, b, preferred_element_type=F32)


def _dot_nt(a, b):
    return lax.dot_general(a, b, (((1,), (1,)), ((), ())), preferred_element_type=F32)


def _split_bf16(x):
    hi = x.astype(BF16)
    lo = (x - hi.astype(F32)).astype(BF16)
    return hi, lo


def _params(sem, vmem=VMEM_LIMIT):
    return pltpu.CompilerParams(dimension_semantics=sem, vmem_limit_bytes=vmem)


def _store_token_tiles(ref, x, n):
    for a in range(ROW_TILES):
        ref[pl.ds(a, n, stride=ROW_TILES), :] = x[:, a * LANES:(a + 1) * LANES]


def _load_token_tiles(ref, n, base=0):
    return [ref[pl.ds(base + a, n, stride=ROW_TILES), :] for a in range(ROW_TILES)]


def _inproj_kernel(h_ref, g_ref, w_ref, wvt_ref, gq_ref, gk_ref, grp_ref,
                   zq_ref, zf_ref, zi_ref, zg_ref, dq_ref, dk_ref, dvt_ref):
    x = h_ref[...]
    ms = jnp.mean(x * x, axis=-1, keepdims=True)
    hn = (x * lax.rsqrt(ms + EPS) * g_ref[...]).astype(BF16)
    w = HGRN_WIDTH
    for j, o_ref in enumerate((zq_ref, zf_ref, zi_ref, zg_ref)):
        o_ref[...] = _dot(hn, w_ref[:, j * w:(j + 1) * w]).astype(BF16)
    for j, o_ref, gain_ref in ((4, dq_ref, gq_ref), (5, dk_ref, gk_ref)):
        z = _dot(hn, w_ref[:, j * w:(j + 1) * w])
        ss = _dot((z * z).astype(BF16), grp_ref[...])
        o_ref[...] = (z * lax.rsqrt(ss * (1.0 / DIFF_HEAD_DIM) + EPS) * gain_ref[...]).astype(BF16)
    dvt_ref[0] = _dot_nt(wvt_ref[...], hn).astype(BF16)


def _inproj(h, gain, w_main_bf, w_vt_bf, gq, gk, grp, tm):
    t = h.shape[0]
    w = HGRN_WIDTH
    row = lambda i: (i, 0)
    const = lambda i: (0, 0)
    out = jax.ShapeDtypeStruct((t, w), BF16)
    return pl.pallas_call(
        _inproj_kernel,
        grid=(t // tm,),
        in_specs=[
            pl.BlockSpec((tm, D_MODEL), row),
            pl.BlockSpec((1, D_MODEL), const),
            pl.BlockSpec((D_MODEL, IN_COLS - w), const),
            pl.BlockSpec((w, D_MODEL), const),
            pl.BlockSpec((1, w), const),
            pl.BlockSpec((1, w), const),
            pl.BlockSpec((w, w), const),
        ],
        out_specs=[pl.BlockSpec((tm, w), row)] * 6 + [pl.BlockSpec((1, w, tm), lambda i: (i, 0, 0))],
        out_shape=[out] * 6 + [jax.ShapeDtypeStruct((t // tm, w, tm), BF16)],
        compiler_params=_params(("arbitrary",)),
        name="inproj",
    )(h, gain, w_main_bf, w_vt_bf, gq, gk, grp)


_PAIR_HALVES = (8, 16, 32, 64)
_N_EXP = 3 + len(_PAIR_HALVES)


def _hgrn_constants():
    c = CHUNK
    t = np.arange(c)[:, None]
    j = np.arange(c)[None, :]
    mats = [(j <= t).astype(np.float32), (j > t).astype(np.float32)]
    mid = t - t % 8 + 3
    m0 = np.where((t > mid) & (j > mid) & (j <= t), 1.0, 0.0) - np.where((t < mid) & (j > t) & (j <= mid), 1.0, 0.0)
    mats.append(m0.astype(np.float32))
    lvl = np.full((c, c), -1, np.int32)
    tt, ss = np.broadcast_arrays(t, j)
    lvl[(tt // 8 == ss // 8) & (ss <= tt)] = 0
    for n, m in enumerate(_PAIR_HALVES):
        b = t - t % (2 * m) + m - 1
        mats.append(np.where(t > b, (j > b) & (j <= t), (j > t) & (j <= b)).astype(np.float32))
        sel = (tt // (2 * m) == ss // (2 * m)) & (tt % (2 * m) >= m) & (ss % (2 * m) < m)
        lvl[sel] = n + 1
    return np.concatenate(mats, axis=0), lvl


def _hgrn_kernel(zq_ref, zf_ref, zi_ref, zg_ref, lb_ref, og_ref, m_ref, lvl_ref, o_ref, state_ref, *, n_chunks):
    @pl.when(pl.program_id(1) == 0)
    def _():
        state_ref[...] = jnp.zeros_like(state_ref)

    lb = lb_ref[...]
    og = og_ref[...]
    lvl = lvl_ref[...]
    level_masks = [lvl == n for n in range(1 + len(_PAIR_HALVES))]
    c = CHUNK

    def chunk(ci, carry):
        sl = pl.ds(pl.multiple_of(ci * c, c), c)
        xq = zq_ref[0, sl, :].astype(F32)
        fl = zf_ref[0, sl, :].astype(F32)
        xg = zg_ref[0, sl, :].astype(F32)
        q_all = xq * jax.nn.sigmoid(xq)
        forget = lb + (1.0 - lb) * jax.nn.sigmoid(fl)
        logf = jnp.log(jnp.maximum(forget, MIN_FORGET))
        k_all = 1.0 - forget
        gate_all = xg * jax.nn.sigmoid(xg)
        lf_hi, lf_lo = _split_bf16(logf)
        e_all = _dot(m_ref[...], lf_hi) + _dot(m_ref[...], lf_lo)
        e0 = e_all[2 * c:3 * c]
        q_lv = [(q_all * jnp.exp(e0)).astype(BF16)]
        k_lv = [(k_all * jnp.exp(-e0)).astype(BF16)]
        for n in range(len(_PAIR_HALVES)):
            p = jnp.exp(e_all[(3 + n) * c:(4 + n) * c])
            q_lv.append((q_all * p).astype(BF16))
            k_lv.append((k_all * p).astype(BF16))
        p_cum = jnp.exp(e_all[0:c])
        q_in = (q_all * p_cum).astype(BF16)
        k_out = (k_all * jnp.exp(e_all[c:2 * c])).astype(BF16)
        heads = [slice(hd * HEAD, (hd + 1) * HEAD) for hd in range(N_HEADS)]
        sc = [[_dot_nt(q_lv[n][:, hs], k_lv[n][:, hs]) for n in range(len(q_lv))] for hs in heads]
        s_bf = []
        for per_level in sc:
            s = jnp.where(level_masks[0], per_level[0], 0.0)
            for n in range(1, len(per_level)):
                s = jnp.where(level_masks[n], per_level[n], s)
            s_bf.append(s.astype(BF16))
        v_bf = [zi_ref[0, sl, hs] for hs in heads]
        states = [state_ref[hd] for hd in range(N_HEADS)]
        outs = [_dot_nt(q_in[:, hs], states[hd].astype(BF16)) + _dot(s_bf[hd], v_bf[hd])
                for hd, hs in enumerate(heads)]
        for hd, hs in enumerate(heads):
            v_t = v_bf[hd].astype(F32).T.astype(BF16)
            state_ref[hd] = states[hd] * p_cum[c - 1:c, hs] + _dot(v_t, k_out[:, hs])
        for hd, hs in enumerate(heads):
            o = outs[hd]
            ms = jnp.mean(o * o, axis=-1, keepdims=True)
            y = o * lax.rsqrt(ms + EPS) * og * gate_all[:, hs]
            o_ref[0, sl, hs] = y.astype(BF16)
        return carry

    lax.fori_loop(0, n_chunks, chunk, 0, unroll=2)


def _hgrn(zq, zf, zi, zg, lb, og, ts):
    b, s, _ = zq.shape
    mats, lvl = _hgrn_constants()
    blk = pl.BlockSpec((1, ts, HGRN_WIDTH), lambda bi, si: (bi, si, 0))
    const = lambda bi, si: (0, 0)
    return pl.pallas_call(
        functools.partial(_hgrn_kernel, n_chunks=ts // CHUNK),
        grid=(b, s // ts),
        in_specs=[blk, blk, blk, blk,
                  pl.BlockSpec((1, HGRN_WIDTH), const),
                  pl.BlockSpec((1, HEAD), const),
                  pl.BlockSpec((_N_EXP * CHUNK, CHUNK), const),
                  pl.BlockSpec((CHUNK, CHUNK), const)],
        out_specs=blk,
        out_shape=jax.ShapeDtypeStruct((b, s, HGRN_WIDTH), BF16),
        scratch_shapes=[pltpu.VMEM((N_HEADS, HEAD, HEAD), F32)],
        compiler_params=_params(("arbitrary", "arbitrary")),
        name="hgrn2",
    )(zq, zf, zi, zg, lb, og, jnp.asarray(mats, BF16), jnp.asarray(lvl))


def _attn_kernel(fast_ref, q_ref, k_ref, vt_ref, lq1_ref, lk1_ref, lq2_ref, lk2_ref, sub_ref, o_ref,
                 m_sc, l_sc, acc_sc, *, tq, lam_init):
    qi = pl.program_id(1)
    lane = lax.broadcasted_iota(jnp.int32, (tq, HEAD), 1)
    q_maps = []
    for hd in range(N_HEADS):
        q = q_ref[0, :, hd * HEAD:(hd + 1) * HEAD]
        zero = jnp.zeros_like(q)
        q_maps.append((jnp.where(lane < DIFF_HEAD_DIM, q, zero), jnp.where(lane >= DIFF_HEAD_DIM, q, zero)))
    l_sc[...] = jnp.zeros_like(l_sc)
    acc_sc[...] = jnp.zeros_like(acc_sc)

    def sublane_tile_sum(p):
        return jnp.sum(p.reshape(tq // SUBLANES, SUBLANES, tq), axis=0)

    def scores(q_map, kb, masked):
        s = _dot_nt(kb, q_map)
        if masked:
            key = lax.broadcasted_iota(jnp.int32, s.shape, 0)
            qry = lax.broadcasted_iota(jnp.int32, s.shape, 1)
            s = jnp.where(key <= qry, s, MASK_VALUE)
        return s

    def run(update):
        def step(ki, masked):
            k_all = k_ref[0, pl.ds(pl.multiple_of(ki * tq, tq), tq), :]
            vt_all = vt_ref[ki]
            for hd in range(N_HEADS):
                kb = k_all[:, hd * HEAD:(hd + 1) * HEAD]
                vt = vt_all[hd * HEAD:(hd + 1) * HEAD, :]
                for c in range(2):
                    update(hd, c, scores(q_maps[hd][c], kb, masked), vt)

        def body(ki, carry):
            step(ki, False)
            return carry
        lax.fori_loop(0, qi, body, 0)
        step(qi, True)

    @pl.when(fast_ref[0] == 1)
    def _():
        def step(ki, masked):
            k_all = k_ref[0, pl.ds(pl.multiple_of(ki * tq, tq), tq), :]
            vt_all = vt_ref[ki]
            chains = [(hd, c) for hd in range(N_HEADS) for c in range(2)]
            ps = [jnp.exp(scores(q_maps[hd][c], k_all[:, hd * HEAD:(hd + 1) * HEAD], masked)) for hd, c in chains]
            for (hd, c), p in zip(chains, ps):
                l_sc[hd, c] += sublane_tile_sum(p)
                acc_sc[hd, c] += _dot(vt_all[hd * HEAD:(hd + 1) * HEAD, :], p.astype(BF16))

        def body(ki, carry):
            step(ki, False)
            return carry
        lax.fori_loop(0, qi, body, 0)
        step(qi, True)

    @pl.when(fast_ref[0] != 1)
    def _():
        m_sc[...] = jnp.full_like(m_sc, MASK_VALUE)

        def update(hd, c, s, vt):
            m_old = m_sc[hd, c]
            m_new = jnp.maximum(m_old, jnp.max(s, axis=0, keepdims=True))
            alpha = jnp.exp(m_old - m_new)
            p = jnp.exp(s - m_new)
            l_sc[hd, c] = alpha * l_sc[hd, c] + sublane_tile_sum(p)
            acc_sc[hd, c] = alpha * acc_sc[hd, c] + _dot(vt, p.astype(BF16))
            m_sc[hd, c] = m_new
        run(update)

    lam = (jnp.exp(jnp.sum(lq1_ref[...] * lk1_ref[...], keepdims=True))
           - jnp.exp(jnp.sum(lq2_ref[...] * lk2_ref[...], keepdims=True)) + lam_init)
    for hd in range(N_HEADS):
        l0 = jnp.sum(l_sc[hd, 0], axis=0, keepdims=True)
        l1 = jnp.sum(l_sc[hd, 1], axis=0, keepdims=True)
        o_t = acc_sc[hd, 0] / l0 - lam * (acc_sc[hd, 1] / l1)
        ms = jnp.mean(o_t * o_t, axis=0, keepdims=True)
        o = (o_t * lax.rsqrt(ms + EPS)).T
        o_ref[0, :, hd * HEAD:(hd + 1) * HEAD] = (o * sub_ref[...] * (1.0 - lam_init)).astype(BF16)


def _attn(fast, q, k, vt, lq1, lk1, lq2, lk2, sub, lam_init, tq):
    b, s, w = q.shape
    n_kv = s // tq
    qblk = pl.BlockSpec((1, tq, w), lambda bi, qi, f: (bi, qi, 0))
    kblk = pl.BlockSpec((1, s, w), lambda bi, qi, f: (bi, 0, 0))
    vtblk = pl.BlockSpec((n_kv, w, tq), lambda bi, qi, f: (bi, 0, 0))
    const = lambda bi, qi, f: (0, 0)
    lamspec = pl.BlockSpec((1, DIFF_HEAD_DIM), const)
    return pl.pallas_call(
        functools.partial(_attn_kernel, tq=tq, lam_init=lam_init),
        grid_spec=pltpu.PrefetchScalarGridSpec(
            num_scalar_prefetch=1,
            grid=(b, s // tq),
            in_specs=[qblk, kblk, vtblk, lamspec, lamspec, lamspec, lamspec, pl.BlockSpec((1, HEAD), const)],
            out_specs=qblk,
            scratch_shapes=[pltpu.VMEM((N_HEADS, 2, 1, tq), F32), pltpu.VMEM((N_HEADS, 2, SUBLANES, tq), F32),
                            pltpu.VMEM((N_HEADS, 2, HEAD, tq), F32)],
        ),
        out_shape=jax.ShapeDtypeStruct((b, s, w), BF16),
        compiler_params=_params(("arbitrary", "arbitrary")),
        name="diffattn",
    )(fast, q, k, vt, lq1, lk1, lq2, lk2, sub)


def _outproj_kernel(oh_ref, od_ref, h_ref, wo_ref, fg_ref, wr_ref, h1_ref, xn_ref, route_ref, *, tm):
    n_sub = 2 if tm % (2 * LANES) == 0 else 1
    sub = tm // n_sub
    for si in range(n_sub):
        rs = slice(si * sub, (si + 1) * sub)
        _outproj_rows(oh_ref[rs, :], od_ref[rs, :], h_ref[rs, :], wo_ref, fg_ref, wr_ref,
                      h1_ref.at[rs], xn_ref.at[pl.ds(si * sub * ROW_TILES, sub * ROW_TILES)], route_ref.at[rs], sub)


def _outproj_rows(oh, od, h, wo_ref, fg_ref, wr_ref, h1_ref, xn_ref, route_ref, tm):
    hw = HGRN_WIDTH
    h1 = h + _dot(oh, wo_ref[0:hw, :]) + _dot(od, wo_ref[hw:2 * hw, :])
    h1_ref[...] = h1
    ms = jnp.mean(h1 * h1, axis=-1, keepdims=True)
    xn = h1 * lax.rsqrt(ms + EPS) * fg_ref[...]
    _store_token_tiles(xn_ref, xn, tm)
    x_hi, x_lo = _split_bf16(xn)
    a = _dot(x_hi, wr_ref[...])
    logits = a + pltpu.roll(a, LANES - N_EXPERTS, axis=1) + _dot(x_lo, wr_ref[...])
    lt = logits.T[0:N_EXPERTS, :]
    ex = jnp.exp(lt - jnp.max(lt, axis=0, keepdims=True))
    aff = ex / jnp.sum(ex, axis=0, keepdims=True)
    rows = [aff[e:e + 1, :] for e in range(N_EXPERTS)]

    def top2(a, first):
        m1, i1 = a[0], jnp.full_like(a[0], first)
        for j in range(1, len(a)):
            better = a[j] > m1
            m1 = jnp.where(better, a[j], m1)
            i1 = jnp.where(better, float(first + j), i1)
        m2, i2 = jnp.full_like(m1, -1.0), jnp.zeros_like(m1)
        for j in range(len(a)):
            cand = jnp.where(i1 == float(first + j), -1.0, a[j])
            better = cand > m2
            m2 = jnp.where(better, cand, m2)
            i2 = jnp.where(better, float(first + j), i2)
        return m1, i1, m2, i2

    best = None
    for g in range(N_GROUPS):
        first = g * EXPERTS_PER_GROUP
        m1, i1, m2, i2 = top2(rows[first:first + EXPERTS_PER_GROUP], first)
        cand = (m1, i1 - first, m2, i2 - first, jnp.full_like(m1, float(g)))
        score = m1 + m2
        if best is None:
            best, best_score = cand, score
        else:
            better = score > best_score
            best = tuple(jnp.where(better, c, b) for c, b in zip(cand, best))
            best_score = jnp.where(better, score, best_score)
    m1, j1, m2, j2, gsel = best
    denom = m1 + m2
    w1, w2 = m1 / denom, m2 / denom
    rid = lax.broadcasted_iota(jnp.int32, (SUBLANES, tm), 0).astype(F32)
    r8 = jnp.where(rid == j1, w1, jnp.where(rid == j2, w2, 0.0))
    r8 = jnp.where(rid == float(EXPERTS_PER_GROUP), gsel, r8)
    route_t = jnp.concatenate([r8, jnp.zeros((LANES - SUBLANES, tm), F32)], axis=0)
    route_ref[...] = route_t.T


def _outproj(oh, od, h, wo_bf, fg, wr_pack, tm):
    t = h.shape[0]
    row = lambda i: (i, 0)
    const = lambda i: (0, 0)
    return pl.pallas_call(
        functools.partial(_outproj_kernel, tm=tm),
        grid=(t // tm,),
        in_specs=[
            pl.BlockSpec((tm, HGRN_WIDTH), row),
            pl.BlockSpec((tm, HGRN_WIDTH), row),
            pl.BlockSpec((tm, D_MODEL), row),
            pl.BlockSpec((D_MODEL, D_MODEL), const),
            pl.BlockSpec((1, D_MODEL), const),
            pl.BlockSpec((D_MODEL, LANES), const),
        ],
        out_specs=[pl.BlockSpec((tm, D_MODEL), row), pl.BlockSpec((tm * ROW_TILES, LANES), row),
                   pl.BlockSpec((tm, LANES), row)],
        out_shape=[jax.ShapeDtypeStruct((t, D_MODEL), F32), jax.ShapeDtypeStruct((t * ROW_TILES, LANES), F32),
                   jax.ShapeDtypeStruct((t, LANES), F32)],
        compiler_params=_params(("arbitrary",)),
        name="outproj_router",
    )(oh, od, h, wo_bf, fg, wr_pack)


def _tile_copy(src, src_row, dst, dst_row, sem):
    return pltpu.make_async_copy(src.at[pl.ds(pl.multiple_of(src_row * ROW_TILES, ROW_TILES), ROW_TILES)],
                                 dst.at[pl.ds(pl.multiple_of(dst_row * ROW_TILES, ROW_TILES), ROW_TILES)], sem)


def _gather_rows(idx_ref, n, src_hbm, dst, sem):
    def issue(jo, carry):
        for u in range(GATHER_UNROLL):
            j = jo * GATHER_UNROLL + u
            _tile_copy(src_hbm, idx_ref[0, 0, j], dst, j, sem).start(priority=u % 2)
        return carry
    lax.fori_loop(0, n // GATHER_UNROLL, issue, 0)


def _wait_rows(n, src_hbm, dst, sem):
    pltpu.make_async_copy(src_hbm.at[pl.ds(0, n * ROW_TILES)], dst, sem).wait()


def _expert_kernel(blk_g_ref, tok0_ref, tok1_ref, tok2_ref, xn_hbm, wrow_ref, wg_ref, wu_ref, wd_ref, y_ref,
                   xbuf, sem):
    del blk_g_ref
    i = pl.program_id(0)
    slot = i % GATHER_SLOTS

    @pl.when(i == 0)
    def _():
        _gather_rows(tok0_ref, ROW_BLOCK, xn_hbm, xbuf.at[0], sem.at[0])

    @pl.when((i == 0) & (pl.num_programs(0) > 1))
    def _():
        _gather_rows(tok1_ref, ROW_BLOCK, xn_hbm, xbuf.at[1], sem.at[1])

    @pl.when(i + 2 < pl.num_programs(0))
    def _():
        ahead = (i + 2) % GATHER_SLOTS
        _gather_rows(tok2_ref, ROW_BLOCK, xn_hbm, xbuf.at[ahead], sem.at[ahead])

    _wait_rows(ROW_BLOCK, xn_hbm, xbuf.at[slot], sem.at[slot])
    x = jnp.concatenate(_load_token_tiles(xbuf.at[slot], ROW_BLOCK), axis=1).astype(BF16)
    wrow = wrow_ref[...]
    y = None
    for j in range(EXPERTS_PER_GROUP):
        g = _dot(x, wg_ref[j])
        u = _dot(x, wu_ref[j])
        wj = wrow[:, j:j + 1]
        mid = jnp.where(wj != 0.0, (g * jax.nn.sigmoid(g)) * u * wj, 0.0)
        part = _dot(mid.astype(BF16), wd_ref[j])
        y = part if y is None else y + part
    _store_token_tiles(y_ref, y, ROW_BLOCK)


def _experts(blk_g, row_tok, row_w, xn_tiles, wg, wu, wd):
    n_blk = row_tok.shape[0]
    wsel = lambda i, bg: (bg[i], 0, 0)

    def tok_spec(ahead):
        return pl.BlockSpec((1, 1, ROW_BLOCK), lambda i, be: (jnp.minimum(i + ahead, n_blk - 1), 0, 0),
                            memory_space=pltpu.SMEM)

    return pl.pallas_call(
        _expert_kernel,
        grid_spec=pltpu.PrefetchScalarGridSpec(
            num_scalar_prefetch=1,
            grid=(n_blk,),
            in_specs=[tok_spec(0), tok_spec(1), tok_spec(2),
                      pl.BlockSpec(memory_space=pl.ANY),
                      pl.BlockSpec((ROW_BLOCK, LANES), lambda i, bg: (i, 0)),
                      pl.BlockSpec((EXPERTS_PER_GROUP, D_MODEL, D_FF), wsel),
                      pl.BlockSpec((EXPERTS_PER_GROUP, D_MODEL, D_FF), wsel),
                      pl.BlockSpec((EXPERTS_PER_GROUP, D_FF, D_MODEL), wsel)],
            out_specs=pl.BlockSpec((ROW_BLOCK * ROW_TILES, LANES), lambda i, bg: (i, 0)),
            scratch_shapes=[pltpu.VMEM((GATHER_SLOTS, ROW_BLOCK * ROW_TILES, LANES), F32),
                            pltpu.SemaphoreType.DMA((GATHER_SLOTS,))],
        ),
        out_shape=jax.ShapeDtypeStruct((n_blk * ROW_BLOCK * ROW_TILES, LANES), F32),
        compiler_params=_params(("arbitrary",)),
        name="moe_experts",
    )(blk_g, row_tok, row_tok, row_tok, xn_tiles, row_w, wg, wu, wd)


def _combine_kernel(dest_ref, dest_next_ref, h1_ref, p_ref, pp_ref, pn_ref, pg_ref, y_hbm, o_ref,
                    ybuf, sem, *, tm):
    i = pl.program_id(0)
    slot = i % 2

    @pl.when(i == 0)
    def _():
        _gather_rows(dest_ref, tm, y_hbm, ybuf.at[0], sem.at[0])

    @pl.when(i + 1 < pl.num_programs(0))
    def _():
        _gather_rows(dest_next_ref, tm, y_hbm, ybuf.at[1 - slot], sem.at[1 - slot])

    e = _dot(p_ref[...].astype(BF16), pp_ref[...])
    ms = jnp.mean(e * e, axis=-1, keepdims=True)
    ple = e * lax.rsqrt(ms + EPS) * pn_ref[...]

    _wait_rows(tm, y_hbm, ybuf.at[slot], sem.at[slot])
    h2 = h1_ref[...] + jnp.concatenate(_load_token_tiles(ybuf.at[slot], tm), axis=1)
    gate = jax.nn.sigmoid(_dot(h2.astype(BF16), pg_ref[...]))
    o_ref[...] = h2 + ple * gate


def _combine(dest3, h1, p, pp_bf, pn, pg_bf, y_tiles, tm):
    t = h1.shape[0]
    n_tiles = t // tm
    row = lambda i: (i, 0)
    const = lambda i: (0, 0)
    n = tm
    return pl.pallas_call(
        functools.partial(_combine_kernel, tm=tm),
        grid=(n_tiles,),
        in_specs=[pl.BlockSpec((1, 1, n), lambda i: (i, 0, 0), memory_space=pltpu.SMEM),
                  pl.BlockSpec((1, 1, n), lambda i: (jnp.minimum(i + 1, n_tiles - 1), 0, 0), memory_space=pltpu.SMEM),
                  pl.BlockSpec((tm, D_MODEL), row),
                  pl.BlockSpec((tm, PLE_DIM), row),
                  pl.BlockSpec((PLE_DIM, D_MODEL), const),
                  pl.BlockSpec((1, D_MODEL), const),
                  pl.BlockSpec((D_MODEL, D_MODEL), const),
                  pl.BlockSpec(memory_space=pl.ANY)],
        out_specs=pl.BlockSpec((tm, D_MODEL), row),
        out_shape=jax.ShapeDtypeStruct((t, D_MODEL), F32),
        scratch_shapes=[pltpu.VMEM((2, n * ROW_TILES, LANES), F32), pltpu.SemaphoreType.DMA((2,))],
        compiler_params=_params(("arbitrary",)),
        name="moe_combine_ple",
    )(dest3, dest3, h1, p, pp_bf, pn, pg_bf, y_tiles)


def _routing_tables(route, n_rows):
    tok_g = route[:, EXPERTS_PER_GROUP].astype(jnp.int32)
    n_tok = tok_g.shape[0]
    groups = jnp.arange(N_GROUPS, dtype=jnp.int32)
    onehot = (tok_g[:, None] == groups[None, :]).astype(jnp.int32)
    csum = jnp.cumsum(onehot, axis=0)
    rank = jnp.sum(csum * onehot, axis=1) - 1
    counts = csum[-1]
    starts = jnp.cumsum(counts) - counts
    padded = (counts + ROW_BLOCK - 1) // ROW_BLOCK * ROW_BLOCK
    padded_end = jnp.cumsum(padded)
    padded_start = padded_end - padded
    dest = jnp.sum(onehot * padded_start[None, :], axis=1) + rank
    n_blk = n_rows // ROW_BLOCK
    blk_start = jnp.arange(n_blk, dtype=jnp.int32) * ROW_BLOCK
    blk_g = jnp.sum((blk_start[:, None] >= padded_end[None, :]).astype(jnp.int32), axis=1)
    blk_g = jnp.minimum(blk_g, N_GROUPS - 1).astype(jnp.int32)
    order = jnp.argsort(tok_g, stable=True).astype(jnp.int32)
    row = jnp.arange(n_rows, dtype=jnp.int32)
    row_g = jnp.repeat(blk_g, ROW_BLOCK)
    idx = row - padded_start[row_g]
    src = order[jnp.clip(starts[row_g] + idx, 0, n_tok - 1)]
    row_tok = jnp.where(idx < counts[row_g], src, 0)
    return dest.astype(jnp.int32), row_tok.astype(jnp.int32), blk_g


def _tile(n, pref):
    return pref if n % pref == 0 else n


def kernel(x, p, mix_norm, w_in, hgrn_lb, hgrn_out_norm, q_norm, k_norm, lam_q1, lam_k1, lam_q2, lam_k2,
           diff_subln, w_out, ffn_norm, w_router, w_gate, w_up, w_down, ple_proj, ple_norm, ple_gate):
    b, s, d = x.shape
    depth = w_in.shape[0]
    t = b * s
    tm = _tile(t, 512)
    ts = _tile(s, 1024)
    tq = _tile(s, 512)
    tmc = _tile(t, 256)
    n_rows = t + N_GROUPS * ROW_BLOCK

    lb_soft = jax.nn.softmax(hgrn_lb.astype(F32), axis=0)
    lower_bounds = jnp.cumsum(lb_soft, axis=0) - lb_soft[0]
    grp = np.kron(np.eye(HGRN_WIDTH // DIFF_HEAD_DIM), np.ones((DIFF_HEAD_DIM, DIFF_HEAD_DIM)))
    grp = jnp.asarray(grp, BF16)
    wr_hi, wr_lo = _split_bf16(w_router.astype(F32))
    wr_pack = jnp.pad(jnp.concatenate([wr_hi, wr_lo], axis=1), ((0, 0), (0, LANES - 2 * N_EXPERTS)))
    n_rep = HGRN_WIDTH // DIFF_HEAD_DIM

    h = x.reshape(t, d)
    for i in range(depth):
        lam_init = 0.8 - 0.6 * math.exp(-0.3 * i)
        gq = (jnp.tile(q_norm[i], n_rep) * DIFF_HEAD_DIM ** -0.5).reshape(1, HGRN_WIDTH)
        gk = jnp.tile(k_norm[i], n_rep).reshape(1, HGRN_WIDTH)
        score_bound = 1.02 * DIFF_HEAD_DIM ** 0.5 * jnp.max(jnp.abs(q_norm[i])) * jnp.max(jnp.abs(k_norm[i]))
        fast = (score_bound <= MAX_UNSHIFTED_SCORE).astype(jnp.int32).reshape(1)
        w_bf = w_in[i].astype(BF16)
        n_main = IN_COLS - HGRN_WIDTH
        zq, zf, zi, zg, dq, dk, dvt = _inproj(h, mix_norm[i].reshape(1, d), w_bf[:, :n_main], w_bf[:, n_main:].T,
                                              gq, gk, grp, tq)
        r3 = lambda a: a.reshape(b, s, HGRN_WIDTH)
        o_hgrn = _hgrn(r3(zq), r3(zf), r3(zi), r3(zg), lower_bounds[i].reshape(1, HGRN_WIDTH),
                       hgrn_out_norm[i].reshape(1, HEAD), ts)
        row64 = lambda a: a.reshape(1, DIFF_HEAD_DIM)
        o_diff = _attn(fast, r3(dq), r3(dk), dvt, row64(lam_q1[i]), row64(lam_k1[i]), row64(lam_q2[i]),
                       row64(lam_k2[i]), diff_subln[i].reshape(1, HEAD), lam_init, tq)
        h1, xn_tiles, route = _outproj(o_hgrn.reshape(t, HGRN_WIDTH), o_diff.reshape(t, HGRN_WIDTH), h,
                                       w_out[i].astype(BF16), ffn_norm[i].reshape(1, d), wr_pack, tm)
        dest, row_tok, blk_g = _routing_tables(route, n_rows)
        y_tiles = _experts(blk_g, row_tok.reshape(n_rows // ROW_BLOCK, 1, ROW_BLOCK), route[row_tok], xn_tiles,
                           w_gate[i].astype(BF16), w_up[i].astype(BF16), w_down[i].astype(BF16))
        h = _combine(dest.reshape(t // tmc, 1, tmc), h1, p[i].reshape(t, PLE_DIM), ple_proj[i].astype(BF16),
                     ple_norm[i].reshape(1, d), ple_gate[i].astype(BF16), y_tiles, tmc)
    return h.reshape(b, s, d)
```

```python
import functools
import math

import numpy as np
import jax
import jax.numpy as jnp
from jax import lax
from jax.experimental import pallas as pl
from jax.experimental.pallas import tpu as pltpu

F32 = jnp.float32
BF16 = jnp.bfloat16

EPS = 1e-6
MIN_FORGET = 1e-6
MASK_VALUE = -1e30

D_MODEL = 1024
HGRN_WIDTH = 512
HEAD = 128
N_HEADS = 4
DIFF_HEAD_DIM = 64
PLE_DIM = 256
N_EXPERTS = 16
EXPERTS_PER_GROUP = 4
N_GROUPS = 4
TOP_K = 2
D_FF = 512
IN_COLS = 4 * HGRN_WIDTH + 3 * HGRN_WIDTH

LANES = 128
SUBLANES = 8
ROW_TILES = D_MODEL // LANES
CHUNK = 128
ROW_BLOCK = 256
GATHER_UNROLL = 8
GATHER_SLOTS = 3
VMEM_LIMIT = 48 * 1024 * 1024
MAX_UNSHIFTED_SCORE = 40.0

assert ROW_TILES == SUBLANES


def _dot(a, b):
    return jnp.dot(a, b, preferred_element_type=F32)


def _dot_nt(a, b):
    return lax.dot_general(a, b, (((1,), (1,)), ((), ())), preferred_element_type=F32)


def _split_bf16(x):
    hi = x.astype(BF16)
    lo = (x - hi.astype(F32)).astype(BF16)
    return hi, lo


def _params(sem, vmem=VMEM_LIMIT):
    return pltpu.CompilerParams(dimension_semantics=sem, vmem_limit_bytes=vmem)


def _store_token_tiles(ref, x, n):
    for a in range(ROW_TILES):
        ref[pl.ds(a, n, stride=ROW_TILES), :] = x[:, a * LANES:(a + 1) * LANES]


def _load_token_tiles(ref, n, base=0):
    return [ref[pl.ds(base + a, n, stride=ROW_TILES), :] for a in range(ROW_TILES)]


def _inproj_kernel(h_ref, g_ref, w_ref, wvt_ref, gq_ref, gk_ref, grp_ref,
                   zq_ref, zf_ref, zi_ref, zg_ref, dq_ref, dk_ref, dvt_ref):
    x = h_ref[...]
    ms = jnp.mean(x * x, axis=-1, keepdims=True)
    hn = (x * lax.rsqrt(ms + EPS) * g_ref[...]).astype(BF16)
    w = HGRN_WIDTH
    for j, o_ref in enumerate((zq_ref, zf_ref, zi_ref, zg_ref)):
        o_ref[...] = _dot(hn, w_ref[:, j * w:(j + 1) * w]).astype(BF16)
    for j, o_ref, gain_ref in ((4, dq_ref, gq_ref), (5, dk_ref, gk_ref)):
        z = _dot(hn, w_ref[:, j * w:(j + 1) * w])
        ss = _dot((z * z).astype(BF16), grp_ref[...])
        o_ref[...] = (z * lax.rsqrt(ss * (1.0 / DIFF_HEAD_DIM) + EPS) * gain_ref[...]).astype(BF16)
    dvt_ref[0] = _dot_nt(wvt_ref[...], hn).astype(BF16)


def _inproj(h, gain, w_main_bf, w_vt_bf, gq, gk, grp, tm):
    t = h.shape[0]
    w = HGRN_WIDTH
    row = lambda i: (i, 0)
    const = lambda i: (0, 0)
    out = jax.ShapeDtypeStruct((t, w), BF16)
    return pl.pallas_call(
        _inproj_kernel,
        grid=(t // tm,),
        in_specs=[
            pl.BlockSpec((tm, D_MODEL), row),
            pl.BlockSpec((1, D_MODEL), const),
            pl.BlockSpec((D_MODEL, IN_COLS - w), const),
            pl.BlockSpec((w, D_MODEL), const),
            pl.BlockSpec((1, w), const),
            pl.BlockSpec((1, w), const),
            pl.BlockSpec((w, w), const),
        ],
        out_specs=[pl.BlockSpec((tm, w), row)] * 6 + [pl.BlockSpec((1, w, tm), lambda i: (i, 0, 0))],
        out_shape=[out] * 6 + [jax.ShapeDtypeStruct((t // tm, w, tm), BF16)],
        compiler_params=_params(("arbitrary",)),
        name="inproj",
    )(h, gain, w_main_bf, w_vt_bf, gq, gk, grp)


_PAIR_HALVES = (8, 16, 32, 64)
_N_EXP = 3 + len(_PAIR_HALVES)


def _hgrn_constants():
    c = CHUNK
    t = np.arange(c)[:, None]
    j = np.arange(c)[None, :]
    mats = [(j <= t).astype(np.float32), (j > t).astype(np.float32)]
    mid = t - t % 8 + 3
    m0 = np.where((t > mid) & (j > mid) & (j <= t), 1.0, 0.0) - np.where((t < mid) & (j > t) & (j <= mid), 1.0, 0.0)
    mats.append(m0.astype(np.float32))
    lvl = np.full((c, c), -1, np.int32)
    tt, ss = np.broadcast_arrays(t, j)
    lvl[(tt // 8 == ss // 8) & (ss <= tt)] = 0
    for n, m in enumerate(_PAIR_HALVES):
        b = t - t % (2 * m) + m - 1
        mats.append(np.where(t > b, (j > b) & (j <= t), (j > t) & (j <= b)).astype(np.float32))
        sel = (tt // (2 * m) == ss // (2 * m)) & (tt % (2 * m) >= m) & (ss % (2 * m) < m)
        lvl[sel] = n + 1
    return np.concatenate(mats, axis=0), lvl


def _hgrn_kernel(zq_ref, zf_ref, zi_ref, zg_ref, lb_ref, og_ref, m_ref, lvl_ref, o_ref, state_ref, *, n_chunks):
    @pl.when(pl.program_id(1) == 0)
    def _():
        state_ref[...] = jnp.zeros_like(state_ref)

    lb = lb_ref[...]
    og = og_ref[...]
    lvl = lvl_ref[...]
    level_masks = [lvl == n for n in range(1 + len(_PAIR_HALVES))]
    c = CHUNK

    def chunk(ci, carry):
        sl = pl.ds(pl.multiple_of(ci * c, c), c)
        xq = zq_ref[0, sl, :].astype(F32)
        fl = zf_ref[0, sl, :].astype(F32)
        xg = zg_ref[0, sl, :].astype(F32)
        q_all = xq * jax.nn.sigmoid(xq)
        forget = lb + (1.0 - lb) * jax.nn.sigmoid(fl)
        logf = jnp.log(jnp.maximum(forget, MIN_FORGET))
        k_all = 1.0 - forget
        gate_all = xg * jax.nn.sigmoid(xg)
        lf_hi, lf_lo = _split_bf16(logf)
        e_all = _dot(m_ref[...], lf_hi) + _dot(m_ref[...], lf_lo)
        e0 = e_all[2 * c:3 * c]
        q_lv = [(q_all * jnp.exp(e0)).astype(BF16)]
        k_lv = [(k_all * jnp.exp(-e0)).astype(BF16)]
        for n in range(len(_PAIR_HALVES)):
            p = jnp.exp(e_all[(3 + n) * c:(4 + n) * c])
            q_lv.append((q_all * p).astype(BF16))
            k_lv.append((k_all * p).astype(BF16))
        p_cum = jnp.exp(e_all[0:c])
        q_in = (q_all * p_cum).astype(BF16)
        k_out = (k_all * jnp.exp(e_all[c:2 * c])).astype(BF16)
        heads = [slice(hd * HEAD, (hd + 1) * HEAD) for hd in range(N_HEADS)]
        sc = [[_dot_nt(q_lv[n][:, hs], k_lv[n][:, hs]) for n in range(len(q_lv))] for hs in heads]
        s_bf = []
        for per_level in sc:
            s = jnp.where(level_masks[0], per_level[0], 0.0)
            for n in range(1, len(per_level)):
                s = jnp.where(level_masks[n], per_level[n], s)
            s_bf.append(s.astype(BF16))
        v_bf = [zi_ref[0, sl, hs] for hs in heads]
        states = [state_ref[hd] for hd in range(N_HEADS)]
        outs = [_dot_nt(q_in[:, hs], states[hd].astype(BF16)) + _dot(s_bf[hd], v_bf[hd])
                for hd, hs in enumerate(heads)]
        for hd, hs in enumerate(heads):
            v_t = v_bf[hd].astype(F32).T.astype(BF16)
            state_ref[hd] = states[hd] * p_cum[c - 1:c, hs] + _dot(v_t, k_out[:, hs])
        for hd, hs in enumerate(heads):
            o = outs[hd]
            ms = jnp.mean(o * o, axis=-1, keepdims=True)
            y = o * lax.rsqrt(ms + EPS) * og * gate_all[:, hs]
            o_ref[0, sl, hs] = y.astype(BF16)
        return carry

    lax.fori_loop(0, n_chunks, chunk, 0, unroll=2)


def _hgrn(zq, zf, zi, zg, lb, og, ts):
    b, s, _ = zq.shape
    mats, lvl = _hgrn_constants()
    blk = pl.BlockSpec((1, ts, HGRN_WIDTH), lambda bi, si: (bi, si, 0))
    const = lambda bi, si: (0, 0)
    return pl.pallas_call(
        functools.partial(_hgrn_kernel, n_chunks=ts // CHUNK),
        grid=(b, s // ts),
        in_specs=[blk, blk, blk, blk,
                  pl.BlockSpec((1, HGRN_WIDTH), const),
                  pl.BlockSpec((1, HEAD), const),
                  pl.BlockSpec((_N_EXP * CHUNK, CHUNK), const),
                  pl.BlockSpec((CHUNK, CHUNK), const)],
        out_specs=blk,
        out_shape=jax.ShapeDtypeStruct((b, s, HGRN_WIDTH), BF16),
        scratch_shapes=[pltpu.VMEM((N_HEADS, HEAD, HEAD), F32)],
        compiler_params=_params(("arbitrary", "arbitrary")),
        name="hgrn2",
    )(zq, zf, zi, zg, lb, og, jnp.asarray(mats, BF16), jnp.asarray(lvl))


def _attn_kernel(fast_ref, q_ref, k_ref, vt_ref, lq1_ref, lk1_ref, lq2_ref, lk2_ref, sub_ref, o_ref,
                 m_sc, l_sc, acc_sc, *, tq, lam_init):
    qi = pl.program_id(1)
    lane = lax.broadcasted_iota(jnp.int32, (tq, HEAD), 1)
    q_maps = []
    for hd in range(N_HEADS):
        q = q_ref[0, :, hd * HEAD:(hd + 1) * HEAD]
        zero = jnp.zeros_like(q)
        q_maps.append((jnp.where(lane < DIFF_HEAD_DIM, q, zero), jnp.where(lane >= DIFF_HEAD_DIM, q, zero)))
    l_sc[...] = jnp.zeros_like(l_sc)
    acc_sc[...] = jnp.zeros_like(acc_sc)

    def sublane_tile_sum(p):
        return jnp.sum(p.reshape(tq // SUBLANES, SUBLANES, tq), axis=0)

    def scores(q_map, kb, masked):
        s = _dot_nt(kb, q_map)
        if masked:
            key = lax.broadcasted_iota(jnp.int32, s.shape, 0)
            qry = lax.broadcasted_iota(jnp.int32, s.shape, 1)
            s = jnp.where(key <= qry, s, MASK_VALUE)
        return s

    def run(update):
        def step(ki, masked):
            k_all = k_ref[0, pl.ds(pl.multiple_of(ki * tq, tq), tq), :]
            vt_all = vt_ref[ki]
            for hd in range(N_HEADS):
                kb = k_all[:, hd * HEAD:(hd + 1) * HEAD]
                vt = vt_all[hd * HEAD:(hd + 1) * HEAD, :]
                for c in range(2):
                    update(hd, c, scores(q_maps[hd][c], kb, masked), vt)

        def body(ki, carry):
            step(ki, False)
            return carry
        lax.fori_loop(0, qi, body, 0)
        step(qi, True)

    @pl.when(fast_ref[0] == 1)
    def _():
        def step(ki, masked):
            k_all = k_ref[0, pl.ds(pl.multiple_of(ki * tq, tq), tq), :]
            vt_all = vt_ref[ki]
            chains = [(hd, c) for hd in range(N_HEADS) for c in range(2)]
            ps = [jnp.exp(scores(q_maps[hd][c], k_all[:, hd * HEAD:(hd + 1) * HEAD], masked)) for hd, c in chains]
            for (hd, c), p in zip(chains, ps):
                l_sc[hd, c] += sublane_tile_sum(p)
                acc_sc[hd, c] += _dot(vt_all[hd * HEAD:(hd + 1) * HEAD, :], p.astype(BF16))

        def body(ki, carry):
            step(ki, False)
            return carry
        lax.fori_loop(0, qi, body, 0)
        step(qi, True)

    @pl.when(fast_ref[0] != 1)
    def _():
        m_sc[...] = jnp.full_like(m_sc, MASK_VALUE)

        def update(hd, c, s, vt):
            m_old = m_sc[hd, c]
            m_new = jnp.maximum(m_old, jnp.max(s, axis=0, keepdims=True))
            alpha = jnp.exp(m_old - m_new)
            p = jnp.exp(s - m_new)
            l_sc[hd, c] = alpha * l_sc[hd, c] + sublane_tile_sum(p)
            acc_sc[hd, c] = alpha * acc_sc[hd, c] + _dot(vt, p.astype(BF16))
            m_sc[hd, c] = m_new
        run(update)

    lam = (jnp.exp(jnp.sum(lq1_ref[...] * lk1_ref[...], keepdims=True))
           - jnp.exp(jnp.sum(lq2_ref[...] * lk2_ref[...], keepdims=True)) + lam_init)
    for hd in range(N_HEADS):
        l0 = jnp.sum(l_sc[hd, 0], axis=0, keepdims=True)
        l1 = jnp.sum(l_sc[hd, 1], axis=0, keepdims=True)
        o_t = acc_sc[hd, 0] / l0 - lam * (acc_sc[hd, 1] / l1)
        ms = jnp.mean(o_t * o_t, axis=0, keepdims=True)
        o = (o_t * lax.rsqrt(ms + EPS)).T
        o_ref[0, :, hd * HEAD:(hd + 1) * HEAD] = (o * sub_ref[...] * (1.0 - lam_init)).astype(BF16)


def _attn(fast, q, k, vt, lq1, lk1, lq2, lk2, sub, lam_init, tq):
    b, s, w = q.shape
    n_kv = s // tq
    qblk = pl.BlockSpec((1, tq, w), lambda bi, qi, f: (bi, qi, 0))
    kblk = pl.BlockSpec((1, s, w), lambda bi, qi, f: (bi, 0, 0))
    vtblk = pl.BlockSpec((n_kv, w, tq), lambda bi, qi, f: (bi, 0, 0))
    const = lambda bi, qi, f: (0, 0)
    lamspec = pl.BlockSpec((1, DIFF_HEAD_DIM), const)
    return pl.pallas_call(
        functools.partial(_attn_kernel, tq=tq, lam_init=lam_init),
        grid_spec=pltpu.PrefetchScalarGridSpec(
            num_scalar_prefetch=1,
            grid=(b, s // tq),
            in_specs=[qblk, kblk, vtblk, lamspec, lamspec, lamspec, lamspec, pl.BlockSpec((1, HEAD), const)],
            out_specs=qblk,
            scratch_shapes=[pltpu.VMEM((N_HEADS, 2, 1, tq), F32), pltpu.VMEM((N_HEADS, 2, SUBLANES, tq), F32),
                            pltpu.VMEM((N_HEADS, 2, HEAD, tq), F32)],
        ),
        out_shape=jax.ShapeDtypeStruct((b, s, w), BF16),
        compiler_params=_params(("arbitrary", "arbitrary")),
        name="diffattn",
    )(fast, q, k, vt, lq1, lk1, lq2, lk2, sub)


def _outproj_kernel(oh_ref, od_ref, h_ref, wo_ref, fg_ref, wr_ref, h1_ref, xn_ref, route_ref, *, tm):
    n_sub = 2 if tm % (2 * LANES) == 0 else 1
    sub = tm // n_sub
    for si in range(n_sub):
        rs = slice(si * sub, (si + 1) * sub)
        _outproj_rows(oh_ref[rs, :], od_ref[rs, :], h_ref[rs, :], wo_ref, fg_ref, wr_ref,
                      h1_ref.at[rs], xn_ref.at[pl.ds(si * sub * ROW_TILES, sub * ROW_TILES)], route_ref.at[rs], sub)


def _outproj_rows(oh, od, h, wo_ref, fg_ref, wr_ref, h1_ref, xn_ref, route_ref, tm):
    hw = HGRN_WIDTH
    h1 = h + _dot(oh, wo_ref[0:hw, :]) + _dot(od, wo_ref[hw:2 * hw, :])
    h1_ref[...] = h1
    ms = jnp.mean(h1 * h1, axis=-1, keepdims=True)
    xn = h1 * lax.rsqrt(ms + EPS) * fg_ref[...]
    _store_token_tiles(xn_ref, xn, tm)
    x_hi, x_lo = _split_bf16(xn)
    a = _dot(x_hi, wr_ref[...])
    logits = a + pltpu.roll(a, LANES - N_EXPERTS, axis=1) + _dot(x_lo, wr_ref[...])
    lt = logits.T[0:N_EXPERTS, :]
    ex = jnp.exp(lt - jnp.max(lt, axis=0, keepdims=True))
    aff = ex / jnp.sum(ex, axis=0, keepdims=True)
    rows = [aff[e:e + 1, :] for e in range(N_EXPERTS)]

    def top2(a, first):
        m1, i1 = a[0], jnp.full_like(a[0], first)
        for j in range(1, len(a)):
            better = a[j] > m1
            m1 = jnp.where(better, a[j], m1)
            i1 = jnp.where(better, float(first + j), i1)
        m2, i2 = jnp.full_like(m1, -1.0), jnp.zeros_like(m1)
        for j in range(len(a)):
            cand = jnp.where(i1 == float(first + j), -1.0, a[j])
            better = cand > m2
            m2 = jnp.where(better, cand, m2)
            i2 = jnp.where(better, float(first + j), i2)
        return m1, i1, m2, i2

    best = None
    for g in range(N_GROUPS):
        first = g * EXPERTS_PER_GROUP
        m1, i1, m2, i2 = top2(rows[first:first + EXPERTS_PER_GROUP], first)
        cand = (m1, i1 - first, m2, i2 - first, jnp.full_like(m1, float(g)))
        score = m1 + m2
        if best is None:
            best, best_score = cand, score
        else:
            better = score > best_score
            best = tuple(jnp.where(better, c, b) for c, b in zip(cand, best))
            best_score = jnp.where(better, score, best_score)
    m1, j1, m2, j2, gsel = best
    denom = m1 + m2
    w1, w2 = m1 / denom, m2 / denom
    rid = lax.broadcasted_iota(jnp.int32, (SUBLANES, tm), 0).astype(F32)
    r8 = jnp.where(rid == j1, w1, jnp.where(rid == j2, w2, 0.0))
    r8 = jnp.where(rid == float(EXPERTS_PER_GROUP), gsel, r8)
    route_t = jnp.concatenate([r8, jnp.zeros((LANES - SUBLANES, tm), F32)], axis=0)
    route_ref[...] = route_t.T


def _outproj(oh, od, h, wo_bf, fg, wr_pack, tm):
    t = h.shape[0]
    row = lambda i: (i, 0)
    const = lambda i: (0, 0)
    return pl.pallas_call(
        functools.partial(_outproj_kernel, tm=tm),
        grid=(t // tm,),
        in_specs=[
            pl.BlockSpec((tm, HGRN_WIDTH), row),
            pl.BlockSpec((tm, HGRN_WIDTH), row),
            pl.BlockSpec((tm, D_MODEL), row),
            pl.BlockSpec((D_MODEL, D_MODEL), const),
            pl.BlockSpec((1, D_MODEL), const),
            pl.BlockSpec((D_MODEL, LANES), const),
        ],
        out_specs=[pl.BlockSpec((tm, D_MODEL), row), pl.BlockSpec((tm * ROW_TILES, LANES), row),
                   pl.BlockSpec((tm, LANES), row)],
        out_shape=[jax.ShapeDtypeStruct((t, D_MODEL), F32), jax.ShapeDtypeStruct((t * ROW_TILES, LANES), F32),
                   jax.ShapeDtypeStruct((t, LANES), F32)],
        compiler_params=_params(("arbitrary",)),
        name="outproj_router",
    )(oh, od, h, wo_bf, fg, wr_pack)


def _tile_copy(src, src_row, dst, dst_row, sem):
    return pltpu.make_async_copy(src.at[pl.ds(pl.multiple_of(src_row * ROW_TILES, ROW_TILES), ROW_TILES)],
                                 dst.at[pl.ds(pl.multiple_of(dst_row * ROW_TILES, ROW_TILES), ROW_TILES)], sem)


def _gather_rows(idx_ref, n, src_hbm, dst, sem):
    def issue(jo, carry):
        for u in range(GATHER_UNROLL):
            j = jo * GATHER_UNROLL + u
            _tile_copy(src_hbm, idx_ref[0, 0, j], dst, j, sem).start(priority=u % 2)
        return carry
    lax.fori_loop(0, n // GATHER_UNROLL, issue, 0)


def _gather_rows_inline(idx_ref, n, src_hbm, dst, sem):
    for j in range(n):
        _tile_copy(src_hbm, idx_ref[0, 0, j], dst, j, sem).start(priority=j % 2)


def _wait_rows(n, src_hbm, dst, sem):
    pltpu.make_async_copy(src_hbm.at[pl.ds(0, n * ROW_TILES)], dst, sem).wait()


def _expert_kernel(blk_g_ref, tok0_ref, tok1_ref, tok2_ref, xn_hbm, wrow_ref, wg_ref, wu_ref, wd_ref, y_ref,
                   xbuf, sem):
    del blk_g_ref
    i = pl.program_id(0)
    slot = i % GATHER_SLOTS
    ahead = (i + 2) % GATHER_SLOTS

    @pl.when(i == 0)
    def _():
        _gather_rows(tok0_ref, ROW_BLOCK, xn_hbm, xbuf.at[0], sem.at[0])
        _gather_rows(tok1_ref, ROW_BLOCK, xn_hbm, xbuf.at[1], sem.at[1])

    _wait_rows(ROW_BLOCK, xn_hbm, xbuf.at[slot], sem.at[slot])
    x = jnp.concatenate(_load_token_tiles(xbuf.at[slot], ROW_BLOCK), axis=1).astype(BF16)
    _gather_rows_inline(tok2_ref, ROW_BLOCK, xn_hbm, xbuf.at[ahead], sem.at[ahead])
    wrow = wrow_ref[...]
    y = None
    for j in range(EXPERTS_PER_GROUP):
        g = _dot(x, wg_ref[j])
        u = _dot(x, wu_ref[j])
        wj = wrow[:, j:j + 1]
        mid = jnp.where(wj != 0.0, (g * jax.nn.sigmoid(g)) * u * wj, 0.0)
        part = _dot(mid.astype(BF16), wd_ref[j])
        y = part if y is None else y + part
    _store_token_tiles(y_ref, y, ROW_BLOCK)

    @pl.when(i == pl.num_programs(0) - 1)
    def _():
        for extra in (1, 2):
            s_extra = (i + extra) % GATHER_SLOTS
            _wait_rows(ROW_BLOCK, xn_hbm, xbuf.at[s_extra], sem.at[s_extra])


def _experts(blk_g, row_tok, row_w, xn_tiles, wg, wu, wd):
    n_blk = row_tok.shape[0]
    assert n_blk >= GATHER_SLOTS
    wsel = lambda i, bg: (bg[i], 0, 0)

    def tok_spec(ahead):
        return pl.BlockSpec((1, 1, ROW_BLOCK), lambda i, be: (jnp.minimum(i + ahead, n_blk - 1), 0, 0),
                            memory_space=pltpu.SMEM)

    return pl.pallas_call(
        _expert_kernel,
        grid_spec=pltpu.PrefetchScalarGridSpec(
            num_scalar_prefetch=1,
            grid=(n_blk,),
            in_specs=[tok_spec(0), tok_spec(1), tok_spec(2),
                      pl.BlockSpec(memory_space=pl.ANY),
                      pl.BlockSpec((ROW_BLOCK, SUBLANES), lambda i, bg: (i, 0)),
                      pl.BlockSpec((EXPERTS_PER_GROUP, D_MODEL, D_FF), wsel),
                      pl.BlockSpec((EXPERTS_PER_GROUP, D_MODEL, D_FF), wsel),
                      pl.BlockSpec((EXPERTS_PER_GROUP, D_FF, D_MODEL), wsel)],
            out_specs=pl.BlockSpec((ROW_BLOCK * ROW_TILES, LANES), lambda i, bg: (i, 0)),
            scratch_shapes=[pltpu.VMEM((GATHER_SLOTS, ROW_BLOCK * ROW_TILES, LANES), F32),
                            pltpu.SemaphoreType.DMA((GATHER_SLOTS,))],
        ),
        out_shape=jax.ShapeDtypeStruct((n_blk * ROW_BLOCK * ROW_TILES, LANES), F32),
        compiler_params=_params(("arbitrary",)),
        name="moe_experts",
    )(blk_g, row_tok, row_tok, row_tok, xn_tiles, row_w, wg, wu, wd)


def _combine_kernel(dest_ref, dest_next_ref, h1_ref, p_ref, pp_ref, pn_ref, pg_ref, y_hbm, o_ref,
                    ybuf, sem, *, tm):
    i = pl.program_id(0)
    slot = i % 2

    @pl.when(i == 0)
    def _():
        _gather_rows(dest_ref, tm, y_hbm, ybuf.at[0], sem.at[0])

    _gather_rows_inline(dest_next_ref, tm, y_hbm, ybuf.at[1 - slot], sem.at[1 - slot])

    e = _dot(p_ref[...].astype(BF16), pp_ref[...])
    ms = jnp.mean(e * e, axis=-1, keepdims=True)
    ple = e * lax.rsqrt(ms + EPS) * pn_ref[...]

    _wait_rows(tm, y_hbm, ybuf.at[slot], sem.at[slot])
    h2 = h1_ref[...] + jnp.concatenate(_load_token_tiles(ybuf.at[slot], tm), axis=1)
    gate = jax.nn.sigmoid(_dot(h2.astype(BF16), pg_ref[...]))
    o_ref[...] = h2 + ple * gate

    @pl.when(i == pl.num_programs(0) - 1)
    def _():
        _wait_rows(tm, y_hbm, ybuf.at[1 - slot], sem.at[1 - slot])


def _combine(dest3, h1, p, pp_bf, pn, pg_bf, y_tiles, tm):
    t = h1.shape[0]
    n_tiles = t // tm
    row = lambda i: (i, 0)
    const = lambda i: (0, 0)
    n = tm
    return pl.pallas_call(
        functools.partial(_combine_kernel, tm=tm),
        grid=(n_tiles,),
        in_specs=[pl.BlockSpec((1, 1, n), lambda i: (i, 0, 0), memory_space=pltpu.SMEM),
                  pl.BlockSpec((1, 1, n), lambda i: (jnp.minimum(i + 1, n_tiles - 1), 0, 0), memory_space=pltpu.SMEM),
                  pl.BlockSpec((tm, D_MODEL), row),
                  pl.BlockSpec((tm, PLE_DIM), row),
                  pl.BlockSpec((PLE_DIM, D_MODEL), const),
                  pl.BlockSpec((1, D_MODEL), const),
                  pl.BlockSpec((D_MODEL, D_MODEL), const),
                  pl.BlockSpec(memory_space=pl.ANY)],
        out_specs=pl.BlockSpec((tm, D_MODEL), row),
        out_shape=jax.ShapeDtypeStruct((t, D_MODEL), F32),
        scratch_shapes=[pltpu.VMEM((2, n * ROW_TILES, LANES), F32), pltpu.SemaphoreType.DMA((2,))],
        compiler_params=_params(("arbitrary",)),
        name="moe_combine_ple",
    )(dest3, dest3, h1, p, pp_bf, pn, pg_bf, y_tiles)


def _routing_tables(route, n_rows):
    tok_g = route[:, EXPERTS_PER_GROUP].astype(jnp.int32)
    n_tok = tok_g.shape[0]
    groups = jnp.arange(N_GROUPS, dtype=jnp.int32)
    onehot = (tok_g[:, None] == groups[None, :]).astype(jnp.int32)
    csum = jnp.cumsum(onehot, axis=0)
    rank = jnp.sum(csum * onehot, axis=1) - 1
    counts = csum[-1]
    starts = jnp.cumsum(counts) - counts
    padded = (counts + ROW_BLOCK - 1) // ROW_BLOCK * ROW_BLOCK
    padded_end = jnp.cumsum(padded)
    padded_start = padded_end - padded
    dest = jnp.sum(onehot * padded_start[None, :], axis=1) + rank
    n_blk = n_rows // ROW_BLOCK
    blk_start = jnp.arange(n_blk, dtype=jnp.int32) * ROW_BLOCK
    blk_g = jnp.sum((blk_start[:, None] >= padded_end[None, :]).astype(jnp.int32), axis=1)
    blk_g = jnp.minimum(blk_g, N_GROUPS - 1).astype(jnp.int32)
    order = jnp.argsort(tok_g, stable=True).astype(jnp.int32)
    row = jnp.arange(n_rows, dtype=jnp.int32)
    row_g = jnp.repeat(blk_g, ROW_BLOCK)
    idx = row - padded_start[row_g]
    src = order[jnp.clip(starts[row_g] + idx, 0, n_tok - 1)]
    row_tok = jnp.where(idx < counts[row_g], src, 0)
    return dest.astype(jnp.int32), row_tok.astype(jnp.int32), blk_g


def _tile(n, pref):
    return pref if n % pref == 0 else n


def kernel(x, p, mix_norm, w_in, hgrn_lb, hgrn_out_norm, q_norm, k_norm, lam_q1, lam_k1, lam_q2, lam_k2,
           diff_subln, w_out, ffn_norm, w_router, w_gate, w_up, w_down, ple_proj, ple_norm, ple_gate):
    b, s, d = x.shape
    depth = w_in.shape[0]
    t = b * s
    tm = _tile(t, 512)
    ts = _tile(s, 1024)
    tq = _tile(s, 512)
    tmc = _tile(t, 256)
    n_rows = t + N_GROUPS * ROW_BLOCK

    lb_soft = jax.nn.softmax(hgrn_lb.astype(F32), axis=0)
    lower_bounds = jnp.cumsum(lb_soft, axis=0) - lb_soft[0]
    grp = np.kron(np.eye(HGRN_WIDTH // DIFF_HEAD_DIM), np.ones((DIFF_HEAD_DIM, DIFF_HEAD_DIM)))
    grp = jnp.asarray(grp, BF16)
    wr_hi, wr_lo = _split_bf16(w_router.astype(F32))
    wr_pack = jnp.pad(jnp.concatenate([wr_hi, wr_lo], axis=1), ((0, 0), (0, LANES - 2 * N_EXPERTS)))
    n_rep = HGRN_WIDTH // DIFF_HEAD_DIM

    h = x.reshape(t, d)
    for i in range(depth):
        lam_init = 0.8 - 0.6 * math.exp(-0.3 * i)
        gq = (jnp.tile(q_norm[i], n_rep) * DIFF_HEAD_DIM ** -0.5).reshape(1, HGRN_WIDTH)
        gk = jnp.tile(k_norm[i], n_rep).reshape(1, HGRN_WIDTH)
        score_bound = 1.02 * DIFF_HEAD_DIM ** 0.5 * jnp.max(jnp.abs(q_norm[i])) * jnp.max(jnp.abs(k_norm[i]))
        fast = (score_bound <= MAX_UNSHIFTED_SCORE).astype(jnp.int32).reshape(1)
        w_bf = w_in[i].astype(BF16)
        n_main = IN_COLS - HGRN_WIDTH
        zq, zf, zi, zg, dq, dk, dvt = _inproj(h, mix_norm[i].reshape(1, d), w_bf[:, :n_main], w_bf[:, n_main:].T,
                                              gq, gk, grp, tq)
        r3 = lambda a: a.reshape(b, s, HGRN_WIDTH)
        o_hgrn = _hgrn(r3(zq), r3(zf), r3(zi), r3(zg), lower_bounds[i].reshape(1, HGRN_WIDTH),
                       hgrn_out_norm[i].reshape(1, HEAD), ts)
        row64 = lambda a: a.reshape(1, DIFF_HEAD_DIM)
        o_diff = _attn(fast, r3(dq), r3(dk), dvt, row64(lam_q1[i]), row64(lam_k1[i]), row64(lam_q2[i]),
                       row64(lam_k2[i]), diff_subln[i].reshape(1, HEAD), lam_init, tq)
        h1, xn_tiles, route = _outproj(o_hgrn.reshape(t, HGRN_WIDTH), o_diff.reshape(t, HGRN_WIDTH), h,
                                       w_out[i].astype(BF16), ffn_norm[i].reshape(1, d), wr_pack, tm)
        dest, row_tok, blk_g = _routing_tables(route, n_rows)
        y_tiles = _experts(blk_g, row_tok.reshape(n_rows // ROW_BLOCK, 1, ROW_BLOCK), route[:, :SUBLANES][row_tok], xn_tiles,
                           w_gate[i].astype(BF16), w_up[i].astype(BF16), w_down[i].astype(BF16))
        h = _combine(dest.reshape(t // tmc, 1, tmc), h1, p[i].reshape(t, PLE_DIM), ple_proj[i].astype(BF16),
                     ple_norm[i].reshape(1, d), ple_gate[i].astype(BF16), y_tiles, tmc)
    return h.reshape(b, s, d)
```

```python
import functools
import math

import numpy as np
import jax
import jax.numpy as jnp
from jax import lax
from jax.experimental import pallas as pl
from jax.experimental.pallas import tpu as pltpu

F32 = jnp.float32
BF16 = jnp.bfloat16

EPS = 1e-6
MIN_FORGET = 1e-6
MASK_VALUE = -1e30

D_MODEL = 1024
HGRN_WIDTH = 512
HEAD = 128
N_HEADS = 4
DIFF_HEAD_DIM = 64
PLE_DIM = 256
N_EXPERTS = 16
EXPERTS_PER_GROUP = 4
N_GROUPS = 4
TOP_K = 2
D_FF = 512
IN_COLS = 4 * HGRN_WIDTH + 3 * HGRN_WIDTH

LANES = 128
SUBLANES = 8
ROW_TILES = D_MODEL // LANES
CHUNK = 128
ROW_BLOCK = 512
GATHER_UNROLL = 8
GATHER_SLOTS = 3
VMEM_LIMIT = 48 * 1024 * 1024
MAX_UNSHIFTED_SCORE = 40.0

assert ROW_TILES == SUBLANES


def _dot(a, b):
    return jnp.dot(a, b, preferred_element_type=F32)


def _dot_nt(a, b):
    return lax.dot_general(a, b, (((1,), (1,)), ((), ())), preferred_element_type=F32)


def _split_bf16(x):
    hi = x.astype(BF16)
    lo = (x - hi.astype(F32)).astype(BF16)
    return hi, lo


def _params(sem, vmem=VMEM_LIMIT):
    return pltpu.CompilerParams(dimension_semantics=sem, vmem_limit_bytes=vmem)


def _store_token_tiles(ref, x, n):
    for a in range(ROW_TILES):
        ref[pl.ds(a, n, stride=ROW_TILES), :] = x[:, a * LANES:(a + 1) * LANES]


def _load_token_tiles(ref, n, base=0):
    return [ref[pl.ds(base + a, n, stride=ROW_TILES), :] for a in range(ROW_TILES)]


def _inproj_kernel(h_ref, g_ref, w_ref, wvt_ref, gq_ref, gk_ref, grp_ref,
                   zq_ref, zf_ref, zi_ref, zg_ref, dq_ref, dk_ref, dvt_ref):
    x = h_ref[...]
    ms = jnp.mean(x * x, axis=-1, keepdims=True)
    hn = (x * lax.rsqrt(ms + EPS) * g_ref[...]).astype(BF16)
    w = HGRN_WIDTH
    for j, o_ref in enumerate((zq_ref, zf_ref, zi_ref, zg_ref)):
        o_ref[...] = _dot(hn, w_ref[:, j * w:(j + 1) * w]).astype(BF16)
    for j, o_ref, gain_ref in ((4, dq_ref, gq_ref), (5, dk_ref, gk_ref)):
        z = _dot(hn, w_ref[:, j * w:(j + 1) * w])
        ss = _dot((z * z).astype(BF16), grp_ref[...])
        o_ref[...] = (z * lax.rsqrt(ss * (1.0 / DIFF_HEAD_DIM) + EPS) * gain_ref[...]).astype(BF16)
    dvt_ref[0] = _dot_nt(wvt_ref[...], hn).astype(BF16)


def _inproj(h, gain, w_main_bf, w_vt_bf, gq, gk, grp, tm):
    t = h.shape[0]
    w = HGRN_WIDTH
    row = lambda i: (i, 0)
    const = lambda i: (0, 0)
    out = jax.ShapeDtypeStruct((t, w), BF16)
    return pl.pallas_call(
        _inproj_kernel,
        grid=(t // tm,),
        in_specs=[
            pl.BlockSpec((tm, D_MODEL), row),
            pl.BlockSpec((1, D_MODEL), const),
            pl.BlockSpec((D_MODEL, IN_COLS - w), const),
            pl.BlockSpec((w, D_MODEL), const),
            pl.BlockSpec((1, w), const),
            pl.BlockSpec((1, w), const),
            pl.BlockSpec((w, w), const),
        ],
        out_specs=[pl.BlockSpec((tm, w), row)] * 6 + [pl.BlockSpec((1, w, tm), lambda i: (i, 0, 0))],
        out_shape=[out] * 6 + [jax.ShapeDtypeStruct((t // tm, w, tm), BF16)],
        compiler_params=_params(("arbitrary",)),
        name="inproj",
    )(h, gain, w_main_bf, w_vt_bf, gq, gk, grp)


_PAIR_HALVES = (8, 16, 32, 64)
_N_EXP = 3 + len(_PAIR_HALVES)


def _hgrn_constants():
    c = CHUNK
    t = np.arange(c)[:, None]
    j = np.arange(c)[None, :]
    mats = [(j <= t).astype(np.float32), (j > t).astype(np.float32)]
    mid = t - t % 8 + 3
    m0 = np.where((t > mid) & (j > mid) & (j <= t), 1.0, 0.0) - np.where((t < mid) & (j > t) & (j <= mid), 1.0, 0.0)
    mats.append(m0.astype(np.float32))
    lvl = np.full((c, c), -1, np.int32)
    tt, ss = np.broadcast_arrays(t, j)
    lvl[(tt // 8 == ss // 8) & (ss <= tt)] = 0
    for n, m in enumerate(_PAIR_HALVES):
        b = t - t % (2 * m) + m - 1
        mats.append(np.where(t > b, (j > b) & (j <= t), (j > t) & (j <= b)).astype(np.float32))
        sel = (tt // (2 * m) == ss // (2 * m)) & (tt % (2 * m) >= m) & (ss % (2 * m) < m)
        lvl[sel] = n + 1
    return np.concatenate(mats, axis=0), lvl


def _hgrn_kernel(zq_ref, zf_ref, zi_ref, zg_ref, lb_ref, og_ref, m_ref, lvl_ref, o_ref, state_ref, *, n_chunks):
    @pl.when(pl.program_id(1) == 0)
    def _():
        state_ref[...] = jnp.zeros_like(state_ref)

    lb = lb_ref[...]
    og = og_ref[...]
    lvl = lvl_ref[...]
    level_masks = [lvl == n for n in range(1 + len(_PAIR_HALVES))]
    c = CHUNK

    def chunk(ci, carry):
        sl = pl.ds(pl.multiple_of(ci * c, c), c)
        xq = zq_ref[0, sl, :].astype(F32)
        fl = zf_ref[0, sl, :].astype(F32)
        xg = zg_ref[0, sl, :].astype(F32)
        q_all = xq * jax.nn.sigmoid(xq)
        forget = lb + (1.0 - lb) * jax.nn.sigmoid(fl)
        logf = jnp.log(jnp.maximum(forget, MIN_FORGET))
        k_all = 1.0 - forget
        gate_all = xg * jax.nn.sigmoid(xg)
        e_all = _dot(m_ref[...], jnp.concatenate(_split_bf16(logf), axis=0))
        e0 = e_all[2 * c:3 * c]
        q_lv = [(q_all * jnp.exp(e0)).astype(BF16)]
        k_lv = [(k_all * jnp.exp(-e0)).astype(BF16)]
        for n in range(len(_PAIR_HALVES)):
            p = jnp.exp(e_all[(3 + n) * c:(4 + n) * c])
            q_lv.append((q_all * p).astype(BF16))
            k_lv.append((k_all * p).astype(BF16))
        p_cum = jnp.exp(e_all[0:c])
        q_in = (q_all * p_cum).astype(BF16)
        k_out = (k_all * jnp.exp(e_all[c:2 * c])).astype(BF16)
        heads = [slice(hd * HEAD, (hd + 1) * HEAD) for hd in range(N_HEADS)]
        sc = [[_dot_nt(q_lv[n][:, hs], k_lv[n][:, hs]) for n in range(len(q_lv))] for hs in heads]
        s_bf = []
        for per_level in sc:
            s = jnp.where(level_masks[0], per_level[0], 0.0)
            for n in range(1, len(per_level)):
                s = jnp.where(level_masks[n], per_level[n], s)
            s_bf.append(s.astype(BF16))
        v_bf = [zi_ref[0, sl, hs] for hs in heads]
        states = [state_ref[hd] for hd in range(N_HEADS)]
        outs = [_dot_nt(q_in[:, hs], states[hd].astype(BF16)) + _dot(s_bf[hd], v_bf[hd])
                for hd, hs in enumerate(heads)]
        for hd, hs in enumerate(heads):
            v_t = v_bf[hd].astype(F32).T.astype(BF16)
            state_ref[hd] = states[hd] * p_cum[c - 1:c, hs] + _dot(v_t, k_out[:, hs])
        for hd, hs in enumerate(heads):
            o = outs[hd]
            ms = jnp.mean(o * o, axis=-1, keepdims=True)
            y = o * lax.rsqrt(ms + EPS) * og * gate_all[:, hs]
            o_ref[0, sl, hs] = y.astype(BF16)
        return carry

    lax.fori_loop(0, n_chunks, chunk, 0, unroll=2)


def _hgrn(zq, zf, zi, zg, lb, og, ts):
    b, s, _ = zq.shape
    mats, lvl = _hgrn_constants()
    blk = pl.BlockSpec((1, ts, HGRN_WIDTH), lambda bi, si: (bi, si, 0))
    const = lambda bi, si: (0, 0)
    return pl.pallas_call(
        functools.partial(_hgrn_kernel, n_chunks=ts // CHUNK),
        grid=(b, s // ts),
        in_specs=[blk, blk, blk, blk,
                  pl.BlockSpec((1, HGRN_WIDTH), const),
                  pl.BlockSpec((1, HEAD), const),
                  pl.BlockSpec((_N_EXP * CHUNK, 2 * CHUNK), const),
                  pl.BlockSpec((CHUNK, CHUNK), const)],
        out_specs=blk,
        out_shape=jax.ShapeDtypeStruct((b, s, HGRN_WIDTH), BF16),
        scratch_shapes=[pltpu.VMEM((N_HEADS, HEAD, HEAD), F32)],
        compiler_params=_params(("arbitrary", "arbitrary")),
        name="hgrn2",
    )(zq, zf, zi, zg, lb, og, jnp.asarray(np.concatenate([mats, mats], axis=1), BF16), jnp.asarray(lvl))


def _attn_kernel(fast_ref, q_ref, k_ref, vt_ref, lq1_ref, lk1_ref, lq2_ref, lk2_ref, sub_ref, o_ref,
                 m_sc, l_sc, acc_sc, *, tq, lam_init):
    qi = pl.program_id(1)
    lane = lax.broadcasted_iota(jnp.int32, (tq, HEAD), 1)
    q_maps = []
    for hd in range(N_HEADS):
        q = q_ref[0, :, hd * HEAD:(hd + 1) * HEAD]
        zero = jnp.zeros_like(q)
        q_maps.append((jnp.where(lane < DIFF_HEAD_DIM, q, zero), jnp.where(lane >= DIFF_HEAD_DIM, q, zero)))
    l_sc[...] = jnp.zeros_like(l_sc)
    acc_sc[...] = jnp.zeros_like(acc_sc)

    def sublane_tile_sum(p):
        return jnp.sum(p.reshape(tq // SUBLANES, SUBLANES, tq), axis=0)

    def scores(q_map, kb, masked):
        s = _dot_nt(kb, q_map)
        if masked:
            key = lax.broadcasted_iota(jnp.int32, s.shape, 0)
            qry = lax.broadcasted_iota(jnp.int32, s.shape, 1)
            s = jnp.where(key <= qry, s, MASK_VALUE)
        return s

    def run(update):
        def step(ki, masked):
            k_all = k_ref[0, pl.ds(pl.multiple_of(ki * tq, tq), tq), :]
            vt_all = vt_ref[ki]
            for hd in range(N_HEADS):
                kb = k_all[:, hd * HEAD:(hd + 1) * HEAD]
                vt = vt_all[hd * HEAD:(hd + 1) * HEAD, :]
                for c in range(2):
                    update(hd, c, scores(q_maps[hd][c], kb, masked), vt)

        def body(ki, carry):
            step(ki, False)
            return carry
        lax.fori_loop(0, qi, body, 0)
        step(qi, True)

    @pl.when(fast_ref[0] == 1)
    def _():
        def step(ki, masked):
            k_all = k_ref[0, pl.ds(pl.multiple_of(ki * tq, tq), tq), :]
            vt_all = vt_ref[ki]
            chains = [(hd, c) for hd in range(N_HEADS) for c in range(2)]
            ps = [jnp.exp(scores(q_maps[hd][c], k_all[:, hd * HEAD:(hd + 1) * HEAD], masked)) for hd, c in chains]
            for (hd, c), p in zip(chains, ps):
                l_sc[hd, c] += sublane_tile_sum(p)
                acc_sc[hd, c] += _dot(vt_all[hd * HEAD:(hd + 1) * HEAD, :], p.astype(BF16))

        def body(ki, carry):
            step(ki, False)
            return carry
        lax.fori_loop(0, qi, body, 0)
        step(qi, True)

    @pl.when(fast_ref[0] != 1)
    def _():
        m_sc[...] = jnp.full_like(m_sc, MASK_VALUE)

        def update(hd, c, s, vt):
            m_old = m_sc[hd, c]
            m_new = jnp.maximum(m_old, jnp.max(s, axis=0, keepdims=True))
            alpha = jnp.exp(m_old - m_new)
            p = jnp.exp(s - m_new)
            l_sc[hd, c] = alpha * l_sc[hd, c] + sublane_tile_sum(p)
            acc_sc[hd, c] = alpha * acc_sc[hd, c] + _dot(vt, p.astype(BF16))
            m_sc[hd, c] = m_new
        run(update)

    lam = (jnp.exp(jnp.sum(lq1_ref[...] * lk1_ref[...], keepdims=True))
           - jnp.exp(jnp.sum(lq2_ref[...] * lk2_ref[...], keepdims=True)) + lam_init)
    for hd in range(N_HEADS):
        l0 = jnp.sum(l_sc[hd, 0], axis=0, keepdims=True)
        l1 = jnp.sum(l_sc[hd, 1], axis=0, keepdims=True)
        o_t = acc_sc[hd, 0] / l0 - lam * (acc_sc[hd, 1] / l1)
        ms = jnp.mean(o_t * o_t, axis=0, keepdims=True)
        o = (o_t * lax.rsqrt(ms + EPS)).T
        o_ref[0, :, hd * HEAD:(hd + 1) * HEAD] = (o * sub_ref[...] * (1.0 - lam_init)).astype(BF16)


def _attn(fast, q, k, vt, lq1, lk1, lq2, lk2, sub, lam_init, tq):
    b, s, w = q.shape
    n_kv = s // tq
    qblk = pl.BlockSpec((1, tq, w), lambda bi, qi, f: (bi, qi, 0))
    kblk = pl.BlockSpec((1, s, w), lambda bi, qi, f: (bi, 0, 0))
    vtblk = pl.BlockSpec((n_kv, w, tq), lambda bi, qi, f: (bi, 0, 0))
    const = lambda bi, qi, f: (0, 0)
    lamspec = pl.BlockSpec((1, DIFF_HEAD_DIM), const)
    return pl.pallas_call(
        functools.partial(_attn_kernel, tq=tq, lam_init=lam_init),
        grid_spec=pltpu.PrefetchScalarGridSpec(
            num_scalar_prefetch=1,
            grid=(b, s // tq),
            in_specs=[qblk, kblk, vtblk, lamspec, lamspec, lamspec, lamspec, pl.BlockSpec((1, HEAD), const)],
            out_specs=qblk,
            scratch_shapes=[pltpu.VMEM((N_HEADS, 2, 1, tq), F32), pltpu.VMEM((N_HEADS, 2, SUBLANES, tq), F32),
                            pltpu.VMEM((N_HEADS, 2, HEAD, tq), F32)],
        ),
        out_shape=jax.ShapeDtypeStruct((b, s, w), BF16),
        compiler_params=_params(("arbitrary", "arbitrary")),
        name="diffattn",
    )(fast, q, k, vt, lq1, lk1, lq2, lk2, sub)


def _outproj_kernel(oh_ref, od_ref, h_ref, wo_ref, fg_ref, wr_ref, h1_ref, xn_ref, route_ref, *, tm):
    n_sub = 2 if tm % (2 * LANES) == 0 else 1
    sub = tm // n_sub
    for si in range(n_sub):
        rs = slice(si * sub, (si + 1) * sub)
        _outproj_rows(oh_ref[rs, :], od_ref[rs, :], h_ref[rs, :], wo_ref, fg_ref, wr_ref,
                      h1_ref.at[rs], xn_ref.at[pl.ds(si * sub * ROW_TILES, sub * ROW_TILES)], route_ref.at[rs], sub)


def _outproj_rows(oh, od, h, wo_ref, fg_ref, wr_ref, h1_ref, xn_ref, route_ref, tm):
    hw = HGRN_WIDTH
    h1 = h + _dot(oh, wo_ref[0:hw, :]) + _dot(od, wo_ref[hw:2 * hw, :])
    h1_ref[...] = h1
    ms = jnp.mean(h1 * h1, axis=-1, keepdims=True)
    xn = h1 * lax.rsqrt(ms + EPS) * fg_ref[...]
    _store_token_tiles(xn_ref, xn, tm)
    x_hi, x_lo = _split_bf16(xn)
    a = _dot(x_hi, wr_ref[...])
    logits = a + pltpu.roll(a, LANES - N_EXPERTS, axis=1) + _dot(x_lo, wr_ref[...])
    lt = logits.T[0:N_EXPERTS, :]
    ex = jnp.exp(lt - jnp.max(lt, axis=0, keepdims=True))
    aff = ex / jnp.sum(ex, axis=0, keepdims=True)
    rows = [aff[e:e + 1, :] for e in range(N_EXPERTS)]

    def top2(a, first):
        m1, i1 = a[0], jnp.full_like(a[0], first)
        for j in range(1, len(a)):
            better = a[j] > m1
            m1 = jnp.where(better, a[j], m1)
            i1 = jnp.where(better, float(first + j), i1)
        m2, i2 = jnp.full_like(m1, -1.0), jnp.zeros_like(m1)
        for j in range(len(a)):
            cand = jnp.where(i1 == float(first + j), -1.0, a[j])
            better = cand > m2
            m2 = jnp.where(better, cand, m2)
            i2 = jnp.where(better, float(first + j), i2)
        return m1, i1, m2, i2

    best = None
    for g in range(N_GROUPS):
        first = g * EXPERTS_PER_GROUP
        m1, i1, m2, i2 = top2(rows[first:first + EXPERTS_PER_GROUP], first)
        cand = (m1, i1 - first, m2, i2 - first, jnp.full_like(m1, float(g)))
        score = m1 + m2
        if best is None:
            best, best_score = cand, score
        else:
            better = score > best_score
            best = tuple(jnp.where(better, c, b) for c, b in zip(cand, best))
            best_score = jnp.where(better, score, best_score)
    m1, j1, m2, j2, gsel = best
    denom = m1 + m2
    w1, w2 = m1 / denom, m2 / denom
    rid = lax.broadcasted_iota(jnp.int32, (SUBLANES, tm), 0).astype(F32)
    r8 = jnp.where(rid == j1, w1, jnp.where(rid == j2, w2, 0.0))
    r8 = jnp.where(rid == float(EXPERTS_PER_GROUP), gsel, r8)
    route_t = jnp.concatenate([r8, jnp.zeros((LANES - SUBLANES, tm), F32)], axis=0)
    route_ref[...] = route_t.T


def _outproj(oh, od, h, wo_bf, fg, wr_pack, tm):
    t = h.shape[0]
    row = lambda i: (i, 0)
    const = lambda i: (0, 0)
    return pl.pallas_call(
        functools.partial(_outproj_kernel, tm=tm),
        grid=(t // tm,),
        in_specs=[
            pl.BlockSpec((tm, HGRN_WIDTH), row),
            pl.BlockSpec((tm, HGRN_WIDTH), row),
            pl.BlockSpec((tm, D_MODEL), row),
            pl.BlockSpec((D_MODEL, D_MODEL), const),
            pl.BlockSpec((1, D_MODEL), const),
            pl.BlockSpec((D_MODEL, LANES), const),
        ],
        out_specs=[pl.BlockSpec((tm, D_MODEL), row), pl.BlockSpec((tm * ROW_TILES, LANES), row),
                   pl.BlockSpec((tm, LANES), row)],
        out_shape=[jax.ShapeDtypeStruct((t, D_MODEL), F32), jax.ShapeDtypeStruct((t * ROW_TILES, LANES), F32),
                   jax.ShapeDtypeStruct((t, LANES), F32)],
        compiler_params=_params(("arbitrary",)),
        name="outproj_router",
    )(oh, od, h, wo_bf, fg, wr_pack)


def _tile_copy(src, src_row, dst, dst_row, sem):
    return pltpu.make_async_copy(src.at[pl.ds(pl.multiple_of(src_row * ROW_TILES, ROW_TILES), ROW_TILES)],
                                 dst.at[pl.ds(pl.multiple_of(dst_row * ROW_TILES, ROW_TILES), ROW_TILES)], sem)


def _gather_rows(idx_ref, n, src_hbm, dst, sem):
    def issue(jo, carry):
        for u in range(GATHER_UNROLL):
            j = jo * GATHER_UNROLL + u
            _tile_copy(src_hbm, idx_ref[0, 0, j], dst, j, sem).start(priority=u % 2)
        return carry
    lax.fori_loop(0, n // GATHER_UNROLL, issue, 0)


def _gather_rows_inline(idx_ref, n, src_hbm, dst, sem):
    for j in range(n):
        _tile_copy(src_hbm, idx_ref[0, 0, j], dst, j, sem).start(priority=j % 2)


def _wait_rows(n, src_hbm, dst, sem):
    pltpu.make_async_copy(src_hbm.at[pl.ds(0, n * ROW_TILES)], dst, sem).wait()


def _expert_kernel(blk_g_ref, tok0_ref, tok1_ref, tok2_ref, xn_hbm, wrow_ref, wg_ref, wu_ref, wd_ref, y_ref,
                   xbuf, sem):
    del blk_g_ref
    i = pl.program_id(0)
    slot = i % GATHER_SLOTS
    ahead = (i + 2) % GATHER_SLOTS

    @pl.when(i == 0)
    def _():
        _gather_rows(tok0_ref, ROW_BLOCK, xn_hbm, xbuf.at[0], sem.at[0])
        _gather_rows(tok1_ref, ROW_BLOCK, xn_hbm, xbuf.at[1], sem.at[1])

    _wait_rows(ROW_BLOCK, xn_hbm, xbuf.at[slot], sem.at[slot])
    x = jnp.concatenate(_load_token_tiles(xbuf.at[slot], ROW_BLOCK), axis=1).astype(BF16)
    _gather_rows_inline(tok2_ref, ROW_BLOCK, xn_hbm, xbuf.at[ahead], sem.at[ahead])
    wrow = wrow_ref[...]
    y = None
    for j in range(EXPERTS_PER_GROUP):
        g = _dot(x, wg_ref[j])
        u = _dot(x, wu_ref[j])
        wj = wrow[:, j:j + 1]
        mid = jnp.where(wj != 0.0, (g * jax.nn.sigmoid(g)) * u * wj, 0.0)
        part = _dot(mid.astype(BF16), wd_ref[j])
        y = part if y is None else y + part
    _store_token_tiles(y_ref, y, ROW_BLOCK)

    @pl.when(i == pl.num_programs(0) - 1)
    def _():
        for extra in (1, 2):
            s_extra = (i + extra) % GATHER_SLOTS
            _wait_rows(ROW_BLOCK, xn_hbm, xbuf.at[s_extra], sem.at[s_extra])


def _experts(blk_g, row_tok, row_w, xn_tiles, wg, wu, wd):
    n_blk = row_tok.shape[0]
    assert n_blk >= GATHER_SLOTS
    wsel = lambda i, bg: (bg[i], 0, 0)

    def tok_spec(ahead):
        return pl.BlockSpec((1, 1, ROW_BLOCK), lambda i, be: (jnp.minimum(i + ahead, n_blk - 1), 0, 0),
                            memory_space=pltpu.SMEM)

    return pl.pallas_call(
        _expert_kernel,
        grid_spec=pltpu.PrefetchScalarGridSpec(
            num_scalar_prefetch=1,
            grid=(n_blk,),
            in_specs=[tok_spec(0), tok_spec(1), tok_spec(2),
                      pl.BlockSpec(memory_space=pl.ANY),
                      pl.BlockSpec((ROW_BLOCK, EXPERTS_PER_GROUP), lambda i, bg: (i, 0)),
                      pl.BlockSpec((EXPERTS_PER_GROUP, D_MODEL, D_FF), wsel),
                      pl.BlockSpec((EXPERTS_PER_GROUP, D_MODEL, D_FF), wsel),
                      pl.BlockSpec((EXPERTS_PER_GROUP, D_FF, D_MODEL), wsel)],
            out_specs=pl.BlockSpec((ROW_BLOCK * ROW_TILES, LANES), lambda i, bg: (i, 0)),
            scratch_shapes=[pltpu.VMEM((GATHER_SLOTS, ROW_BLOCK * ROW_TILES, LANES), F32),
                            pltpu.SemaphoreType.DMA((GATHER_SLOTS,))],
        ),
        out_shape=jax.ShapeDtypeStruct((n_blk * ROW_BLOCK * ROW_TILES, LANES), F32),
        compiler_params=_params(("arbitrary",)),
        name="moe_experts",
    )(blk_g, row_tok, row_tok, row_tok, xn_tiles, row_w, wg, wu, wd)


def _combine_kernel(dest_ref, dest_next_ref, h1_ref, p_ref, pp_ref, pn_ref, pg_ref, y_hbm, o_ref,
                    ybuf, sem, *, tm):
    i = pl.program_id(0)
    slot = i % 2

    @pl.when(i == 0)
    def _():
        _gather_rows(dest_ref, tm, y_hbm, ybuf.at[0], sem.at[0])

    _gather_rows_inline(dest_next_ref, tm, y_hbm, ybuf.at[1 - slot], sem.at[1 - slot])

    e = _dot(p_ref[...].astype(BF16), pp_ref[...])
    ms = jnp.mean(e * e, axis=-1, keepdims=True)
    ple = e * lax.rsqrt(ms + EPS) * pn_ref[...]

    _wait_rows(tm, y_hbm, ybuf.at[slot], sem.at[slot])
    h2 = h1_ref[...] + jnp.concatenate(_load_token_tiles(ybuf.at[slot], tm), axis=1)
    gate = jax.nn.sigmoid(_dot(h2.astype(BF16), pg_ref[...]))
    o_ref[...] = h2 + ple * gate

    @pl.when(i == pl.num_programs(0) - 1)
    def _():
        _wait_rows(tm, y_hbm, ybuf.at[1 - slot], sem.at[1 - slot])


def _combine(dest3, h1, p, pp_bf, pn, pg_bf, y_tiles, tm):
    t = h1.shape[0]
    n_tiles = t // tm
    row = lambda i: (i, 0)
    const = lambda i: (0, 0)
    n = tm
    return pl.pallas_call(
        functools.partial(_combine_kernel, tm=tm),
        grid=(n_tiles,),
        in_specs=[pl.BlockSpec((1, 1, n), lambda i: (i, 0, 0), memory_space=pltpu.SMEM),
                  pl.BlockSpec((1, 1, n), lambda i: (jnp.minimum(i + 1, n_tiles - 1), 0, 0), memory_space=pltpu.SMEM),
                  pl.BlockSpec((tm, D_MODEL), row),
                  pl.BlockSpec((tm, PLE_DIM), row),
                  pl.BlockSpec((PLE_DIM, D_MODEL), const),
                  pl.BlockSpec((1, D_MODEL), const),
                  pl.BlockSpec((D_MODEL, D_MODEL), const),
                  pl.BlockSpec(memory_space=pl.ANY)],
        out_specs=pl.BlockSpec((tm, D_MODEL), row),
        out_shape=jax.ShapeDtypeStruct((t, D_MODEL), F32),
        scratch_shapes=[pltpu.VMEM((2, n * ROW_TILES, LANES), F32), pltpu.SemaphoreType.DMA((2,))],
        compiler_params=_params(("arbitrary",)),
        name="moe_combine_ple",
    )(dest3, dest3, h1, p, pp_bf, pn, pg_bf, y_tiles)


def _routing_tables(route, n_rows):
    tok_g = route[:, EXPERTS_PER_GROUP].astype(jnp.int32)
    n_tok = tok_g.shape[0]
    groups = jnp.arange(N_GROUPS, dtype=jnp.int32)
    onehot = (tok_g[:, None] == groups[None, :]).astype(jnp.int32)
    csum = jnp.cumsum(onehot, axis=0)
    rank = jnp.sum(csum * onehot, axis=1) - 1
    counts = csum[-1]
    starts = jnp.cumsum(counts) - counts
    padded = (counts + ROW_BLOCK - 1) // ROW_BLOCK * ROW_BLOCK
    padded_end = jnp.cumsum(padded)
    padded_start = padded_end - padded
    dest = jnp.sum(onehot * padded_start[None, :], axis=1) + rank
    n_blk = n_rows // ROW_BLOCK
    blk_start = jnp.arange(n_blk, dtype=jnp.int32) * ROW_BLOCK
    blk_g = jnp.sum((blk_start[:, None] >= padded_end[None, :]).astype(jnp.int32), axis=1)
    blk_g = jnp.minimum(blk_g, N_GROUPS - 1).astype(jnp.int32)
    order = jnp.argsort(tok_g, stable=True).astype(jnp.int32)
    row = jnp.arange(n_rows, dtype=jnp.int32)
    row_g = jnp.repeat(blk_g, ROW_BLOCK)
    idx = row - padded_start[row_g]
    src = order[jnp.clip(starts[row_g] + idx, 0, n_tok - 1)]
    row_tok = jnp.where(idx < counts[row_g], src, 0)
    return dest.astype(jnp.int32), row_tok.astype(jnp.int32), blk_g


def _tile(n, pref):
    return pref if n % pref == 0 else n


def kernel(x, p, mix_norm, w_in, hgrn_lb, hgrn_out_norm, q_norm, k_norm, lam_q1, lam_k1, lam_q2, lam_k2,
           diff_subln, w_out, ffn_norm, w_router, w_gate, w_up, w_down, ple_proj, ple_norm, ple_gate):
    b, s, d = x.shape
    depth = w_in.shape[0]
    t = b * s
    tm = _tile(t, 512)
    ts = _tile(s, 1024)
    tq = _tile(s, 512)
    tmc = _tile(t, 256)
    n_rows = t + N_GROUPS * ROW_BLOCK

    lb_soft = jax.nn.softmax(hgrn_lb.astype(F32), axis=0)
    lower_bounds = jnp.cumsum(lb_soft, axis=0) - lb_soft[0]
    grp = np.kron(np.eye(HGRN_WIDTH // DIFF_HEAD_DIM), np.ones((DIFF_HEAD_DIM, DIFF_HEAD_DIM)))
    grp = jnp.asarray(grp, BF16)
    wr_hi, wr_lo = _split_bf16(w_router.astype(F32))
    wr_pack = jnp.pad(jnp.concatenate([wr_hi, wr_lo], axis=1), ((0, 0), (0, LANES - 2 * N_EXPERTS)))
    n_rep = HGRN_WIDTH // DIFF_HEAD_DIM

    h = x.reshape(t, d)
    for i in range(depth):
        lam_init = 0.8 - 0.6 * math.exp(-0.3 * i)
        gq = (jnp.tile(q_norm[i], n_rep) * DIFF_HEAD_DIM ** -0.5).reshape(1, HGRN_WIDTH)
        gk = jnp.tile(k_norm[i], n_rep).reshape(1, HGRN_WIDTH)
        score_bound = 1.02 * DIFF_HEAD_DIM ** 0.5 * jnp.max(jnp.abs(q_norm[i])) * jnp.max(jnp.abs(k_norm[i]))
        fast = (score_bound <= MAX_UNSHIFTED_SCORE).astype(jnp.int32).reshape(1)
        w_bf = w_in[i].astype(BF16)
        n_main = IN_COLS - HGRN_WIDTH
        zq, zf, zi, zg, dq, dk, dvt = _inproj(h, mix_norm[i].reshape(1, d), w_bf[:, :n_main], w_bf[:, n_main:].T,
                                              gq, gk, grp, tq)
        r3 = lambda a: a.reshape(b, s, HGRN_WIDTH)
        o_hgrn = _hgrn(r3(zq), r3(zf), r3(zi), r3(zg), lower_bounds[i].reshape(1, HGRN_WIDTH),
                       hgrn_out_norm[i].reshape(1, HEAD), ts)
        row64 = lambda a: a.reshape(1, DIFF_HEAD_DIM)
        o_diff = _attn(fast, r3(dq), r3(dk), dvt, row64(lam_q1[i]), row64(lam_k1[i]), row64(lam_q2[i]),
                       row64(lam_k2[i]), diff_subln[i].reshape(1, HEAD), lam_init, tq)
        h1, xn_tiles, route = _outproj(o_hgrn.reshape(t, HGRN_WIDTH), o_diff.reshape(t, HGRN_WIDTH), h,
                                       w_out[i].astype(BF16), ffn_norm[i].reshape(1, d), wr_pack, tm)
        dest, row_tok, blk_g = _routing_tables(route, n_rows)
        row_w = jnp.stack([route[:, j][row_tok] for j in range(EXPERTS_PER_GROUP)], axis=1)
        y_tiles = _experts(blk_g, row_tok.reshape(n_rows // ROW_BLOCK, 1, ROW_BLOCK), row_w, xn_tiles,
                           w_gate[i].astype(BF16), w_up[i].astype(BF16), w_down[i].astype(BF16))
        h = _combine(dest.reshape(t // tmc, 1, tmc), h1, p[i].reshape(t, PLE_DIM), ple_proj[i].astype(BF16),
                     ple_norm[i].reshape(1, d), ple_gate[i].astype(BF16), y_tiles, tmc)
    return h.reshape(b, s, d)
```

```python
import functools
import math

import numpy as np
import jax
import jax.numpy as jnp
from jax import lax
from jax.experimental import pallas as pl
from jax.experimental.pallas import tpu as pltpu

F32 = jnp.float32
BF16 = jnp.bfloat16

EPS = 1e-6
MIN_FORGET = 1e-6
MASK_VALUE = -1e30

D_MODEL = 1024
HGRN_WIDTH = 512
HEAD = 128
N_HEADS = 4
DIFF_HEAD_DIM = 64
PLE_DIM = 256
N_EXPERTS = 16
EXPERTS_PER_GROUP = 4
N_GROUPS = 4
TOP_K = 2
D_FF = 512
IN_COLS = 4 * HGRN_WIDTH + 3 * HGRN_WIDTH

LANES = 128
SUBLANES = 8
ROW_TILES = D_MODEL // LANES
CHUNK = 128
ROW_BLOCK = 512
GATHER_UNROLL = 8
GATHER_SLOTS = 3
VMEM_LIMIT = 48 * 1024 * 1024
MAX_UNSHIFTED_SCORE = 40.0

assert ROW_TILES == SUBLANES


def _dot(a, b):
    return jnp.dot(a, b, preferred_element_type=F32)


def _dot_nt(a, b):
    return lax.dot_general(a, b, (((1,), (1,)), ((), ())), preferred_element_type=F32)


def _split_bf16(x):
    hi = x.astype(BF16)
    lo = (x - hi.astype(F32)).astype(BF16)
    return hi, lo


def _params(sem, vmem=VMEM_LIMIT):
    return pltpu.CompilerParams(dimension_semantics=sem, vmem_limit_bytes=vmem)


def _store_token_tiles(ref, x, n):
    for a in range(ROW_TILES):
        ref[pl.ds(a, n, stride=ROW_TILES), :] = x[:, a * LANES:(a + 1) * LANES]


def _load_token_tiles(ref, n, base=0):
    return [ref[pl.ds(base + a, n, stride=ROW_TILES), :] for a in range(ROW_TILES)]


def _inproj_kernel(h_ref, g_ref, w_ref, wvt_ref, gq_ref, gk_ref, grp_ref,
                   zq_ref, zf_ref, zi_ref, zg_ref, dq_ref, dk_ref, dvt_ref):
    x = h_ref[...]
    ms = jnp.mean(x * x, axis=-1, keepdims=True)
    hn = (x * lax.rsqrt(ms + EPS) * g_ref[...]).astype(BF16)
    w = HGRN_WIDTH
    for j, o_ref in enumerate((zq_ref, zf_ref, zi_ref, zg_ref)):
        o_ref[...] = _dot(hn, w_ref[:, j * w:(j + 1) * w]).astype(BF16)
    for j, o_ref, gain_ref in ((4, dq_ref, gq_ref), (5, dk_ref, gk_ref)):
        z = _dot(hn, w_ref[:, j * w:(j + 1) * w])
        ss = _dot((z * z).astype(BF16), grp_ref[...])
        o_ref[...] = (z * lax.rsqrt(ss * (1.0 / DIFF_HEAD_DIM) + EPS) * gain_ref[...]).astype(BF16)
    dvt_ref[0] = _dot_nt(wvt_ref[...], hn).astype(BF16)


def _inproj(h, gain, w_main_bf, w_vt_bf, gq, gk, grp, tm):
    t = h.shape[0]
    w = HGRN_WIDTH
    row = lambda i: (i, 0)
    const = lambda i: (0, 0)
    out = jax.ShapeDtypeStruct((t, w), BF16)
    return pl.pallas_call(
        _inproj_kernel,
        grid=(t // tm,),
        in_specs=[
            pl.BlockSpec((tm, D_MODEL), row),
            pl.BlockSpec((1, D_MODEL), const),
            pl.BlockSpec((D_MODEL, IN_COLS - w), const),
            pl.BlockSpec((w, D_MODEL), const),
            pl.BlockSpec((1, w), const),
            pl.BlockSpec((1, w), const),
            pl.BlockSpec((w, w), const),
        ],
        out_specs=[pl.BlockSpec((tm, w), row)] * 6 + [pl.BlockSpec((1, w, tm), lambda i: (i, 0, 0))],
        out_shape=[out] * 6 + [jax.ShapeDtypeStruct((t // tm, w, tm), BF16)],
        compiler_params=_params(("arbitrary",)),
        name="inproj",
    )(h, gain, w_main_bf, w_vt_bf, gq, gk, grp)


_PAIR_HALVES = (8, 16, 32, 64)
_N_EXP = 3 + len(_PAIR_HALVES)


def _hgrn_constants():
    c = CHUNK
    t = np.arange(c)[:, None]
    j = np.arange(c)[None, :]
    mats = [(j <= t).astype(np.float32), (j > t).astype(np.float32)]
    mid = t - t % 8 + 3
    m0 = np.where((t > mid) & (j > mid) & (j <= t), 1.0, 0.0) - np.where((t < mid) & (j > t) & (j <= mid), 1.0, 0.0)
    mats.append(m0.astype(np.float32))
    lvl = np.full((c, c), -1, np.int32)
    tt, ss = np.broadcast_arrays(t, j)
    lvl[(tt // 8 == ss // 8) & (ss <= tt)] = 0
    for n, m in enumerate(_PAIR_HALVES):
        b = t - t % (2 * m) + m - 1
        mats.append(np.where(t > b, (j > b) & (j <= t), (j > t) & (j <= b)).astype(np.float32))
        sel = (tt // (2 * m) == ss // (2 * m)) & (tt % (2 * m) >= m) & (ss % (2 * m) < m)
        lvl[sel] = n + 1
    return np.concatenate(mats, axis=0), lvl


def _hgrn_kernel(zq_ref, zf_ref, zi_ref, zg_ref, lb_ref, og_ref, m_ref, lvl_ref, o_ref, state_ref, *, n_chunks):
    @pl.when(pl.program_id(1) == 0)
    def _():
        state_ref[...] = jnp.zeros_like(state_ref)

    lb = lb_ref[...]
    og = og_ref[...]
    lvl = lvl_ref[...]
    level_masks = [lvl == n for n in range(1 + len(_PAIR_HALVES))]
    c = CHUNK

    def chunk(ci, carry):
        sl = pl.ds(pl.multiple_of(ci * c, c), c)
        xq = zq_ref[0, sl, :].astype(F32)
        fl = zf_ref[0, sl, :].astype(F32)
        xg = zg_ref[0, sl, :].astype(F32)
        q_all = xq * jax.nn.sigmoid(xq)
        forget = lb + (1.0 - lb) * jax.nn.sigmoid(fl)
        logf = jnp.log(jnp.maximum(forget, MIN_FORGET))
        k_all = 1.0 - forget
        gate_all = xg * jax.nn.sigmoid(xg)
        e_all = _dot(m_ref[...], jnp.concatenate(_split_bf16(logf), axis=0))
        e0 = e_all[2 * c:3 * c]
        q_lv = [(q_all * jnp.exp(e0)).astype(BF16)]
        k_lv = [(k_all * jnp.exp(-e0)).astype(BF16)]
        for n in range(len(_PAIR_HALVES)):
            p = jnp.exp(e_all[(3 + n) * c:(4 + n) * c])
            q_lv.append((q_all * p).astype(BF16))
            k_lv.append((k_all * p).astype(BF16))
        p_cum = jnp.exp(e_all[0:c])
        q_in = (q_all * p_cum).astype(BF16)
        k_out = (k_all * jnp.exp(e_all[c:2 * c])).astype(BF16)
        heads = [slice(hd * HEAD, (hd + 1) * HEAD) for hd in range(N_HEADS)]
        sc = [[_dot_nt(q_lv[n][:, hs], k_lv[n][:, hs]) for n in range(len(q_lv))] for hs in heads]
        s_bf = []
        for per_level in sc:
            s = jnp.where(level_masks[0], per_level[0], 0.0)
            for n in range(1, len(per_level)):
                s = jnp.where(level_masks[n], per_level[n], s)
            s_bf.append(s.astype(BF16))
        v_bf = [zi_ref[0, sl, hs] for hs in heads]
        states = [state_ref[hd] for hd in range(N_HEADS)]
        outs = [_dot_nt(q_in[:, hs], states[hd].astype(BF16)) + _dot(s_bf[hd], v_bf[hd])
                for hd, hs in enumerate(heads)]
        for hd, hs in enumerate(heads):
            v_t = v_bf[hd].astype(F32).T.astype(BF16)
            state_ref[hd] = states[hd] * p_cum[c - 1:c, hs] + _dot(v_t, k_out[:, hs])
        for hd, hs in enumerate(heads):
            o = outs[hd]
            ms = jnp.mean(o * o, axis=-1, keepdims=True)
            y = o * lax.rsqrt(ms + EPS) * og * gate_all[:, hs]
            o_ref[0, sl, hs] = y.astype(BF16)
        return carry

    lax.fori_loop(0, n_chunks, chunk, 0, unroll=2)


def _hgrn(zq, zf, zi, zg, lb, og, ts):
    b, s, _ = zq.shape
    mats, lvl = _hgrn_constants()
    blk = pl.BlockSpec((1, ts, HGRN_WIDTH), lambda bi, si: (bi, si, 0))
    const = lambda bi, si: (0, 0)
    return pl.pallas_call(
        functools.partial(_hgrn_kernel, n_chunks=ts // CHUNK),
        grid=(b, s // ts),
        in_specs=[blk, blk, blk, blk,
                  pl.BlockSpec((1, HGRN_WIDTH), const),
                  pl.BlockSpec((1, HEAD), const),
                  pl.BlockSpec((_N_EXP * CHUNK, 2 * CHUNK), const),
                  pl.BlockSpec((CHUNK, CHUNK), const)],
        out_specs=blk,
        out_shape=jax.ShapeDtypeStruct((b, s, HGRN_WIDTH), BF16),
        scratch_shapes=[pltpu.VMEM((N_HEADS, HEAD, HEAD), F32)],
        compiler_params=_params(("arbitrary", "arbitrary")),
        name="hgrn2",
    )(zq, zf, zi, zg, lb, og, jnp.asarray(np.concatenate([mats, mats], axis=1), BF16), jnp.asarray(lvl))


def _attn_kernel(fast_ref, q_ref, k_ref, vt_ref, lq1_ref, lk1_ref, lq2_ref, lk2_ref, sub_ref, o_ref,
                 m_sc, l_sc, acc_sc, *, tq, lam_init):
    qi = pl.program_id(1)
    lane = lax.broadcasted_iota(jnp.int32, (tq, HEAD), 1)
    q_maps = []
    for hd in range(N_HEADS):
        q = q_ref[0, :, hd * HEAD:(hd + 1) * HEAD]
        zero = jnp.zeros_like(q)
        q_maps.append((jnp.where(lane < DIFF_HEAD_DIM, q, zero), jnp.where(lane >= DIFF_HEAD_DIM, q, zero)))
    l_sc[...] = jnp.zeros_like(l_sc)
    acc_sc[...] = jnp.zeros_like(acc_sc)

    def sublane_tile_sum(p):
        return jnp.sum(p.reshape(tq // SUBLANES, SUBLANES, tq), axis=0)

    def scores(q_map, kb, masked):
        s = _dot_nt(kb, q_map)
        if masked:
            key = lax.broadcasted_iota(jnp.int32, s.shape, 0)
            qry = lax.broadcasted_iota(jnp.int32, s.shape, 1)
            s = jnp.where(key <= qry, s, MASK_VALUE)
        return s

    def run(update):
        def step(ki, masked):
            k_all = k_ref[0, pl.ds(pl.multiple_of(ki * tq, tq), tq), :]
            vt_all = vt_ref[ki]
            for hd in range(N_HEADS):
                kb = k_all[:, hd * HEAD:(hd + 1) * HEAD]
                vt = vt_all[hd * HEAD:(hd + 1) * HEAD, :]
                for c in range(2):
                    update(hd, c, scores(q_maps[hd][c], kb, masked), vt)

        def body(ki, carry):
            step(ki, False)
            return carry
        lax.fori_loop(0, qi, body, 0)
        step(qi, True)

    @pl.when(fast_ref[0] == 1)
    def _():
        def step(ki, masked):
            k_all = k_ref[0, pl.ds(pl.multiple_of(ki * tq, tq), tq), :]
            vt_all = vt_ref[ki]
            chains = [(hd, c) for hd in range(N_HEADS) for c in range(2)]
            ps = [jnp.exp(scores(q_maps[hd][c], k_all[:, hd * HEAD:(hd + 1) * HEAD], masked)) for hd, c in chains]
            for (hd, c), p in zip(chains, ps):
                l_sc[hd, c] += sublane_tile_sum(p)
                acc_sc[hd, c] += _dot(vt_all[hd * HEAD:(hd + 1) * HEAD, :], p.astype(BF16))

        def body(ki, carry):
            step(ki, False)
            return carry
        lax.fori_loop(0, qi, body, 0)
        step(qi, True)

    @pl.when(fast_ref[0] != 1)
    def _():
        m_sc[...] = jnp.full_like(m_sc, MASK_VALUE)

        def update(hd, c, s, vt):
            m_old = m_sc[hd, c]
            m_new = jnp.maximum(m_old, jnp.max(s, axis=0, keepdims=True))
            alpha = jnp.exp(m_old - m_new)
            p = jnp.exp(s - m_new)
            l_sc[hd, c] = alpha * l_sc[hd, c] + sublane_tile_sum(p)
            acc_sc[hd, c] = alpha * acc_sc[hd, c] + _dot(vt, p.astype(BF16))
            m_sc[hd, c] = m_new
        run(update)

    lam = (jnp.exp(jnp.sum(lq1_ref[...] * lk1_ref[...], keepdims=True))
           - jnp.exp(jnp.sum(lq2_ref[...] * lk2_ref[...], keepdims=True)) + lam_init)
    for hd in range(N_HEADS):
        l0 = jnp.sum(l_sc[hd, 0], axis=0, keepdims=True)
        l1 = jnp.sum(l_sc[hd, 1], axis=0, keepdims=True)
        o_t = acc_sc[hd, 0] / l0 - lam * (acc_sc[hd, 1] / l1)
        ms = jnp.mean(o_t * o_t, axis=0, keepdims=True)
        o = (o_t * lax.rsqrt(ms + EPS)).T
        o_ref[0, :, hd * HEAD:(hd + 1) * HEAD] = (o * sub_ref[...] * (1.0 - lam_init)).astype(BF16)


def _attn(fast, q, k, vt, lq1, lk1, lq2, lk2, sub, lam_init, tq):
    b, s, w = q.shape
    n_kv = s // tq
    qblk = pl.BlockSpec((1, tq, w), lambda bi, qi, f: (bi, qi, 0))
    kblk = pl.BlockSpec((1, s, w), lambda bi, qi, f: (bi, 0, 0))
    vtblk = pl.BlockSpec((n_kv, w, tq), lambda bi, qi, f: (bi, 0, 0))
    const = lambda bi, qi, f: (0, 0)
    lamspec = pl.BlockSpec((1, DIFF_HEAD_DIM), const)
    return pl.pallas_call(
        functools.partial(_attn_kernel, tq=tq, lam_init=lam_init),
        grid_spec=pltpu.PrefetchScalarGridSpec(
            num_scalar_prefetch=1,
            grid=(b, s // tq),
            in_specs=[qblk, kblk, vtblk, lamspec, lamspec, lamspec, lamspec, pl.BlockSpec((1, HEAD), const)],
            out_specs=qblk,
            scratch_shapes=[pltpu.VMEM((N_HEADS, 2, 1, tq), F32), pltpu.VMEM((N_HEADS, 2, SUBLANES, tq), F32),
                            pltpu.VMEM((N_HEADS, 2, HEAD, tq), F32)],
        ),
        out_shape=jax.ShapeDtypeStruct((b, s, w), BF16),
        compiler_params=_params(("arbitrary", "arbitrary")),
        name="diffattn",
    )(fast, q, k, vt, lq1, lk1, lq2, lk2, sub)


def _outproj_kernel(oh_ref, od_ref, h_ref, wo_ref, fg_ref, wr_ref, h1_ref, xn_ref, route_ref, *, tm):
    n_sub = 2 if tm % (2 * LANES) == 0 else 1
    sub = tm // n_sub
    for si in range(n_sub):
        rs = slice(si * sub, (si + 1) * sub)
        _outproj_rows(oh_ref[rs, :], od_ref[rs, :], h_ref[rs, :], wo_ref, fg_ref, wr_ref,
                      h1_ref.at[rs], xn_ref.at[pl.ds(si * sub * ROW_TILES, sub * ROW_TILES)], route_ref.at[rs], sub)


def _outproj_rows(oh, od, h, wo_ref, fg_ref, wr_ref, h1_ref, xn_ref, route_ref, tm):
    hw = HGRN_WIDTH
    h1 = h + _dot(oh, wo_ref[0:hw, :]) + _dot(od, wo_ref[hw:2 * hw, :])
    h1_ref[...] = h1
    ms = jnp.mean(h1 * h1, axis=-1, keepdims=True)
    xn = h1 * lax.rsqrt(ms + EPS) * fg_ref[...]
    _store_token_tiles(xn_ref, xn, tm)
    x_hi, x_lo = _split_bf16(xn)
    a = _dot(x_hi, wr_ref[...])
    logits = a + pltpu.roll(a, LANES - N_EXPERTS, axis=1) + _dot(x_lo, wr_ref[...])
    lt = logits.T[0:N_EXPERTS, :]
    ex = jnp.exp(lt - jnp.max(lt, axis=0, keepdims=True))
    aff = ex / jnp.sum(ex, axis=0, keepdims=True)
    rows = [aff[e:e + 1, :] for e in range(N_EXPERTS)]

    def top2(a, first):
        m1, i1 = a[0], jnp.full_like(a[0], first)
        for j in range(1, len(a)):
            better = a[j] > m1
            m1 = jnp.where(better, a[j], m1)
            i1 = jnp.where(better, float(first + j), i1)
        m2, i2 = jnp.full_like(m1, -1.0), jnp.zeros_like(m1)
        for j in range(len(a)):
            cand = jnp.where(i1 == float(first + j), -1.0, a[j])
            better = cand > m2
            m2 = jnp.where(better, cand, m2)
            i2 = jnp.where(better, float(first + j), i2)
        return m1, i1, m2, i2

    best = None
    for g in range(N_GROUPS):
        first = g * EXPERTS_PER_GROUP
        m1, i1, m2, i2 = top2(rows[first:first + EXPERTS_PER_GROUP], first)
        cand = (m1, i1 - first, m2, i2 - first, jnp.full_like(m1, float(g)))
        score = m1 + m2
        if best is None:
            best, best_score = cand, score
        else:
            better = score > best_score
            best = tuple(jnp.where(better, c, b) for c, b in zip(cand, best))
            best_score = jnp.where(better, score, best_score)
    m1, j1, m2, j2, gsel = best
    denom = m1 + m2
    w1, w2 = m1 / denom, m2 / denom
    rid = lax.broadcasted_iota(jnp.int32, (SUBLANES, tm), 0).astype(F32)
    r8 = jnp.where(rid == j1, w1, jnp.where(rid == j2, w2, 0.0))
    r8 = jnp.where(rid == float(EXPERTS_PER_GROUP), gsel, r8)
    route_t = jnp.concatenate([r8, jnp.zeros((LANES - SUBLANES, tm), F32)], axis=0)
    route_ref[...] = route_t.T


def _outproj(oh, od, h, wo_bf, fg, wr_pack, tm):
    t = h.shape[0]
    row = lambda i: (i, 0)
    const = lambda i: (0, 0)
    return pl.pallas_call(
        functools.partial(_outproj_kernel, tm=tm),
        grid=(t // tm,),
        in_specs=[
            pl.BlockSpec((tm, HGRN_WIDTH), row),
            pl.BlockSpec((tm, HGRN_WIDTH), row),
            pl.BlockSpec((tm, D_MODEL), row),
            pl.BlockSpec((D_MODEL, D_MODEL), const),
            pl.BlockSpec((1, D_MODEL), const),
            pl.BlockSpec((D_MODEL, LANES), const),
        ],
        out_specs=[pl.BlockSpec((tm, D_MODEL), row), pl.BlockSpec((tm * ROW_TILES, LANES), row),
                   pl.BlockSpec((tm, LANES), row)],
        out_shape=[jax.ShapeDtypeStruct((t, D_MODEL), F32), jax.ShapeDtypeStruct((t * ROW_TILES, LANES), F32),
                   jax.ShapeDtypeStruct((t, LANES), F32)],
        compiler_params=_params(("arbitrary",)),
        name="outproj_router",
    )(oh, od, h, wo_bf, fg, wr_pack)


def _tile_copy(src, src_row, dst, dst_row, sem):
    return pltpu.make_async_copy(src.at[pl.ds(pl.multiple_of(src_row * ROW_TILES, ROW_TILES), ROW_TILES)],
                                 dst.at[pl.ds(pl.multiple_of(dst_row * ROW_TILES, ROW_TILES), ROW_TILES)], sem)


def _gather_rows(idx_ref, n, src_hbm, dst, sem):
    def issue(jo, carry):
        for u in range(GATHER_UNROLL):
            j = jo * GATHER_UNROLL + u
            _tile_copy(src_hbm, idx_ref[0, 0, j], dst, j, sem).start(priority=u % 2)
        return carry
    lax.fori_loop(0, n // GATHER_UNROLL, issue, 0)


def _gather_rows_inline(idx_ref, n, src_hbm, dst, sem, first=0):
    for j in range(first, first + n):
        _tile_copy(src_hbm, idx_ref[0, 0, j], dst, j, sem).start(priority=j % 2)


def _wait_rows(n, src_hbm, dst, sem):
    pltpu.make_async_copy(src_hbm.at[pl.ds(0, n * ROW_TILES)], dst, sem).wait()


def _expert_kernel(blk_g_ref, tok0_ref, tok1_ref, tok2_ref, xn_hbm, wrow_ref, wg_ref, wu_ref, wd_ref, y_ref,
                   xbuf, sem):
    del blk_g_ref
    i = pl.program_id(0)
    slot = i % GATHER_SLOTS
    ahead = (i + 2) % GATHER_SLOTS

    @pl.when(i == 0)
    def _():
        _gather_rows(tok0_ref, ROW_BLOCK, xn_hbm, xbuf.at[0], sem.at[0])
        _gather_rows(tok1_ref, ROW_BLOCK, xn_hbm, xbuf.at[1], sem.at[1])

    _wait_rows(ROW_BLOCK, xn_hbm, xbuf.at[slot], sem.at[slot])
    wrow = wrow_ref[...]
    y = None
    n_batches = EXPERTS_PER_GROUP - 1
    batch = [ROW_BLOCK * (j + 1) // n_batches - ROW_BLOCK * j // n_batches for j in range(n_batches)] + [0]
    for j in range(EXPERTS_PER_GROUP):
        x = jnp.concatenate(_load_token_tiles(xbuf.at[slot], ROW_BLOCK), axis=1).astype(BF16)
        _gather_rows_inline(tok2_ref, batch[j], xn_hbm, xbuf.at[ahead], sem.at[ahead], first=sum(batch[:j]))
        g = _dot(x, wg_ref[j])
        u = _dot(x, wu_ref[j])
        wj = wrow[:, j:j + 1]
        mid = jnp.where(wj != 0.0, (g * jax.nn.sigmoid(g)) * u * wj, 0.0)
        part = _dot(mid.astype(BF16), wd_ref[j])
        y = part if y is None else y + part
    _store_token_tiles(y_ref, y, ROW_BLOCK)

    @pl.when(i == pl.num_programs(0) - 1)
    def _():
        for extra in (1, 2):
            s_extra = (i + extra) % GATHER_SLOTS
            _wait_rows(ROW_BLOCK, xn_hbm, xbuf.at[s_extra], sem.at[s_extra])


def _experts(blk_g, row_tok, row_w, xn_tiles, wg, wu, wd):
    n_blk = row_tok.shape[0]
    assert n_blk >= GATHER_SLOTS
    wsel = lambda i, bg: (bg[i], 0, 0)

    def tok_spec(ahead):
        return pl.BlockSpec((1, 1, ROW_BLOCK), lambda i, be: (jnp.minimum(i + ahead, n_blk - 1), 0, 0),
                            memory_space=pltpu.SMEM)

    return pl.pallas_call(
        _expert_kernel,
        grid_spec=pltpu.PrefetchScalarGridSpec(
            num_scalar_prefetch=1,
            grid=(n_blk,),
            in_specs=[tok_spec(0), tok_spec(1), tok_spec(2),
                      pl.BlockSpec(memory_space=pl.ANY),
                      pl.BlockSpec((ROW_BLOCK, EXPERTS_PER_GROUP), lambda i, bg: (i, 0)),
                      pl.BlockSpec((EXPERTS_PER_GROUP, D_MODEL, D_FF), wsel),
                      pl.BlockSpec((EXPERTS_PER_GROUP, D_MODEL, D_FF), wsel),
                      pl.BlockSpec((EXPERTS_PER_GROUP, D_FF, D_MODEL), wsel)],
            out_specs=pl.BlockSpec((ROW_BLOCK * ROW_TILES, LANES), lambda i, bg: (i, 0)),
            scratch_shapes=[pltpu.VMEM((GATHER_SLOTS, ROW_BLOCK * ROW_TILES, LANES), F32),
                            pltpu.SemaphoreType.DMA((GATHER_SLOTS,))],
        ),
        out_shape=jax.ShapeDtypeStruct((n_blk * ROW_BLOCK * ROW_TILES, LANES), F32),
        compiler_params=_params(("arbitrary",)),
        name="moe_experts",
    )(blk_g, row_tok, row_tok, row_tok, xn_tiles, row_w, wg, wu, wd)


def _combine_kernel(dest_ref, dest_next_ref, h1_ref, p_ref, pp_ref, pn_ref, pg_ref, y_hbm, o_ref,
                    ybuf, sem, *, tm):
    i = pl.program_id(0)
    slot = i % 2

    @pl.when(i == 0)
    def _():
        _gather_rows(dest_ref, tm, y_hbm, ybuf.at[0], sem.at[0])

    _gather_rows_inline(dest_next_ref, tm, y_hbm, ybuf.at[1 - slot], sem.at[1 - slot])

    e = _dot(p_ref[...].astype(BF16), pp_ref[...])
    ms = jnp.mean(e * e, axis=-1, keepdims=True)
    ple = e * lax.rsqrt(ms + EPS) * pn_ref[...]

    _wait_rows(tm, y_hbm, ybuf.at[slot], sem.at[slot])
    h2 = h1_ref[...] + jnp.concatenate(_load_token_tiles(ybuf.at[slot], tm), axis=1)
    gate = jax.nn.sigmoid(_dot(h2.astype(BF16), pg_ref[...]))
    o_ref[...] = h2 + ple * gate

    @pl.when(i == pl.num_programs(0) - 1)
    def _():
        _wait_rows(tm, y_hbm, ybuf.at[1 - slot], sem.at[1 - slot])


def _combine(dest3, h1, p, pp_bf, pn, pg_bf, y_tiles, tm):
    t = h1.shape[0]
    n_tiles = t // tm
    row = lambda i: (i, 0)
    const = lambda i: (0, 0)
    n = tm
    return pl.pallas_call(
        functools.partial(_combine_kernel, tm=tm),
        grid=(n_tiles,),
        in_specs=[pl.BlockSpec((1, 1, n), lambda i: (i, 0, 0), memory_space=pltpu.SMEM),
                  pl.BlockSpec((1, 1, n), lambda i: (jnp.minimum(i + 1, n_tiles - 1), 0, 0), memory_space=pltpu.SMEM),
                  pl.BlockSpec((tm, D_MODEL), row),
                  pl.BlockSpec((tm, PLE_DIM), row),
                  pl.BlockSpec((PLE_DIM, D_MODEL), const),
                  pl.BlockSpec((1, D_MODEL), const),
                  pl.BlockSpec((D_MODEL, D_MODEL), const),
                  pl.BlockSpec(memory_space=pl.ANY)],
        out_specs=pl.BlockSpec((tm, D_MODEL), row),
        out_shape=jax.ShapeDtypeStruct((t, D_MODEL), F32),
        scratch_shapes=[pltpu.VMEM((2, n * ROW_TILES, LANES), F32), pltpu.SemaphoreType.DMA((2,))],
        compiler_params=_params(("arbitrary",)),
        name="moe_combine_ple",
    )(dest3, dest3, h1, p, pp_bf, pn, pg_bf, y_tiles)


def _routing_tables(route, n_rows):
    tok_g = route[:, EXPERTS_PER_GROUP].astype(jnp.int32)
    n_tok = tok_g.shape[0]
    groups = jnp.arange(N_GROUPS, dtype=jnp.int32)
    onehot = (tok_g[:, None] == groups[None, :]).astype(jnp.int32)
    csum = jnp.cumsum(onehot, axis=0)
    rank = jnp.sum(csum * onehot, axis=1) - 1
    counts = csum[-1]
    starts = jnp.cumsum(counts) - counts
    padded = (counts + ROW_BLOCK - 1) // ROW_BLOCK * ROW_BLOCK
    padded_end = jnp.cumsum(padded)
    padded_start = padded_end - padded
    dest = jnp.sum(onehot * padded_start[None, :], axis=1) + rank
    n_blk = n_rows // ROW_BLOCK
    blk_start = jnp.arange(n_blk, dtype=jnp.int32) * ROW_BLOCK
    blk_g = jnp.sum((blk_start[:, None] >= padded_end[None, :]).astype(jnp.int32), axis=1)
    blk_g = jnp.minimum(blk_g, N_GROUPS - 1).astype(jnp.int32)
    order = jnp.argsort(tok_g, stable=True).astype(jnp.int32)
    row = jnp.arange(n_rows, dtype=jnp.int32)
    row_g = jnp.repeat(blk_g, ROW_BLOCK)
    idx = row - padded_start[row_g]
    src = order[jnp.clip(starts[row_g] + idx, 0, n_tok - 1)]
    row_tok = jnp.where(idx < counts[row_g], src, 0)
    return dest.astype(jnp.int32), row_tok.astype(jnp.int32), blk_g


def _tile(n, pref):
    return pref if n % pref == 0 else n


def kernel(x, p, mix_norm, w_in, hgrn_lb, hgrn_out_norm, q_norm, k_norm, lam_q1, lam_k1, lam_q2, lam_k2,
           diff_subln, w_out, ffn_norm, w_router, w_gate, w_up, w_down, ple_proj, ple_norm, ple_gate):
    b, s, d = x.shape
    depth = w_in.shape[0]
    t = b * s
    tm = _tile(t, 512)
    ts = _tile(s, 1024)
    tq = _tile(s, 512)
    tmc = _tile(t, 256)
    n_rows = t + N_GROUPS * ROW_BLOCK

    lb_soft = jax.nn.softmax(hgrn_lb.astype(F32), axis=0)
    lower_bounds = jnp.cumsum(lb_soft, axis=0) - lb_soft[0]
    grp = np.kron(np.eye(HGRN_WIDTH // DIFF_HEAD_DIM), np.ones((DIFF_HEAD_DIM, DIFF_HEAD_DIM)))
    grp = jnp.asarray(grp, BF16)
    wr_hi, wr_lo = _split_bf16(w_router.astype(F32))
    wr_pack = jnp.pad(jnp.concatenate([wr_hi, wr_lo], axis=1), ((0, 0), (0, LANES - 2 * N_EXPERTS)))
    n_rep = HGRN_WIDTH // DIFF_HEAD_DIM

    h = x.reshape(t, d)
    for i in range(depth):
        lam_init = 0.8 - 0.6 * math.exp(-0.3 * i)
        gq = (jnp.tile(q_norm[i], n_rep) * DIFF_HEAD_DIM ** -0.5).reshape(1, HGRN_WIDTH)
        gk = jnp.tile(k_norm[i], n_rep).reshape(1, HGRN_WIDTH)
        score_bound = 1.02 * DIFF_HEAD_DIM ** 0.5 * jnp.max(jnp.abs(q_norm[i])) * jnp.max(jnp.abs(k_norm[i]))
        fast = (score_bound <= MAX_UNSHIFTED_SCORE).astype(jnp.int32).reshape(1)
        w_bf = w_in[i].astype(BF16)
        n_main = IN_COLS - HGRN_WIDTH
        zq, zf, zi, zg, dq, dk, dvt = _inproj(h, mix_norm[i].reshape(1, d), w_bf[:, :n_main], w_bf[:, n_main:].T,
                                              gq, gk, grp, tq)
        r3 = lambda a: a.reshape(b, s, HGRN_WIDTH)
        o_hgrn = _hgrn(r3(zq), r3(zf), r3(zi), r3(zg), lower_bounds[i].reshape(1, HGRN_WIDTH),
                       hgrn_out_norm[i].reshape(1, HEAD), ts)
        row64 = lambda a: a.reshape(1, DIFF_HEAD_DIM)
        o_diff = _attn(fast, r3(dq), r3(dk), dvt, row64(lam_q1[i]), row64(lam_k1[i]), row64(lam_q2[i]),
                       row64(lam_k2[i]), diff_subln[i].reshape(1, HEAD), lam_init, tq)
        h1, xn_tiles, route = _outproj(o_hgrn.reshape(t, HGRN_WIDTH), o_diff.reshape(t, HGRN_WIDTH), h,
                                       w_out[i].astype(BF16), ffn_norm[i].reshape(1, d), wr_pack, tm)
        dest, row_tok, blk_g = _routing_tables(route, n_rows)
        row_w = jnp.stack([route[:, j][row_tok] for j in range(EXPERTS_PER_GROUP)], axis=1)
        y_tiles = _experts(blk_g, row_tok.reshape(n_rows // ROW_BLOCK, 1, ROW_BLOCK), row_w, xn_tiles,
                           w_gate[i].astype(BF16), w_up[i].astype(BF16), w_down[i].astype(BF16))
        h = _combine(dest.reshape(t // tmc, 1, tmc), h1, p[i].reshape(t, PLE_DIM), ple_proj[i].astype(BF16),
                     ple_norm[i].reshape(1, d), ple_gate[i].astype(BF16), y_tiles, tmc)
    return h.reshape(b, s, d)
```

```python
import functools
import math

import numpy as np
import jax
import jax.numpy as jnp
from jax import lax
from jax.experimental import pallas as pl
from jax.experimental.pallas import tpu as pltpu

F32 = jnp.float32
BF16 = jnp.bfloat16

EPS = 1e-6
MIN_FORGET = 1e-6
MASK_VALUE = -1e30

D_MODEL = 1024
HGRN_WIDTH = 512
HEAD = 128
N_HEADS = 4
DIFF_HEAD_DIM = 64
PLE_DIM = 256
N_EXPERTS = 16
EXPERTS_PER_GROUP = 4
N_GROUPS = 4
TOP_K = 2
D_FF = 512
IN_COLS = 4 * HGRN_WIDTH + 3 * HGRN_WIDTH

LANES = 128
SUBLANES = 8
ROW_TILES = D_MODEL // LANES
CHUNK = 128
ROW_BLOCK = 512
GATHER_UNROLL = 8
GATHER_SLOTS = 3
VMEM_LIMIT = 48 * 1024 * 1024
MAX_UNSHIFTED_SCORE = 40.0

assert ROW_TILES == SUBLANES


def _dot(a, b):
    return jnp.dot(a, b, preferred_element_type=F32)


def _dot_nt(a, b):
    return lax.dot_general(a, b, (((1,), (1,)), ((), ())), preferred_element_type=F32)


def _split_bf16(x):
    hi = x.astype(BF16)
    lo = (x - hi.astype(F32)).astype(BF16)
    return hi, lo


def _params(sem, vmem=VMEM_LIMIT):
    return pltpu.CompilerParams(dimension_semantics=sem, vmem_limit_bytes=vmem)


def _store_token_tiles(ref, x, n):
    for a in range(ROW_TILES):
        ref[pl.ds(a, n, stride=ROW_TILES), :] = x[:, a * LANES:(a + 1) * LANES]


def _load_token_tiles(ref, n, base=0):
    return [ref[pl.ds(base + a, n, stride=ROW_TILES), :] for a in range(ROW_TILES)]


def _inproj_kernel(h_ref, g_ref, w_ref, wvt_ref, gq_ref, gk_ref, grp_ref,
                   zq_ref, zf_ref, zi_ref, zg_ref, dq_ref, dk_ref, dvt_ref):
    x = h_ref[...]
    ms = jnp.mean(x * x, axis=-1, keepdims=True)
    hn = (x * lax.rsqrt(ms + EPS) * g_ref[...]).astype(BF16)
    w = HGRN_WIDTH
    for j, o_ref in enumerate((zq_ref, zf_ref, zi_ref, zg_ref)):
        o_ref[...] = _dot(hn, w_ref[:, j * w:(j + 1) * w]).astype(BF16)
    for j, o_ref, gain_ref in ((4, dq_ref, gq_ref), (5, dk_ref, gk_ref)):
        z = _dot(hn, w_ref[:, j * w:(j + 1) * w])
        ss = _dot((z * z).astype(BF16), grp_ref[...])
        o_ref[...] = (z * lax.rsqrt(ss * (1.0 / DIFF_HEAD_DIM) + EPS) * gain_ref[...]).astype(BF16)
    dvt_ref[0] = _dot_nt(wvt_ref[...], hn).astype(BF16)


def _inproj(h, gain, w_main_bf, w_vt_bf, gq, gk, grp, tm):
    t = h.shape[0]
    w = HGRN_WIDTH
    row = lambda i: (i, 0)
    const = lambda i: (0, 0)
    out = jax.ShapeDtypeStruct((t, w), BF16)
    return pl.pallas_call(
        _inproj_kernel,
        grid=(t // tm,),
        in_specs=[
            pl.BlockSpec((tm, D_MODEL), row),
            pl.BlockSpec((1, D_MODEL), const),
            pl.BlockSpec((D_MODEL, IN_COLS - w), const),
            pl.BlockSpec((w, D_MODEL), const),
            pl.BlockSpec((1, w), const),
            pl.BlockSpec((1, w), const),
            pl.BlockSpec((w, w), const),
        ],
        out_specs=[pl.BlockSpec((tm, w), row)] * 6 + [pl.BlockSpec((1, w, tm), lambda i: (i, 0, 0))],
        out_shape=[out] * 6 + [jax.ShapeDtypeStruct((t // tm, w, tm), BF16)],
        compiler_params=_params(("arbitrary",)),
        name="inproj",
    )(h, gain, w_main_bf, w_vt_bf, gq, gk, grp)


_PAIR_HALVES = (8, 16, 32, 64)
_N_EXP = 3 + len(_PAIR_HALVES)


def _hgrn_constants():
    c = CHUNK
    t = np.arange(c)[:, None]
    j = np.arange(c)[None, :]
    mats = [(j <= t).astype(np.float32), (j > t).astype(np.float32)]
    mid = t - t % 8 + 3
    m0 = np.where((t > mid) & (j > mid) & (j <= t), 1.0, 0.0) - np.where((t < mid) & (j > t) & (j <= mid), 1.0, 0.0)
    mats.append(m0.astype(np.float32))
    lvl = np.full((c, c), -1, np.int32)
    tt, ss = np.broadcast_arrays(t, j)
    lvl[(tt // 8 == ss // 8) & (ss <= tt)] = 0
    for n, m in enumerate(_PAIR_HALVES):
        b = t - t % (2 * m) + m - 1
        mats.append(np.where(t > b, (j > b) & (j <= t), (j > t) & (j <= b)).astype(np.float32))
        sel = (tt // (2 * m) == ss // (2 * m)) & (tt % (2 * m) >= m) & (ss % (2 * m) < m)
        lvl[sel] = n + 1
    return np.concatenate(mats, axis=0), lvl


def _hgrn_kernel(zq_ref, zf_ref, zi_ref, zg_ref, lb_ref, og_ref, m_ref, lvl_ref, o_ref, state_ref, *, n_chunks):
    @pl.when(pl.program_id(1) == 0)
    def _():
        state_ref[...] = jnp.zeros_like(state_ref)

    lb = lb_ref[...]
    og = og_ref[...]
    lvl = lvl_ref[...]
    level_masks = [lvl == n for n in range(1 + len(_PAIR_HALVES))]
    c = CHUNK

    def chunk(ci, carry):
        sl = pl.ds(pl.multiple_of(ci * c, c), c)
        xq = zq_ref[0, sl, :].astype(F32)
        fl = zf_ref[0, sl, :].astype(F32)
        xg = zg_ref[0, sl, :].astype(F32)
        q_all = xq * jax.nn.sigmoid(xq)
        forget = lb + (1.0 - lb) * jax.nn.sigmoid(fl)
        logf = jnp.log(jnp.maximum(forget, MIN_FORGET))
        k_all = 1.0 - forget
        gate_all = xg * jax.nn.sigmoid(xg)
        e_all = _dot(m_ref[...], jnp.concatenate(_split_bf16(logf), axis=0))
        e0 = e_all[2 * c:3 * c]
        q_lv = [(q_all * jnp.exp(e0)).astype(BF16)]
        k_lv = [(k_all * jnp.exp(-e0)).astype(BF16)]
        for n in range(len(_PAIR_HALVES)):
            p = jnp.exp(e_all[(3 + n) * c:(4 + n) * c])
            q_lv.append((q_all * p).astype(BF16))
            k_lv.append((k_all * p).astype(BF16))
        p_cum = jnp.exp(e_all[0:c])
        q_in = (q_all * p_cum).astype(BF16)
        k_out = (k_all * jnp.exp(e_all[c:2 * c])).astype(BF16)
        heads = [slice(hd * HEAD, (hd + 1) * HEAD) for hd in range(N_HEADS)]
        sc = [[_dot_nt(q_lv[n][:, hs], k_lv[n][:, hs]) for n in range(len(q_lv))] for hs in heads]
        s_bf = []
        for per_level in sc:
            s = jnp.where(level_masks[0], per_level[0], 0.0)
            for n in range(1, len(per_level)):
                s = jnp.where(level_masks[n], per_level[n], s)
            s_bf.append(s.astype(BF16))
        v_bf = [zi_ref[0, sl, hs] for hs in heads]
        states = [state_ref[hd] for hd in range(N_HEADS)]
        outs = [_dot_nt(q_in[:, hs], states[hd].astype(BF16)) + _dot(s_bf[hd], v_bf[hd])
                for hd, hs in enumerate(heads)]
        for hd, hs in enumerate(heads):
            v_t = v_bf[hd].astype(F32).T.astype(BF16)
            state_ref[hd] = states[hd] * p_cum[c - 1:c, hs] + _dot(v_t, k_out[:, hs])
        for hd, hs in enumerate(heads):
            o = outs[hd]
            ms = jnp.mean(o * o, axis=-1, keepdims=True)
            y = o * lax.rsqrt(ms + EPS) * og * gate_all[:, hs]
            o_ref[0, sl, hs] = y.astype(BF16)
        return carry

    lax.fori_loop(0, n_chunks, chunk, 0, unroll=2)


def _hgrn(zq, zf, zi, zg, lb, og, ts):
    b, s, _ = zq.shape
    mats, lvl = _hgrn_constants()
    blk = pl.BlockSpec((1, ts, HGRN_WIDTH), lambda bi, si: (bi, si, 0))
    const = lambda bi, si: (0, 0)
    return pl.pallas_call(
        functools.partial(_hgrn_kernel, n_chunks=ts // CHUNK),
        grid=(b, s // ts),
        in_specs=[blk, blk, blk, blk,
                  pl.BlockSpec((1, HGRN_WIDTH), const),
                  pl.BlockSpec((1, HEAD), const),
                  pl.BlockSpec((_N_EXP * CHUNK, 2 * CHUNK), const),
                  pl.BlockSpec((CHUNK, CHUNK), const)],
        out_specs=blk,
        out_shape=jax.ShapeDtypeStruct((b, s, HGRN_WIDTH), BF16),
        scratch_shapes=[pltpu.VMEM((N_HEADS, HEAD, HEAD), F32)],
        compiler_params=_params(("arbitrary", "arbitrary")),
        name="hgrn2",
    )(zq, zf, zi, zg, lb, og, jnp.asarray(np.concatenate([mats, mats], axis=1), BF16), jnp.asarray(lvl))


def _attn_kernel(fast_ref, q_ref, k_ref, vt_ref, lq1_ref, lk1_ref, lq2_ref, lk2_ref, sub_ref, o_ref,
                 m_sc, l_sc, acc_sc, *, tq, lam_init):
    qi = pl.program_id(1)
    lane = lax.broadcasted_iota(jnp.int32, (tq, HEAD), 1)
    q_maps = []
    for hd in range(N_HEADS):
        q = q_ref[0, :, hd * HEAD:(hd + 1) * HEAD]
        zero = jnp.zeros_like(q)
        q_maps.append((jnp.where(lane < DIFF_HEAD_DIM, q, zero), jnp.where(lane >= DIFF_HEAD_DIM, q, zero)))

    def sublane_tile_sum(p):
        return jnp.sum(p.reshape(p.shape[0] // SUBLANES, SUBLANES, p.shape[1]), axis=0)

    def scores(q_map, kb, masked, first_key=0):
        s = _dot_nt(kb, q_map)
        if masked:
            key = lax.broadcasted_iota(jnp.int32, s.shape, 0) + first_key
            qry = lax.broadcasted_iota(jnp.int32, s.shape, 1)
            s = jnp.where(key <= qry, s, MASK_VALUE)
        return s

    def run(update):
        def step(ki, masked):
            k_all = k_ref[0, pl.ds(pl.multiple_of(ki * tq, tq), tq), :]
            vt_all = vt_ref[ki]
            for hd in range(N_HEADS):
                kb = k_all[:, hd * HEAD:(hd + 1) * HEAD]
                vt = vt_all[hd * HEAD:(hd + 1) * HEAD, :]
                for c in range(2):
                    update(hd, c, scores(q_maps[hd][c], kb, masked), vt)

        def body(ki, carry):
            step(ki, False)
            return carry
        lax.fori_loop(0, qi, body, 0)
        step(qi, True)

    @pl.when(fast_ref[0] == 1)
    def _():
        chains = [(hd, c) for hd in range(N_HEADS) for c in range(2)]
        heads = [slice(hd * HEAD, (hd + 1) * HEAD) for hd in range(N_HEADS)]
        half = tq // 2
        assert half % LANES == 0

        k_all = k_ref[0, pl.ds(pl.multiple_of(qi * tq, tq), tq), :]
        vt_all = vt_ref[qi]
        tops = [jnp.exp(scores(q_maps[hd][c], k_all[0:half, heads[hd]], True)) for hd, c in chains]
        bots = [jnp.exp(scores(q_maps[hd][c][half:], k_all[half:, heads[hd]], True)) for hd, c in chains]
        for (hd, c), p_top, p_bot in zip(chains, tops, bots):
            vt = vt_all[heads[hd], :]
            l_sc[hd, c] = sublane_tile_sum(p_top)
            l_sc[hd, c, :, half:] += sublane_tile_sum(p_bot)
            acc_sc[hd, c] = _dot(vt[:, 0:half], p_top.astype(BF16))
            acc_sc[hd, c, :, half:] += _dot(vt[:, half:], p_bot.astype(BF16))

        def body(ki, carry):
            k_all = k_ref[0, pl.ds(pl.multiple_of(ki * tq, tq), tq), :]
            vt_all = vt_ref[ki]
            ps = [jnp.exp(scores(q_maps[hd][c], k_all[:, heads[hd]], False)) for hd, c in chains]
            for (hd, c), p in zip(chains, ps):
                l_sc[hd, c] += sublane_tile_sum(p)
                acc_sc[hd, c] += _dot(vt_all[heads[hd], :], p.astype(BF16))
            return carry
        lax.fori_loop(0, qi, body, 0)

    @pl.when(fast_ref[0] != 1)
    def _():
        m_sc[...] = jnp.full_like(m_sc, MASK_VALUE)
        l_sc[...] = jnp.zeros_like(l_sc)
        acc_sc[...] = jnp.zeros_like(acc_sc)

        def update(hd, c, s, vt):
            m_old = m_sc[hd, c]
            m_new = jnp.maximum(m_old, jnp.max(s, axis=0, keepdims=True))
            alpha = jnp.exp(m_old - m_new)
            p = jnp.exp(s - m_new)
            l_sc[hd, c] = alpha * l_sc[hd, c] + sublane_tile_sum(p)
            acc_sc[hd, c] = alpha * acc_sc[hd, c] + _dot(vt, p.astype(BF16))
            m_sc[hd, c] = m_new
        run(update)

    lam = (jnp.exp(jnp.sum(lq1_ref[...] * lk1_ref[...], keepdims=True))
           - jnp.exp(jnp.sum(lq2_ref[...] * lk2_ref[...], keepdims=True)) + lam_init)
    for hd in range(N_HEADS):
        l0 = jnp.sum(l_sc[hd, 0], axis=0, keepdims=True)
        l1 = jnp.sum(l_sc[hd, 1], axis=0, keepdims=True)
        o_t = acc_sc[hd, 0] / l0 - lam * (acc_sc[hd, 1] / l1)
        ms = jnp.mean(o_t * o_t, axis=0, keepdims=True)
        o = (o_t * lax.rsqrt(ms + EPS)).T
        o_ref[0, :, hd * HEAD:(hd + 1) * HEAD] = (o * sub_ref[...] * (1.0 - lam_init)).astype(BF16)


def _attn(fast, q, k, vt, lq1, lk1, lq2, lk2, sub, lam_init, tq):
    b, s, w = q.shape
    n_kv = s // tq
    qblk = pl.BlockSpec((1, tq, w), lambda bi, qi, f: (bi, qi, 0))
    kblk = pl.BlockSpec((1, s, w), lambda bi, qi, f: (bi, 0, 0))
    vtblk = pl.BlockSpec((n_kv, w, tq), lambda bi, qi, f: (bi, 0, 0))
    const = lambda bi, qi, f: (0, 0)
    lamspec = pl.BlockSpec((1, DIFF_HEAD_DIM), const)
    return pl.pallas_call(
        functools.partial(_attn_kernel, tq=tq, lam_init=lam_init),
        grid_spec=pltpu.PrefetchScalarGridSpec(
            num_scalar_prefetch=1,
            grid=(b, s // tq),
            in_specs=[qblk, kblk, vtblk, lamspec, lamspec, lamspec, lamspec, pl.BlockSpec((1, HEAD), const)],
            out_specs=qblk,
            scratch_shapes=[pltpu.VMEM((N_HEADS, 2, 1, tq), F32), pltpu.VMEM((N_HEADS, 2, SUBLANES, tq), F32),
                            pltpu.VMEM((N_HEADS, 2, HEAD, tq), F32)],
        ),
        out_shape=jax.ShapeDtypeStruct((b, s, w), BF16),
        compiler_params=_params(("arbitrary", "arbitrary")),
        name="diffattn",
    )(fast, q, k, vt, lq1, lk1, lq2, lk2, sub)


def _outproj_kernel(oh_ref, od_ref, h_ref, wo_ref, fg_ref, wr_ref, h1_ref, xn_ref, route_ref, *, tm):
    n_sub = 2 if tm % (2 * LANES) == 0 else 1
    sub = tm // n_sub
    for si in range(n_sub):
        rs = slice(si * sub, (si + 1) * sub)
        _outproj_rows(oh_ref[rs, :], od_ref[rs, :], h_ref[rs, :], wo_ref, fg_ref, wr_ref,
                      h1_ref.at[rs], xn_ref.at[pl.ds(si * sub * ROW_TILES, sub * ROW_TILES)], route_ref.at[rs], sub)


def _outproj_rows(oh, od, h, wo_ref, fg_ref, wr_ref, h1_ref, xn_ref, route_ref, tm):
    hw = HGRN_WIDTH
    h1 = h + _dot(oh, wo_ref[0:hw, :]) + _dot(od, wo_ref[hw:2 * hw, :])
    h1_ref[...] = h1
    ms = jnp.mean(h1 * h1, axis=-1, keepdims=True)
    xn = h1 * lax.rsqrt(ms + EPS) * fg_ref[...]
    _store_token_tiles(xn_ref, xn, tm)
    x_hi, x_lo = _split_bf16(xn)
    a = _dot(x_hi, wr_ref[...])
    logits = a + pltpu.roll(a, LANES - N_EXPERTS, axis=1) + _dot(x_lo, wr_ref[...])
    lt = logits.T[0:N_EXPERTS, :]
    ex = jnp.exp(lt - jnp.max(lt, axis=0, keepdims=True))
    aff = ex / jnp.sum(ex, axis=0, keepdims=True)
    rows = [aff[e:e + 1, :] for e in range(N_EXPERTS)]

    def top2(a, first):
        m1, i1 = a[0], jnp.full_like(a[0], first)
        for j in range(1, len(a)):
            better = a[j] > m1
            m1 = jnp.where(better, a[j], m1)
            i1 = jnp.where(better, float(first + j), i1)
        m2, i2 = jnp.full_like(m1, -1.0), jnp.zeros_like(m1)
        for j in range(len(a)):
            cand = jnp.where(i1 == float(first + j), -1.0, a[j])
            better = cand > m2
            m2 = jnp.where(better, cand, m2)
            i2 = jnp.where(better, float(first + j), i2)
        return m1, i1, m2, i2

    best = None
    for g in range(N_GROUPS):
        first = g * EXPERTS_PER_GROUP
        m1, i1, m2, i2 = top2(rows[first:first + EXPERTS_PER_GROUP], first)
        cand = (m1, i1 - first, m2, i2 - first, jnp.full_like(m1, float(g)))
        score = m1 + m2
        if best is None:
            best, best_score = cand, score
        else:
            better = score > best_score
            best = tuple(jnp.where(better, c, b) for c, b in zip(cand, best))
            best_score = jnp.where(better, score, best_score)
    m1, j1, m2, j2, gsel = best
    denom = m1 + m2
    w1, w2 = m1 / denom, m2 / denom
    rid = lax.broadcasted_iota(jnp.int32, (SUBLANES, tm), 0).astype(F32)
    r8 = jnp.where(rid == j1, w1, jnp.where(rid == j2, w2, 0.0))
    r8 = jnp.where(rid == float(EXPERTS_PER_GROUP), gsel, r8)
    route_t = jnp.concatenate([r8, jnp.zeros((LANES - SUBLANES, tm), F32)], axis=0)
    route_ref[...] = route_t.T


def _outproj(oh, od, h, wo_bf, fg, wr_pack, tm):
    t = h.shape[0]
    row = lambda i: (i, 0)
    const = lambda i: (0, 0)
    return pl.pallas_call(
        functools.partial(_outproj_kernel, tm=tm),
        grid=(t // tm,),
        in_specs=[
            pl.BlockSpec((tm, HGRN_WIDTH), row),
            pl.BlockSpec((tm, HGRN_WIDTH), row),
            pl.BlockSpec((tm, D_MODEL), row),
            pl.BlockSpec((D_MODEL, D_MODEL), const),
            pl.BlockSpec((1, D_MODEL), const),
            pl.BlockSpec((D_MODEL, LANES), const),
        ],
        out_specs=[pl.BlockSpec((tm, D_MODEL), row), pl.BlockSpec((tm * ROW_TILES, LANES), row),
                   pl.BlockSpec((tm, LANES), row)],
        out_shape=[jax.ShapeDtypeStruct((t, D_MODEL), F32), jax.ShapeDtypeStruct((t * ROW_TILES, LANES), F32),
                   jax.ShapeDtypeStruct((t, LANES), F32)],
        compiler_params=_params(("arbitrary",)),
        name="outproj_router",
    )(oh, od, h, wo_bf, fg, wr_pack)


def _tile_copy(src, src_row, dst, dst_row, sem):
    return pltpu.make_async_copy(src.at[pl.ds(pl.multiple_of(src_row * ROW_TILES, ROW_TILES), ROW_TILES)],
                                 dst.at[pl.ds(pl.multiple_of(dst_row * ROW_TILES, ROW_TILES), ROW_TILES)], sem)


def _gather_rows(idx_ref, n, src_hbm, dst, sem):
    def issue(jo, carry):
        for u in range(GATHER_UNROLL):
            j = jo * GATHER_UNROLL + u
            _tile_copy(src_hbm, idx_ref[0, 0, j], dst, j, sem).start(priority=u % 2)
        return carry
    lax.fori_loop(0, n // GATHER_UNROLL, issue, 0)


def _gather_rows_inline(idx_ref, n, src_hbm, dst, sem):
    for j in range(n):
        _tile_copy(src_hbm, idx_ref[0, 0, j], dst, j, sem).start(priority=j % 2)


def _wait_rows(n, src_hbm, dst, sem):
    pltpu.make_async_copy(src_hbm.at[pl.ds(0, n * ROW_TILES)], dst, sem).wait()


def _expert_kernel(blk_g_ref, tok0_ref, tok1_ref, tok2_ref, xn_hbm, wrow_ref, wg_ref, wu_ref, wd_ref, y_ref,
                   xbuf, sem):
    del blk_g_ref
    i = pl.program_id(0)
    slot = i % GATHER_SLOTS
    ahead = (i + 2) % GATHER_SLOTS

    @pl.when(i == 0)
    def _():
        _gather_rows(tok0_ref, ROW_BLOCK, xn_hbm, xbuf.at[0], sem.at[0])
        _gather_rows(tok1_ref, ROW_BLOCK, xn_hbm, xbuf.at[1], sem.at[1])

    _wait_rows(ROW_BLOCK, xn_hbm, xbuf.at[slot], sem.at[slot])
    x = jnp.concatenate(_load_token_tiles(xbuf.at[slot], ROW_BLOCK), axis=1).astype(BF16)
    _gather_rows_inline(tok2_ref, ROW_BLOCK, xn_hbm, xbuf.at[ahead], sem.at[ahead])
    wrow = wrow_ref[...]
    y = None
    for j in range(EXPERTS_PER_GROUP):
        g = _dot(x, wg_ref[j])
        u = _dot(x, wu_ref[j])
        wj = wrow[:, j:j + 1]
        mid = jnp.where(wj != 0.0, (g * jax.nn.sigmoid(g)) * u * wj, 0.0)
        part = _dot(mid.astype(BF16), wd_ref[j])
        y = part if y is None else y + part
    _store_token_tiles(y_ref, y, ROW_BLOCK)

    @pl.when(i == pl.num_programs(0) - 1)
    def _():
        for extra in (1, 2):
            s_extra = (i + extra) % GATHER_SLOTS
            _wait_rows(ROW_BLOCK, xn_hbm, xbuf.at[s_extra], sem.at[s_extra])


def _experts(blk_g, row_tok, row_w, xn_tiles, wg, wu, wd):
    n_blk = row_tok.shape[0]
    assert n_blk >= GATHER_SLOTS
    wsel = lambda i, bg: (bg[i], 0, 0)

    def tok_spec(ahead):
        return pl.BlockSpec((1, 1, ROW_BLOCK), lambda i, be: (jnp.minimum(i + ahead, n_blk - 1), 0, 0),
                            memory_space=pltpu.SMEM)

    return pl.pallas_call(
        _expert_kernel,
        grid_spec=pltpu.PrefetchScalarGridSpec(
            num_scalar_prefetch=1,
            grid=(n_blk,),
            in_specs=[tok_spec(0), tok_spec(1), tok_spec(2),
                      pl.BlockSpec(memory_space=pl.ANY),
                      pl.BlockSpec((ROW_BLOCK, EXPERTS_PER_GROUP), lambda i, bg: (i, 0)),
                      pl.BlockSpec((EXPERTS_PER_GROUP, D_MODEL, D_FF), wsel),
                      pl.BlockSpec((EXPERTS_PER_GROUP, D_MODEL, D_FF), wsel),
                      pl.BlockSpec((EXPERTS_PER_GROUP, D_FF, D_MODEL), wsel)],
            out_specs=pl.BlockSpec((ROW_BLOCK * ROW_TILES, LANES), lambda i, bg: (i, 0)),
            scratch_shapes=[pltpu.VMEM((GATHER_SLOTS, ROW_BLOCK * ROW_TILES, LANES), F32),
                            pltpu.SemaphoreType.DMA((GATHER_SLOTS,))],
        ),
        out_shape=jax.ShapeDtypeStruct((n_blk * ROW_BLOCK * ROW_TILES, LANES), F32),
        compiler_params=_params(("arbitrary",)),
        name="moe_experts",
    )(blk_g, row_tok, row_tok, row_tok, xn_tiles, row_w, wg, wu, wd)


def _combine_kernel(dest_ref, dest_next_ref, h1_ref, p_ref, pp_ref, pn_ref, pg_ref, y_hbm, o_ref,
                    ybuf, sem, *, tm):
    i = pl.program_id(0)
    slot = i % 2

    @pl.when(i == 0)
    def _():
        _gather_rows(dest_ref, tm, y_hbm, ybuf.at[0], sem.at[0])

    _gather_rows_inline(dest_next_ref, tm, y_hbm, ybuf.at[1 - slot], sem.at[1 - slot])

    e = _dot(p_ref[...].astype(BF16), pp_ref[...])
    ms = jnp.mean(e * e, axis=-1, keepdims=True)
    ple = e * lax.rsqrt(ms + EPS) * pn_ref[...]

    _wait_rows(tm, y_hbm, ybuf.at[slot], sem.at[slot])
    h2 = h1_ref[...] + jnp.concatenate(_load_token_tiles(ybuf.at[slot], tm), axis=1)
    gate = jax.nn.sigmoid(_dot(h2.astype(BF16), pg_ref[...]))
    o_ref[...] = h2 + ple * gate

    @pl.when(i == pl.num_programs(0) - 1)
    def _():
        _wait_rows(tm, y_hbm, ybuf.at[1 - slot], sem.at[1 - slot])


def _combine(dest3, h1, p, pp_bf, pn, pg_bf, y_tiles, tm):
    t = h1.shape[0]
    n_tiles = t // tm
    row = lambda i: (i, 0)
    const = lambda i: (0, 0)
    n = tm
    return pl.pallas_call(
        functools.partial(_combine_kernel, tm=tm),
        grid=(n_tiles,),
        in_specs=[pl.BlockSpec((1, 1, n), lambda i: (i, 0, 0), memory_space=pltpu.SMEM),
                  pl.BlockSpec((1, 1, n), lambda i: (jnp.minimum(i + 1, n_tiles - 1), 0, 0), memory_space=pltpu.SMEM),
                  pl.BlockSpec((tm, D_MODEL), row),
                  pl.BlockSpec((tm, PLE_DIM), row),
                  pl.BlockSpec((PLE_DIM, D_MODEL), const),
                  pl.BlockSpec((1, D_MODEL), const),
                  pl.BlockSpec((D_MODEL, D_MODEL), const),
                  pl.BlockSpec(memory_space=pl.ANY)],
        out_specs=pl.BlockSpec((tm, D_MODEL), row),
        out_shape=jax.ShapeDtypeStruct((t, D_MODEL), F32),
        scratch_shapes=[pltpu.VMEM((2, n * ROW_TILES, LANES), F32), pltpu.SemaphoreType.DMA((2,))],
        compiler_params=_params(("arbitrary",)),
        name="moe_combine_ple",
    )(dest3, dest3, h1, p, pp_bf, pn, pg_bf, y_tiles)


def _routing_tables(route, n_rows):
    tok_g = route[:, EXPERTS_PER_GROUP].astype(jnp.int32)
    n_tok = tok_g.shape[0]
    groups = jnp.arange(N_GROUPS, dtype=jnp.int32)
    onehot = (tok_g[:, None] == groups[None, :]).astype(jnp.int32)
    csum = jnp.cumsum(onehot, axis=0)
    rank = jnp.sum(csum * onehot, axis=1) - 1
    counts = csum[-1]
    starts = jnp.cumsum(counts) - counts
    padded = (counts + ROW_BLOCK - 1) // ROW_BLOCK * ROW_BLOCK
    padded_end = jnp.cumsum(padded)
    padded_start = padded_end - padded
    dest = jnp.sum(onehot * padded_start[None, :], axis=1) + rank
    n_blk = n_rows // ROW_BLOCK
    blk_start = jnp.arange(n_blk, dtype=jnp.int32) * ROW_BLOCK
    blk_g = jnp.sum((blk_start[:, None] >= padded_end[None, :]).astype(jnp.int32), axis=1)
    blk_g = jnp.minimum(blk_g, N_GROUPS - 1).astype(jnp.int32)
    order = jnp.argsort(tok_g, stable=True).astype(jnp.int32)
    row = jnp.arange(n_rows, dtype=jnp.int32)
    row_g = jnp.repeat(blk_g, ROW_BLOCK)
    idx = row - padded_start[row_g]
    src = order[jnp.clip(starts[row_g] + idx, 0, n_tok - 1)]
    row_tok = jnp.where(idx < counts[row_g], src, 0)
    return dest.astype(jnp.int32), row_tok.astype(jnp.int32), blk_g


def _tile(n, pref):
    return pref if n % pref == 0 else n


def kernel(x, p, mix_norm, w_in, hgrn_lb, hgrn_out_norm, q_norm, k_norm, lam_q1, lam_k1, lam_q2, lam_k2,
           diff_subln, w_out, ffn_norm, w_router, w_gate, w_up, w_down, ple_proj, ple_norm, ple_gate):
    b, s, d = x.shape
    depth = w_in.shape[0]
    t = b * s
    tm = _tile(t, 512)
    ts = _tile(s, 1024)
    tq = _tile(s, 512)
    tmc = _tile(t, 512)
    n_rows = t + N_GROUPS * ROW_BLOCK

    lb_soft = jax.nn.softmax(hgrn_lb.astype(F32), axis=0)
    lower_bounds = jnp.cumsum(lb_soft, axis=0) - lb_soft[0]
    grp = np.kron(np.eye(HGRN_WIDTH // DIFF_HEAD_DIM), np.ones((DIFF_HEAD_DIM, DIFF_HEAD_DIM)))
    grp = jnp.asarray(grp, BF16)
    wr_hi, wr_lo = _split_bf16(w_router.astype(F32))
    wr_pack = jnp.pad(jnp.concatenate([wr_hi, wr_lo], axis=1), ((0, 0), (0, LANES - 2 * N_EXPERTS)))
    n_rep = HGRN_WIDTH // DIFF_HEAD_DIM

    h = x.reshape(t, d)
    for i in range(depth):
        lam_init = 0.8 - 0.6 * math.exp(-0.3 * i)
        gq = (jnp.tile(q_norm[i], n_rep) * DIFF_HEAD_DIM ** -0.5).reshape(1, HGRN_WIDTH)
        gk = jnp.tile(k_norm[i], n_rep).reshape(1, HGRN_WIDTH)
        score_bound = 1.02 * DIFF_HEAD_DIM ** 0.5 * jnp.max(jnp.abs(q_norm[i])) * jnp.max(jnp.abs(k_norm[i]))
        fast = (score_bound <= MAX_UNSHIFTED_SCORE).astype(jnp.int32).reshape(1)
        w_bf = w_in[i].astype(BF16)
        n_main = IN_COLS - HGRN_WIDTH
        zq, zf, zi, zg, dq, dk, dvt = _inproj(h, mix_norm[i].reshape(1, d), w_bf[:, :n_main], w_bf[:, n_main:].T,
                                              gq, gk, grp, tq)
        r3 = lambda a: a.reshape(b, s, HGRN_WIDTH)
        o_hgrn = _hgrn(r3(zq), r3(zf), r3(zi), r3(zg), lower_bounds[i].reshape(1, HGRN_WIDTH),
                       hgrn_out_norm[i].reshape(1, HEAD), ts)
        row64 = lambda a: a.reshape(1, DIFF_HEAD_DIM)
        o_diff = _attn(fast, r3(dq), r3(dk), dvt, row64(lam_q1[i]), row64(lam_k1[i]), row64(lam_q2[i]),
                       row64(lam_k2[i]), diff_subln[i].reshape(1, HEAD), lam_init, tq)
        h1, xn_tiles, route = _outproj(o_hgrn.reshape(t, HGRN_WIDTH), o_diff.reshape(t, HGRN_WIDTH), h,
                                       w_out[i].astype(BF16), ffn_norm[i].reshape(1, d), wr_pack, tm)
        dest, row_tok, blk_g = _routing_tables(route, n_rows)
        row_w = jnp.stack([route[:, j][row_tok] for j in range(EXPERTS_PER_GROUP)], axis=1)
        y_tiles = _experts(blk_g, row_tok.reshape(n_rows // ROW_BLOCK, 1, ROW_BLOCK), row_w, xn_tiles,
                           w_gate[i].astype(BF16), w_up[i].astype(BF16), w_down[i].astype(BF16))
        h = _combine(dest.reshape(t // tmc, 1, tmc), h1, p[i].reshape(t, PLE_DIM), ple_proj[i].astype(BF16),
                     ple_norm[i].reshape(1, d), ple_gate[i].astype(BF16), y_tiles, tmc)
    return h.reshape(b, s, d)
```

```python
import functools
import math

import numpy as np
import jax
import jax.numpy as jnp
from jax import lax
from jax.experimental import pallas as pl
from jax.experimental.pallas import tpu as pltpu

F32 = jnp.float32
BF16 = jnp.bfloat16

EPS = 1e-6
MIN_FORGET = 1e-6
MASK_VALUE = -1e30

D_MODEL = 1024
HGRN_WIDTH = 512
HEAD = 128
N_HEADS = 4
DIFF_HEAD_DIM = 64
PLE_DIM = 256
N_EXPERTS = 16
EXPERTS_PER_GROUP = 4
N_GROUPS = 4
TOP_K = 2
D_FF = 512
IN_COLS = 4 * HGRN_WIDTH + 3 * HGRN_WIDTH

LANES = 128
SUBLANES = 8
ROW_TILES = D_MODEL // LANES
CHUNK = 128
ROW_BLOCK = 512
GATHER_UNROLL = 8
GATHER_SLOTS = 3
VMEM_LIMIT = 48 * 1024 * 1024
MAX_UNSHIFTED_SCORE = 40.0

assert ROW_TILES == SUBLANES


def _dot(a, b):
    return jnp.dot(a, b, preferred_element_type=F32)


def _dot_nt(a, b):
    return lax.dot_general(a, b, (((1,), (1,)), ((), ())), preferred_element_type=F32)


def _split_bf16(x):
    hi = x.astype(BF16)
    lo = (x - hi.astype(F32)).astype(BF16)
    return hi, lo


def _params(sem, vmem=VMEM_LIMIT):
    return pltpu.CompilerParams(dimension_semantics=sem, vmem_limit_bytes=vmem)


def _store_token_tiles(ref, x, n):
    for a in range(ROW_TILES):
        ref[pl.ds(a, n, stride=ROW_TILES), :] = x[:, a * LANES:(a + 1) * LANES]


def _load_token_tiles(ref, n, base=0):
    return [ref[pl.ds(base + a, n, stride=ROW_TILES), :] for a in range(ROW_TILES)]


def _inproj_kernel(h_ref, g_ref, w_ref, wvt_ref, gq_ref, gk_ref, grp_ref,
                   zq_ref, zf_ref, zi_ref, zg_ref, dq_ref, dk_ref, dvt_ref):
    x = h_ref[...]
    ms = jnp.mean(x * x, axis=-1, keepdims=True)
    hn = (x * lax.rsqrt(ms + EPS) * g_ref[...]).astype(BF16)
    w = HGRN_WIDTH
    for j, o_ref in enumerate((zq_ref, zf_ref, zi_ref, zg_ref)):
        o_ref[...] = _dot(hn, w_ref[:, j * w:(j + 1) * w]).astype(BF16)
    for j, o_ref, gain_ref in ((4, dq_ref, gq_ref), (5, dk_ref, gk_ref)):
        z = _dot(hn, w_ref[:, j * w:(j + 1) * w])
        ss = _dot((z * z).astype(BF16), grp_ref[...])
        o_ref[...] = (z * lax.rsqrt(ss * (1.0 / DIFF_HEAD_DIM) + EPS) * gain_ref[...]).astype(BF16)
    dvt_ref[0] = _dot_nt(wvt_ref[...], hn).astype(BF16)


def _inproj(h, gain, w_main_bf, w_vt_bf, gq, gk, grp, tm):
    t = h.shape[0]
    w = HGRN_WIDTH
    row = lambda i: (i, 0)
    const = lambda i: (0, 0)
    out = jax.ShapeDtypeStruct((t, w), BF16)
    return pl.pallas_call(
        _inproj_kernel,
        grid=(t // tm,),
        in_specs=[
            pl.BlockSpec((tm, D_MODEL), row),
            pl.BlockSpec((1, D_MODEL), const),
            pl.BlockSpec((D_MODEL, IN_COLS - w), const),
            pl.BlockSpec((w, D_MODEL), const),
            pl.BlockSpec((1, w), const),
            pl.BlockSpec((1, w), const),
            pl.BlockSpec((w, w), const),
        ],
        out_specs=[pl.BlockSpec((tm, w), row)] * 6 + [pl.BlockSpec((1, w, tm), lambda i: (i, 0, 0))],
        out_shape=[out] * 6 + [jax.ShapeDtypeStruct((t // tm, w, tm), BF16)],
        compiler_params=_params(("arbitrary",)),
        name="inproj",
    )(h, gain, w_main_bf, w_vt_bf, gq, gk, grp)


_PAIR_HALVES = (8, 16, 32, 64)
_N_EXP = 3 + len(_PAIR_HALVES)


def _hgrn_constants():
    c = CHUNK
    t = np.arange(c)[:, None]
    j = np.arange(c)[None, :]
    mats = [(j <= t).astype(np.float32), (j > t).astype(np.float32)]
    mid = t - t % 8 + 3
    m0 = np.where((t > mid) & (j > mid) & (j <= t), 1.0, 0.0) - np.where((t < mid) & (j > t) & (j <= mid), 1.0, 0.0)
    mats.append(m0.astype(np.float32))
    lvl = np.full((c, c), -1, np.int32)
    tt, ss = np.broadcast_arrays(t, j)
    lvl[(tt // 8 == ss // 8) & (ss <= tt)] = 0
    for n, m in enumerate(_PAIR_HALVES):
        b = t - t % (2 * m) + m - 1
        mats.append(np.where(t > b, (j > b) & (j <= t), (j > t) & (j <= b)).astype(np.float32))
        sel = (tt // (2 * m) == ss // (2 * m)) & (tt % (2 * m) >= m) & (ss % (2 * m) < m)
        lvl[sel] = n + 1
    return np.concatenate(mats, axis=0), lvl


def _hgrn_kernel(zq_ref, zf_ref, zi_ref, zg_ref, lb_ref, og_ref, m_ref, lvl_ref, o_ref, state_ref, *, n_chunks):
    @pl.when(pl.program_id(1) == 0)
    def _():
        state_ref[...] = jnp.zeros_like(state_ref)

    lb = lb_ref[...]
    og = og_ref[...]
    lvl = lvl_ref[...]
    level_masks = [lvl == n for n in range(1 + len(_PAIR_HALVES))]
    c = CHUNK

    def chunk(ci, carry):
        sl = pl.ds(pl.multiple_of(ci * c, c), c)
        xq = zq_ref[0, sl, :].astype(F32)
        fl = zf_ref[0, sl, :].astype(F32)
        xg = zg_ref[0, sl, :].astype(F32)
        q_all = xq * jax.nn.sigmoid(xq)
        forget = lb + (1.0 - lb) * jax.nn.sigmoid(fl)
        logf = jnp.log(jnp.maximum(forget, MIN_FORGET))
        k_all = 1.0 - forget
        gate_all = xg * jax.nn.sigmoid(xg)
        e_all = _dot(m_ref[...], jnp.concatenate(_split_bf16(logf), axis=0))
        e0 = e_all[2 * c:3 * c]
        q_lv = [(q_all * jnp.exp(e0)).astype(BF16)]
        k_lv = [(k_all * jnp.exp(-e0)).astype(BF16)]
        for n in range(len(_PAIR_HALVES)):
            p = jnp.exp(e_all[(3 + n) * c:(4 + n) * c])
            q_lv.append((q_all * p).astype(BF16))
            k_lv.append((k_all * p).astype(BF16))
        p_cum = jnp.exp(e_all[0:c])
        q_in = (q_all * p_cum).astype(BF16)
        k_out = (k_all * jnp.exp(e_all[c:2 * c])).astype(BF16)
        heads = [slice(hd * HEAD, (hd + 1) * HEAD) for hd in range(N_HEADS)]
        sc = [[_dot_nt(q_lv[n][:, hs], k_lv[n][:, hs]) for n in range(len(q_lv))] for hs in heads]
        s_bf = []
        for per_level in sc:
            s = jnp.where(level_masks[0], per_level[0], 0.0)
            for n in range(1, len(per_level)):
                s = jnp.where(level_masks[n], per_level[n], s)
            s_bf.append(s.astype(BF16))
        v_bf = [zi_ref[0, sl, hs] for hs in heads]
        states = [state_ref[hd] for hd in range(N_HEADS)]
        outs = [_dot_nt(q_in[:, hs], states[hd].astype(BF16)) + _dot(s_bf[hd], v_bf[hd])
                for hd, hs in enumerate(heads)]
        for hd, hs in enumerate(heads):
            v_t = v_bf[hd].astype(F32).T.astype(BF16)
            state_ref[hd] = states[hd] * p_cum[c - 1:c, hs] + _dot(v_t, k_out[:, hs])
        for hd, hs in enumerate(heads):
            o = outs[hd]
            ms = jnp.mean(o * o, axis=-1, keepdims=True)
            y = o * lax.rsqrt(ms + EPS) * og * gate_all[:, hs]
            o_ref[0, sl, hs] = y.astype(BF16)
        return carry

    lax.fori_loop(0, n_chunks, chunk, 0, unroll=2)


def _hgrn(zq, zf, zi, zg, lb, og, ts):
    b, s, _ = zq.shape
    mats, lvl = _hgrn_constants()
    blk = pl.BlockSpec((1, ts, HGRN_WIDTH), lambda bi, si: (bi, si, 0))
    const = lambda bi, si: (0, 0)
    return pl.pallas_call(
        functools.partial(_hgrn_kernel, n_chunks=ts // CHUNK),
        grid=(b, s // ts),
        in_specs=[blk, blk, blk, blk,
                  pl.BlockSpec((1, HGRN_WIDTH), const),
                  pl.BlockSpec((1, HEAD), const),
                  pl.BlockSpec((_N_EXP * CHUNK, 2 * CHUNK), const),
                  pl.BlockSpec((CHUNK, CHUNK), const)],
        out_specs=blk,
        out_shape=jax.ShapeDtypeStruct((b, s, HGRN_WIDTH), BF16),
        scratch_shapes=[pltpu.VMEM((N_HEADS, HEAD, HEAD), F32)],
        compiler_params=_params(("arbitrary", "arbitrary")),
        name="hgrn2",
    )(zq, zf, zi, zg, lb, og, jnp.asarray(np.concatenate([mats, mats], axis=1), BF16), jnp.asarray(lvl))


def _attn_kernel(fast_ref, q_ref, k_ref, vt_ref, lq1_ref, lk1_ref, lq2_ref, lk2_ref, sub_ref, o_ref,
                 m_sc, l_sc, acc_sc, *, tq, lam_init):
    qi = pl.program_id(1)
    lane = lax.broadcasted_iota(jnp.int32, (tq, HEAD), 1)
    q_maps = []
    for hd in range(N_HEADS):
        q = q_ref[0, :, hd * HEAD:(hd + 1) * HEAD]
        zero = jnp.zeros_like(q)
        q_maps.append((jnp.where(lane < DIFF_HEAD_DIM, q, zero), jnp.where(lane >= DIFF_HEAD_DIM, q, zero)))

    def sublane_tile_sum(p):
        return jnp.sum(p.reshape(p.shape[0] // SUBLANES, SUBLANES, p.shape[1]), axis=0)

    def scores(q_map, kb, masked, first_key=0):
        s = _dot_nt(kb, q_map)
        if masked:
            key = lax.broadcasted_iota(jnp.int32, s.shape, 0) + first_key
            qry = lax.broadcasted_iota(jnp.int32, s.shape, 1)
            s = jnp.where(key <= qry, s, MASK_VALUE)
        return s

    def run(update):
        def step(ki, masked):
            k_all = k_ref[0, pl.ds(pl.multiple_of(ki * tq, tq), tq), :]
            vt_all = vt_ref[ki]
            for hd in range(N_HEADS):
                kb = k_all[:, hd * HEAD:(hd + 1) * HEAD]
                vt = vt_all[hd * HEAD:(hd + 1) * HEAD, :]
                for c in range(2):
                    update(hd, c, scores(q_maps[hd][c], kb, masked), vt)

        def body(ki, carry):
            step(ki, False)
            return carry
        lax.fori_loop(0, qi, body, 0)
        step(qi, True)

    @pl.when(fast_ref[0] == 1)
    def _():
        chains = [(hd, c) for hd in range(N_HEADS) for c in range(2)]
        heads = [slice(hd * HEAD, (hd + 1) * HEAD) for hd in range(N_HEADS)]
        half = tq // 2
        assert half % LANES == 0

        k_all = k_ref[0, pl.ds(pl.multiple_of(qi * tq, tq), tq), :]
        vt_all = vt_ref[qi]
        tops = [jnp.exp(scores(q_maps[hd][c], k_all[0:half, heads[hd]], True)) for hd, c in chains]
        bots = [jnp.exp(scores(q_maps[hd][c][half:], k_all[half:, heads[hd]], True)) for hd, c in chains]
        for (hd, c), p_top, p_bot in zip(chains, tops, bots):
            vt = vt_all[heads[hd], :]
            l_sc[hd, c] = sublane_tile_sum(p_top)
            l_sc[hd, c, :, half:] += sublane_tile_sum(p_bot)
            acc_sc[hd, c] = _dot(vt[:, 0:half], p_top.astype(BF16))
            acc_sc[hd, c, :, half:] += _dot(vt[:, half:], p_bot.astype(BF16))

        def body(ki, carry):
            k_all = k_ref[0, pl.ds(pl.multiple_of(ki * tq, tq), tq), :]
            vt_all = vt_ref[ki]
            ps = [jnp.exp(scores(q_maps[hd][c], k_all[:, heads[hd]], False)) for hd, c in chains]
            for (hd, c), p in zip(chains, ps):
                l_sc[hd, c] += sublane_tile_sum(p)
                acc_sc[hd, c] += _dot(vt_all[heads[hd], :], p.astype(BF16))
            return carry
        lax.fori_loop(0, qi, body, 0)

    @pl.when(fast_ref[0] != 1)
    def _():
        m_sc[...] = jnp.full_like(m_sc, MASK_VALUE)
        l_sc[...] = jnp.zeros_like(l_sc)
        acc_sc[...] = jnp.zeros_like(acc_sc)

        def update(hd, c, s, vt):
            m_old = m_sc[hd, c]
            m_new = jnp.maximum(m_old, jnp.max(s, axis=0, keepdims=True))
            alpha = jnp.exp(m_old - m_new)
            p = jnp.exp(s - m_new)
            l_sc[hd, c] = alpha * l_sc[hd, c] + sublane_tile_sum(p)
            acc_sc[hd, c] = alpha * acc_sc[hd, c] + _dot(vt, p.astype(BF16))
            m_sc[hd, c] = m_new
        run(update)

    lam = (jnp.exp(jnp.sum(lq1_ref[...] * lk1_ref[...], keepdims=True))
           - jnp.exp(jnp.sum(lq2_ref[...] * lk2_ref[...], keepdims=True)) + lam_init)
    for hd in range(N_HEADS):
        l0 = jnp.sum(l_sc[hd, 0], axis=0, keepdims=True)
        l1 = jnp.sum(l_sc[hd, 1], axis=0, keepdims=True)
        o_t = acc_sc[hd, 0] / l0 - lam * (acc_sc[hd, 1] / l1)
        ms = jnp.mean(o_t * o_t, axis=0, keepdims=True)
        o = (o_t * lax.rsqrt(ms + EPS)).T
        o_ref[0, :, hd * HEAD:(hd + 1) * HEAD] = (o * sub_ref[...] * (1.0 - lam_init)).astype(BF16)


def _attn(fast, q, k, vt, lq1, lk1, lq2, lk2, sub, lam_init, tq):
    b, s, w = q.shape
    n_kv = s // tq
    qblk = pl.BlockSpec((1, tq, w), lambda bi, qi, f: (bi, qi, 0))
    kblk = pl.BlockSpec((1, s, w), lambda bi, qi, f: (bi, 0, 0))
    vtblk = pl.BlockSpec((n_kv, w, tq), lambda bi, qi, f: (bi, 0, 0))
    const = lambda bi, qi, f: (0, 0)
    lamspec = pl.BlockSpec((1, DIFF_HEAD_DIM), const)
    return pl.pallas_call(
        functools.partial(_attn_kernel, tq=tq, lam_init=lam_init),
        grid_spec=pltpu.PrefetchScalarGridSpec(
            num_scalar_prefetch=1,
            grid=(b, s // tq),
            in_specs=[qblk, kblk, vtblk, lamspec, lamspec, lamspec, lamspec, pl.BlockSpec((1, HEAD), const)],
            out_specs=qblk,
            scratch_shapes=[pltpu.VMEM((N_HEADS, 2, 1, tq), F32), pltpu.VMEM((N_HEADS, 2, SUBLANES, tq), F32),
                            pltpu.VMEM((N_HEADS, 2, HEAD, tq), F32)],
        ),
        out_shape=jax.ShapeDtypeStruct((b, s, w), BF16),
        compiler_params=_params(("arbitrary", "arbitrary")),
        name="diffattn",
    )(fast, q, k, vt, lq1, lk1, lq2, lk2, sub)


def _outproj_kernel(oh_ref, od_ref, h_ref, wo_ref, fg_ref, wr_ref, h1_ref, xn_ref, route_ref, *, tm):
    n_sub = 2 if tm % (2 * LANES) == 0 else 1
    sub = tm // n_sub
    for si in range(n_sub):
        rs = slice(si * sub, (si + 1) * sub)
        _outproj_rows(oh_ref[rs, :], od_ref[rs, :], h_ref[rs, :], wo_ref, fg_ref, wr_ref,
                      h1_ref.at[rs], xn_ref.at[pl.ds(si * sub * ROW_TILES, sub * ROW_TILES)], route_ref.at[rs], sub)


def _outproj_rows(oh, od, h, wo_ref, fg_ref, wr_ref, h1_ref, xn_ref, route_ref, tm):
    hw = HGRN_WIDTH
    h1 = h + _dot(oh, wo_ref[0:hw, :]) + _dot(od, wo_ref[hw:2 * hw, :])
    h1_ref[...] = h1
    ms = jnp.mean(h1 * h1, axis=-1, keepdims=True)
    xn = h1 * lax.rsqrt(ms + EPS) * fg_ref[...]
    _store_token_tiles(xn_ref, xn, tm)
    x_hi, x_lo = _split_bf16(xn)
    a = _dot(x_hi, wr_ref[...])
    logits = a + pltpu.roll(a, LANES - N_EXPERTS, axis=1) + _dot(x_lo, wr_ref[...])
    lt = logits.T[0:N_EXPERTS, :]
    ex = jnp.exp(lt - jnp.max(lt, axis=0, keepdims=True))
    aff = ex / jnp.sum(ex, axis=0, keepdims=True)
    rows = [aff[e:e + 1, :] for e in range(N_EXPERTS)]

    def top2(a, first):
        m1, i1 = a[0], jnp.full_like(a[0], first)
        for j in range(1, len(a)):
            better = a[j] > m1
            m1 = jnp.where(better, a[j], m1)
            i1 = jnp.where(better, float(first + j), i1)
        m2, i2 = jnp.full_like(m1, -1.0), jnp.zeros_like(m1)
        for j in range(len(a)):
            cand = jnp.where(i1 == float(first + j), -1.0, a[j])
            better = cand > m2
            m2 = jnp.where(better, cand, m2)
            i2 = jnp.where(better, float(first + j), i2)
        return m1, i1, m2, i2

    best = None
    for g in range(N_GROUPS):
        first = g * EXPERTS_PER_GROUP
        m1, i1, m2, i2 = top2(rows[first:first + EXPERTS_PER_GROUP], first)
        cand = (m1, i1 - first, m2, i2 - first, jnp.full_like(m1, float(g)))
        score = m1 + m2
        if best is None:
            best, best_score = cand, score
        else:
            better = score > best_score
            best = tuple(jnp.where(better, c, b) for c, b in zip(cand, best))
            best_score = jnp.where(better, score, best_score)
    m1, j1, m2, j2, gsel = best
    denom = m1 + m2
    w1, w2 = m1 / denom, m2 / denom
    rid = lax.broadcasted_iota(jnp.int32, (SUBLANES, tm), 0).astype(F32)
    r8 = jnp.where(rid == j1, w1, jnp.where(rid == j2, w2, 0.0))
    r8 = jnp.where(rid == float(EXPERTS_PER_GROUP), gsel, r8)
    route_t = jnp.concatenate([r8, jnp.zeros((LANES - SUBLANES, tm), F32)], axis=0)
    route_ref[...] = route_t.T


def _outproj(oh, od, h, wo_bf, fg, wr_pack, tm):
    t = h.shape[0]
    row = lambda i: (i, 0)
    const = lambda i: (0, 0)
    return pl.pallas_call(
        functools.partial(_outproj_kernel, tm=tm),
        grid=(t // tm,),
        in_specs=[
            pl.BlockSpec((tm, HGRN_WIDTH), row),
            pl.BlockSpec((tm, HGRN_WIDTH), row),
            pl.BlockSpec((tm, D_MODEL), row),
            pl.BlockSpec((D_MODEL, D_MODEL), const),
            pl.BlockSpec((1, D_MODEL), const),
            pl.BlockSpec((D_MODEL, LANES), const),
        ],
        out_specs=[pl.BlockSpec((tm, D_MODEL), row), pl.BlockSpec((tm * ROW_TILES, LANES), row),
                   pl.BlockSpec((tm, LANES), row)],
        out_shape=[jax.ShapeDtypeStruct((t, D_MODEL), F32), jax.ShapeDtypeStruct((t * ROW_TILES, LANES), F32),
                   jax.ShapeDtypeStruct((t, LANES), F32)],
        compiler_params=_params(("arbitrary",)),
        name="outproj_router",
    )(oh, od, h, wo_bf, fg, wr_pack)


def _tile_copy(src, src_row, dst, dst_row, sem):
    return pltpu.make_async_copy(src.at[pl.ds(pl.multiple_of(src_row * ROW_TILES, ROW_TILES), ROW_TILES)],
                                 dst.at[pl.ds(pl.multiple_of(dst_row * ROW_TILES, ROW_TILES), ROW_TILES)], sem)


def _gather_rows(idx_ref, n, src_hbm, dst, sem):
    def issue(jo, carry):
        for u in range(GATHER_UNROLL):
            j = jo * GATHER_UNROLL + u
            _tile_copy(src_hbm, idx_ref[0, 0, j], dst, j, sem).start(priority=u % 2)
        return carry
    lax.fori_loop(0, n // GATHER_UNROLL, issue, 0)


def _gather_rows_inline(idx_ref, n, src_hbm, dst, sem):
    for j in range(n):
        _tile_copy(src_hbm, idx_ref[0, 0, j], dst, j, sem).start(priority=j % 2)


def _wait_rows(n, src_hbm, dst, sem):
    pltpu.make_async_copy(src_hbm.at[pl.ds(0, n * ROW_TILES)], dst, sem).wait()


def _expert_kernel(g_lo_ref, g_hi_ref, tok0_ref, tok1_ref, tok2_ref, xn_hbm, wrow_ref, wg_ref, wu_ref, wd_ref,
                   y_ref, xbuf, sem):
    i = pl.program_id(0)
    s = pl.program_id(1)
    slot = i % GATHER_SLOTS

    @pl.when((i == 0) & (s == 0))
    def _():
        _gather_rows(tok0_ref, ROW_BLOCK, xn_hbm, xbuf.at[0], sem.at[0])
        _gather_rows(tok1_ref, ROW_BLOCK, xn_hbm, xbuf.at[1], sem.at[1])

    @pl.when(s == 0)
    def _():
        ahead = (i + 2) % GATHER_SLOTS
        _wait_rows(ROW_BLOCK, xn_hbm, xbuf.at[slot], sem.at[slot])
        _gather_rows(tok2_ref, ROW_BLOCK, xn_hbm, xbuf.at[ahead], sem.at[ahead])

    @pl.when((s >= g_lo_ref[i]) & (s <= g_hi_ref[i]))
    def _():
        x = jnp.concatenate(_load_token_tiles(xbuf.at[slot], ROW_BLOCK), axis=1).astype(BF16)
        wrow = wrow_ref[...]
        in_group = wrow[:, EXPERTS_PER_GROUP:EXPERTS_PER_GROUP + 1] == s.astype(F32)
        y = None
        for j in range(EXPERTS_PER_GROUP):
            g = _dot(x, wg_ref[j])
            u = _dot(x, wu_ref[j])
            wj = jnp.where(in_group, wrow[:, j:j + 1], 0.0)
            mid = jnp.where(wj != 0.0, (g * jax.nn.sigmoid(g)) * u * wj, 0.0)
            part = _dot(mid.astype(BF16), wd_ref[j])
            y = part if y is None else y + part

        @pl.when(s == g_lo_ref[i])
        def _():
            _store_token_tiles(y_ref, y, ROW_BLOCK)

        @pl.when(s != g_lo_ref[i])
        def _():
            prev = jnp.concatenate(_load_token_tiles(y_ref, ROW_BLOCK), axis=1)
            _store_token_tiles(y_ref, prev + y, ROW_BLOCK)

    @pl.when((i == pl.num_programs(0) - 1) & (s == pl.num_programs(1) - 1))
    def _():
        for extra in (1, 2):
            s_extra = (i + extra) % GATHER_SLOTS
            _wait_rows(ROW_BLOCK, xn_hbm, xbuf.at[s_extra], sem.at[s_extra])


def _experts(g_lo, g_hi, row_tok, row_w, xn_tiles, wg, wu, wd):
    n_blk = row_tok.shape[0]
    assert n_blk >= GATHER_SLOTS
    wsel = lambda i, s, lo, hi: (jnp.clip(s, lo[i], hi[i]), 0, 0)
    blk = lambda i, s, lo, hi: (i, 0)

    def tok_spec(ahead):
        return pl.BlockSpec((1, 1, ROW_BLOCK), lambda i, s, lo, hi: (jnp.minimum(i + ahead, n_blk - 1), 0, 0),
                            memory_space=pltpu.SMEM)

    return pl.pallas_call(
        _expert_kernel,
        grid_spec=pltpu.PrefetchScalarGridSpec(
            num_scalar_prefetch=2,
            grid=(n_blk, N_GROUPS),
            in_specs=[tok_spec(0), tok_spec(1), tok_spec(2),
                      pl.BlockSpec(memory_space=pl.ANY),
                      pl.BlockSpec((ROW_BLOCK, SUBLANES), blk),
                      pl.BlockSpec((EXPERTS_PER_GROUP, D_MODEL, D_FF), wsel),
                      pl.BlockSpec((EXPERTS_PER_GROUP, D_MODEL, D_FF), wsel),
                      pl.BlockSpec((EXPERTS_PER_GROUP, D_FF, D_MODEL), wsel)],
            out_specs=pl.BlockSpec((ROW_BLOCK * ROW_TILES, LANES), blk),
            scratch_shapes=[pltpu.VMEM((GATHER_SLOTS, ROW_BLOCK * ROW_TILES, LANES), F32),
                            pltpu.SemaphoreType.DMA((GATHER_SLOTS,))],
        ),
        out_shape=jax.ShapeDtypeStruct((n_blk * ROW_BLOCK * ROW_TILES, LANES), F32),
        compiler_params=_params(("arbitrary", "arbitrary")),
        name="moe_experts",
    )(g_lo, g_hi, row_tok, row_tok, row_tok, xn_tiles, row_w, wg, wu, wd)


def _combine_kernel(dest_ref, dest_next_ref, h1_ref, p_ref, pp_ref, pn_ref, pg_ref, y_hbm, o_ref,
                    ybuf, sem, *, tm):
    i = pl.program_id(0)
    slot = i % 2

    @pl.when(i == 0)
    def _():
        _gather_rows(dest_ref, tm, y_hbm, ybuf.at[0], sem.at[0])

    _gather_rows_inline(dest_next_ref, tm, y_hbm, ybuf.at[1 - slot], sem.at[1 - slot])

    e = _dot(p_ref[...].astype(BF16), pp_ref[...])
    ms = jnp.mean(e * e, axis=-1, keepdims=True)
    ple = e * lax.rsqrt(ms + EPS) * pn_ref[...]

    _wait_rows(tm, y_hbm, ybuf.at[slot], sem.at[slot])
    h2 = h1_ref[...] + jnp.concatenate(_load_token_tiles(ybuf.at[slot], tm), axis=1)
    gate = jax.nn.sigmoid(_dot(h2.astype(BF16), pg_ref[...]))
    o_ref[...] = h2 + ple * gate

    @pl.when(i == pl.num_programs(0) - 1)
    def _():
        _wait_rows(tm, y_hbm, ybuf.at[1 - slot], sem.at[1 - slot])


def _combine(dest3, h1, p, pp_bf, pn, pg_bf, y_tiles, tm):
    t = h1.shape[0]
    n_tiles = t // tm
    row = lambda i: (i, 0)
    const = lambda i: (0, 0)
    n = tm
    return pl.pallas_call(
        functools.partial(_combine_kernel, tm=tm),
        grid=(n_tiles,),
        in_specs=[pl.BlockSpec((1, 1, n), lambda i: (i, 0, 0), memory_space=pltpu.SMEM),
                  pl.BlockSpec((1, 1, n), lambda i: (jnp.minimum(i + 1, n_tiles - 1), 0, 0), memory_space=pltpu.SMEM),
                  pl.BlockSpec((tm, D_MODEL), row),
                  pl.BlockSpec((tm, PLE_DIM), row),
                  pl.BlockSpec((PLE_DIM, D_MODEL), const),
                  pl.BlockSpec((1, D_MODEL), const),
                  pl.BlockSpec((D_MODEL, D_MODEL), const),
                  pl.BlockSpec(memory_space=pl.ANY)],
        out_specs=pl.BlockSpec((tm, D_MODEL), row),
        out_shape=jax.ShapeDtypeStruct((t, D_MODEL), F32),
        scratch_shapes=[pltpu.VMEM((2, n * ROW_TILES, LANES), F32), pltpu.SemaphoreType.DMA((2,))],
        compiler_params=_params(("arbitrary",)),
        name="moe_combine_ple",
    )(dest3, dest3, h1, p, pp_bf, pn, pg_bf, y_tiles)


def _routing_tables(route):
    n_tok = route.shape[0]
    assert n_tok % ROW_BLOCK == 0
    tok_g = route[:, EXPERTS_PER_GROUP].astype(jnp.int32)
    iota = jnp.arange(n_tok, dtype=jnp.int32)
    weights = [route[:, j] for j in range(EXPERTS_PER_GROUP)]
    sorted_g, row_tok, *row_weights = lax.sort((tok_g, iota, *weights), num_keys=1, is_stable=True)
    _, dest = lax.sort((row_tok, iota), num_keys=1)
    zeros = jnp.zeros_like(row_weights[0])
    row_w = jnp.stack(row_weights + [sorted_g.astype(F32)] + [zeros] * (SUBLANES - EXPERTS_PER_GROUP - 1), axis=1)
    g_lo = sorted_g[0::ROW_BLOCK]
    g_hi = sorted_g[ROW_BLOCK - 1::ROW_BLOCK]
    return dest, row_tok, row_w, g_lo, g_hi


def _tile(n, pref):
    return pref if n % pref == 0 else n


def kernel(x, p, mix_norm, w_in, hgrn_lb, hgrn_out_norm, q_norm, k_norm, lam_q1, lam_k1, lam_q2, lam_k2,
           diff_subln, w_out, ffn_norm, w_router, w_gate, w_up, w_down, ple_proj, ple_norm, ple_gate):
    b, s, d = x.shape
    depth = w_in.shape[0]
    t = b * s
    tm = _tile(t, 512)
    ts = _tile(s, 1024)
    tq = _tile(s, 512)
    tmc = _tile(t, 512)

    lb_soft = jax.nn.softmax(hgrn_lb.astype(F32), axis=0)
    lower_bounds = jnp.cumsum(lb_soft, axis=0) - lb_soft[0]
    grp = np.kron(np.eye(HGRN_WIDTH // DIFF_HEAD_DIM), np.ones((DIFF_HEAD_DIM, DIFF_HEAD_DIM)))
    grp = jnp.asarray(grp, BF16)
    wr_hi, wr_lo = _split_bf16(w_router.astype(F32))
    wr_pack = jnp.pad(jnp.concatenate([wr_hi, wr_lo], axis=1), ((0, 0), (0, LANES - 2 * N_EXPERTS)))
    n_rep = HGRN_WIDTH // DIFF_HEAD_DIM

    h = x.reshape(t, d)
    for i in range(depth):
        lam_init = 0.8 - 0.6 * math.exp(-0.3 * i)
        gq = (jnp.tile(q_norm[i], n_rep) * DIFF_HEAD_DIM ** -0.5).reshape(1, HGRN_WIDTH)
        gk = jnp.tile(k_norm[i], n_rep).reshape(1, HGRN_WIDTH)
        score_bound = 1.02 * DIFF_HEAD_DIM ** 0.5 * jnp.max(jnp.abs(q_norm[i])) * jnp.max(jnp.abs(k_norm[i]))
        fast = (score_bound <= MAX_UNSHIFTED_SCORE).astype(jnp.int32).reshape(1)
        w_bf = w_in[i].astype(BF16)
        n_main = IN_COLS - HGRN_WIDTH
        zq, zf, zi, zg, dq, dk, dvt = _inproj(h, mix_norm[i].reshape(1, d), w_bf[:, :n_main], w_bf[:, n_main:].T,
                                              gq, gk, grp, tq)
        r3 = lambda a: a.reshape(b, s, HGRN_WIDTH)
        o_hgrn = _hgrn(r3(zq), r3(zf), r3(zi), r3(zg), lower_bounds[i].reshape(1, HGRN_WIDTH),
                       hgrn_out_norm[i].reshape(1, HEAD), ts)
        row64 = lambda a: a.reshape(1, DIFF_HEAD_DIM)
        o_diff = _attn(fast, r3(dq), r3(dk), dvt, row64(lam_q1[i]), row64(lam_k1[i]), row64(lam_q2[i]),
                       row64(lam_k2[i]), diff_subln[i].reshape(1, HEAD), lam_init, tq)
        h1, xn_tiles, route = _outproj(o_hgrn.reshape(t, HGRN_WIDTH), o_diff.reshape(t, HGRN_WIDTH), h,
                                       w_out[i].astype(BF16), ffn_norm[i].reshape(1, d), wr_pack, tm)
        dest, row_tok, row_w, g_lo, g_hi = _routing_tables(route)
        y_tiles = _experts(g_lo, g_hi, row_tok.reshape(t // ROW_BLOCK, 1, ROW_BLOCK), row_w, xn_tiles,
                           w_gate[i].astype(BF16), w_up[i].astype(BF16), w_down[i].astype(BF16))
        h = _combine(dest.reshape(t // tmc, 1, tmc), h1, p[i].reshape(t, PLE_DIM), ple_proj[i].astype(BF16),
                     ple_norm[i].reshape(1, d), ple_gate[i].astype(BF16), y_tiles, tmc)
    return h.reshape(b, s, d)
```

```python
import functools
import math

import numpy as np
import jax
import jax.numpy as jnp
from jax import lax
from jax.experimental import pallas as pl
from jax.experimental.pallas import tpu as pltpu

F32 = jnp.float32
BF16 = jnp.bfloat16

EPS = 1e-6
MIN_FORGET = 1e-6
MASK_VALUE = -1e30

D_MODEL = 1024
HGRN_WIDTH = 512
HEAD = 128
N_HEADS = 4
DIFF_HEAD_DIM = 64
PLE_DIM = 256
N_EXPERTS = 16
EXPERTS_PER_GROUP = 4
N_GROUPS = 4
D_FF = 512
IN_COLS = 4 * HGRN_WIDTH + 3 * HGRN_WIDTH

LANES = 128
SUBLANES = 8
ROW_TILES = D_MODEL // LANES
CHUNK = 128
ROW_BLOCK = 512
GATHER_UNROLL = 8
GATHER_SLOTS = 3
V7X_VMEM_BYTES = 64 * 1024 * 1024
VMEM_LIMIT = V7X_VMEM_BYTES * 3 // 4
TOKEN_TILE = 512
SEQ_TILE = 1024
MAX_UNSHIFTED_SCORE = 40.0

assert ROW_TILES == SUBLANES


def _dot(a, b):
    return jnp.dot(a, b, preferred_element_type=F32)


def _dot_nt(a, b):
    return lax.dot_general(a, b, (((1,), (1,)), ((), ())), preferred_element_type=F32)


def _split_bf16(x):
    hi = x.astype(BF16)
    lo = (x - hi.astype(F32)).astype(BF16)
    return hi, lo


def _params(sem, vmem=VMEM_LIMIT):
    return pltpu.CompilerParams(dimension_semantics=sem, vmem_limit_bytes=vmem)


def _store_token_tiles(ref, x, n):
    for a in range(ROW_TILES):
        ref[pl.ds(a, n, stride=ROW_TILES), :] = x[:, a * LANES:(a + 1) * LANES]


def _load_token_tiles(ref, n):
    return [ref[pl.ds(a, n, stride=ROW_TILES), :] for a in range(ROW_TILES)]


def _inproj_kernel(h_ref, g_ref, w_ref, wvt_ref, gq_ref, gk_ref, grp_ref,
                   zq_ref, zf_ref, zi_ref, zg_ref, dq_ref, dk_ref, dvt_ref):
    x = h_ref[...]
    ms = jnp.mean(x * x, axis=-1, keepdims=True)
    hn = (x * lax.rsqrt(ms + EPS) * g_ref[...]).astype(BF16)
    w = HGRN_WIDTH
    for j, o_ref in enumerate((zq_ref, zf_ref, zi_ref, zg_ref)):
        o_ref[...] = _dot(hn, w_ref[:, j * w:(j + 1) * w]).astype(BF16)
    for j, o_ref, gain_ref in ((4, dq_ref, gq_ref), (5, dk_ref, gk_ref)):
        z = _dot(hn, w_ref[:, j * w:(j + 1) * w])
        ss = _dot((z * z).astype(BF16), grp_ref[...])
        o_ref[...] = (z * lax.rsqrt(ss * (1.0 / DIFF_HEAD_DIM) + EPS) * gain_ref[...]).astype(BF16)
    dvt_ref[0] = _dot_nt(wvt_ref[...], hn).astype(BF16)


def _inproj(h, gain, w_main_bf, w_vt_bf, gq, gk, grp, tm):
    t = h.shape[0]
    w = HGRN_WIDTH
    row = lambda i: (i, 0)
    const = lambda i: (0, 0)
    out = jax.ShapeDtypeStruct((t, w), BF16)
    return pl.pallas_call(
        _inproj_kernel,
        grid=(t // tm,),
        in_specs=[
            pl.BlockSpec((tm, D_MODEL), row),
            pl.BlockSpec((1, D_MODEL), const),
            pl.BlockSpec((D_MODEL, IN_COLS - w), const),
            pl.BlockSpec((w, D_MODEL), const),
            pl.BlockSpec((1, w), const),
            pl.BlockSpec((1, w), const),
            pl.BlockSpec((w, w), const),
        ],
        out_specs=[pl.BlockSpec((tm, w), row)] * 6 + [pl.BlockSpec((1, w, tm), lambda i: (i, 0, 0))],
        out_shape=[out] * 6 + [jax.ShapeDtypeStruct((t // tm, w, tm), BF16)],
        compiler_params=_params(("arbitrary",)),
        name="inproj",
    )(h, gain, w_main_bf, w_vt_bf, gq, gk, grp)


_PAIR_HALVES = (8, 16, 32, 64)
_N_EXP = 3 + len(_PAIR_HALVES)


def _hgrn_constants():
    c = CHUNK
    t = np.arange(c)[:, None]
    j = np.arange(c)[None, :]
    mats = [(j <= t).astype(np.float32), (j > t).astype(np.float32)]
    mid = t - t % 8 + 3
    m0 = np.where((t > mid) & (j > mid) & (j <= t), 1.0, 0.0) - np.where((t < mid) & (j > t) & (j <= mid), 1.0, 0.0)
    mats.append(m0.astype(np.float32))
    lvl = np.full((c, c), -1, np.int32)
    tt, ss = np.broadcast_arrays(t, j)
    lvl[(tt // 8 == ss // 8) & (ss <= tt)] = 0
    for n, m in enumerate(_PAIR_HALVES):
        b = t - t % (2 * m) + m - 1
        mats.append(np.where(t > b, (j > b) & (j <= t), (j > t) & (j <= b)).astype(np.float32))
        sel = (tt // (2 * m) == ss // (2 * m)) & (tt % (2 * m) >= m) & (ss % (2 * m) < m)
        lvl[sel] = n + 1
    return np.concatenate(mats, axis=0), lvl


def _hgrn_kernel(zq_ref, zf_ref, zi_ref, zg_ref, lb_ref, og_ref, m_ref, lvl_ref, o_ref, state_ref, *, n_chunks):
    @pl.when(pl.program_id(1) == 0)
    def _():
        state_ref[...] = jnp.zeros_like(state_ref)

    lb = lb_ref[...]
    og = og_ref[...]
    lvl = lvl_ref[...]
    level_masks = [lvl == n for n in range(1 + len(_PAIR_HALVES))]
    c = CHUNK

    def chunk(ci, carry):
        sl = pl.ds(pl.multiple_of(ci * c, c), c)
        xq = zq_ref[0, sl, :].astype(F32)
        fl = zf_ref[0, sl, :].astype(F32)
        xg = zg_ref[0, sl, :].astype(F32)
        q_all = xq * jax.nn.sigmoid(xq)
        forget = lb + (1.0 - lb) * jax.nn.sigmoid(fl)
        logf = jnp.log(jnp.maximum(forget, MIN_FORGET))
        k_all = 1.0 - forget
        gate_all = xg * jax.nn.sigmoid(xg)
        e_all = _dot(m_ref[...], jnp.concatenate(_split_bf16(logf), axis=0))
        e0 = e_all[2 * c:3 * c]
        q_lv = [(q_all * jnp.exp(e0)).astype(BF16)]
        k_lv = [(k_all * jnp.exp(-e0)).astype(BF16)]
        for n in range(len(_PAIR_HALVES)):
            p = jnp.exp(e_all[(3 + n) * c:(4 + n) * c])
            q_lv.append((q_all * p).astype(BF16))
            k_lv.append((k_all * p).astype(BF16))
        p_cum = jnp.exp(e_all[0:c])
        q_in = (q_all * p_cum).astype(BF16)
        k_out = (k_all * jnp.exp(e_all[c:2 * c])).astype(BF16)
        heads = [slice(hd * HEAD, (hd + 1) * HEAD) for hd in range(N_HEADS)]
        sc = [[_dot_nt(q_lv[n][:, hs], k_lv[n][:, hs]) for n in range(len(q_lv))] for hs in heads]
        s_bf = []
        for per_level in sc:
            s = jnp.where(level_masks[0], per_level[0], 0.0)
            for n in range(1, len(per_level)):
                s = jnp.where(level_masks[n], per_level[n], s)
            s_bf.append(s.astype(BF16))
        v_bf = [zi_ref[0, sl, hs] for hs in heads]
        states = [state_ref[hd] for hd in range(N_HEADS)]
        outs = [_dot_nt(q_in[:, hs], states[hd].astype(BF16)) + _dot(s_bf[hd], v_bf[hd])
                for hd, hs in enumerate(heads)]
        for hd, hs in enumerate(heads):
            v_t = v_bf[hd].astype(F32).T.astype(BF16)
            state_ref[hd] = states[hd] * p_cum[c - 1:c, hs] + _dot(v_t, k_out[:, hs])
        for hd, hs in enumerate(heads):
            o = outs[hd]
            ms = jnp.mean(o * o, axis=-1, keepdims=True)
            y = o * lax.rsqrt(ms + EPS) * og * gate_all[:, hs]
            o_ref[0, sl, hs] = y.astype(BF16)
        return carry

    lax.fori_loop(0, n_chunks, chunk, 0, unroll=2)


def _hgrn(zq, zf, zi, zg, lb, og, ts):
    b, s, _ = zq.shape
    mats, lvl = _hgrn_constants()
    blk = pl.BlockSpec((1, ts, HGRN_WIDTH), lambda bi, si: (bi, si, 0))
    const = lambda bi, si: (0, 0)
    return pl.pallas_call(
        functools.partial(_hgrn_kernel, n_chunks=ts // CHUNK),
        grid=(b, s // ts),
        in_specs=[blk, blk, blk, blk,
                  pl.BlockSpec((1, HGRN_WIDTH), const),
                  pl.BlockSpec((1, HEAD), const),
                  pl.BlockSpec((_N_EXP * CHUNK, 2 * CHUNK), const),
                  pl.BlockSpec((CHUNK, CHUNK), const)],
        out_specs=blk,
        out_shape=jax.ShapeDtypeStruct((b, s, HGRN_WIDTH), BF16),
        scratch_shapes=[pltpu.VMEM((N_HEADS, HEAD, HEAD), F32)],
        compiler_params=_params(("arbitrary", "arbitrary")),
        name="hgrn2",
    )(zq, zf, zi, zg, lb, og, jnp.asarray(np.concatenate([mats, mats], axis=1), BF16), jnp.asarray(lvl))


def _attn_kernel(fast_ref, q_ref, k_ref, vt_ref, lq1_ref, lk1_ref, lq2_ref, lk2_ref, sub_ref, o_ref,
                 m_sc, l_sc, acc_sc, *, tq, lam_init):
    qi = pl.program_id(1)
    lane = lax.broadcasted_iota(jnp.int32, (tq, HEAD), 1)
    q_maps = []
    for hd in range(N_HEADS):
        q = q_ref[0, :, hd * HEAD:(hd + 1) * HEAD]
        zero = jnp.zeros_like(q)
        q_maps.append((jnp.where(lane < DIFF_HEAD_DIM, q, zero), jnp.where(lane >= DIFF_HEAD_DIM, q, zero)))

    def sublane_tile_sum(p):
        return jnp.sum(p.reshape(p.shape[0] // SUBLANES, SUBLANES, p.shape[1]), axis=0)

    def scores(q_map, kb, masked, first_key=0):
        s = _dot_nt(kb, q_map)
        if masked:
            key = lax.broadcasted_iota(jnp.int32, s.shape, 0) + first_key
            qry = lax.broadcasted_iota(jnp.int32, s.shape, 1)
            s = jnp.where(key <= qry, s, MASK_VALUE)
        return s

    def run(update):
        def step(ki, masked):
            k_all = k_ref[0, pl.ds(pl.multiple_of(ki * tq, tq), tq), :]
            vt_all = vt_ref[ki]
            for hd in range(N_HEADS):
                kb = k_all[:, hd * HEAD:(hd + 1) * HEAD]
                vt = vt_all[hd * HEAD:(hd + 1) * HEAD, :]
                for c in range(2):
                    update(hd, c, scores(q_maps[hd][c], kb, masked), vt)

        def body(ki, carry):
            step(ki, False)
            return carry
        lax.fori_loop(0, qi, body, 0)
        step(qi, True)

    @pl.when(fast_ref[0] == 1)
    def _():
        chains = [(hd, c) for hd in range(N_HEADS) for c in range(2)]
        heads = [slice(hd * HEAD, (hd + 1) * HEAD) for hd in range(N_HEADS)]
        half = tq // 2
        assert half % LANES == 0

        k_all = k_ref[0, pl.ds(pl.multiple_of(qi * tq, tq), tq), :]
        vt_all = vt_ref[qi]
        tops = [jnp.exp(scores(q_maps[hd][c], k_all[0:half, heads[hd]], True)) for hd, c in chains]
        bots = [jnp.exp(scores(q_maps[hd][c][half:], k_all[half:, heads[hd]], True)) for hd, c in chains]
        for (hd, c), p_top, p_bot in zip(chains, tops, bots):
            vt = vt_all[heads[hd], :]
            l_sc[hd, c] = sublane_tile_sum(p_top)
            l_sc[hd, c, :, half:] += sublane_tile_sum(p_bot)
            acc_sc[hd, c] = _dot(vt[:, 0:half], p_top.astype(BF16))
            acc_sc[hd, c, :, half:] += _dot(vt[:, half:], p_bot.astype(BF16))

        def body(ki, carry):
            k_all = k_ref[0, pl.ds(pl.multiple_of(ki * tq, tq), tq), :]
            vt_all = vt_ref[ki]
            ps = [jnp.exp(scores(q_maps[hd][c], k_all[:, heads[hd]], False)) for hd, c in chains]
            for (hd, c), p in zip(chains, ps):
                l_sc[hd, c] += sublane_tile_sum(p)
                acc_sc[hd, c] += _dot(vt_all[heads[hd], :], p.astype(BF16))
            return carry
        lax.fori_loop(0, qi, body, 0)

    @pl.when(fast_ref[0] != 1)
    def _():
        m_sc[...] = jnp.full_like(m_sc, MASK_VALUE)
        l_sc[...] = jnp.zeros_like(l_sc)
        acc_sc[...] = jnp.zeros_like(acc_sc)

        def update(hd, c, s, vt):
            m_old = m_sc[hd, c]
            m_new = jnp.maximum(m_old, jnp.max(s, axis=0, keepdims=True))
            alpha = jnp.exp(m_old - m_new)
            p = jnp.exp(s - m_new)
            l_sc[hd, c] = alpha * l_sc[hd, c] + sublane_tile_sum(p)
            acc_sc[hd, c] = alpha * acc_sc[hd, c] + _dot(vt, p.astype(BF16))
            m_sc[hd, c] = m_new
        run(update)

    lam = (jnp.exp(jnp.sum(lq1_ref[...] * lk1_ref[...], keepdims=True))
           - jnp.exp(jnp.sum(lq2_ref[...] * lk2_ref[...], keepdims=True)) + lam_init)
    for hd in range(N_HEADS):
        l0 = jnp.sum(l_sc[hd, 0], axis=0, keepdims=True)
        l1 = jnp.sum(l_sc[hd, 1], axis=0, keepdims=True)
        o_t = acc_sc[hd, 0] / l0 - lam * (acc_sc[hd, 1] / l1)
        ms = jnp.mean(o_t * o_t, axis=0, keepdims=True)
        o = (o_t * lax.rsqrt(ms + EPS)).T
        o_ref[0, :, hd * HEAD:(hd + 1) * HEAD] = (o * sub_ref[...] * (1.0 - lam_init)).astype(BF16)


def _attn(fast, q, k, vt, lq1, lk1, lq2, lk2, sub, lam_init, tq):
    b, s, w = q.shape
    n_kv = s // tq
    qblk = pl.BlockSpec((1, tq, w), lambda bi, qi, f: (bi, qi, 0))
    kblk = pl.BlockSpec((1, s, w), lambda bi, qi, f: (bi, 0, 0))
    vtblk = pl.BlockSpec((n_kv, w, tq), lambda bi, qi, f: (bi, 0, 0))
    const = lambda bi, qi, f: (0, 0)
    lamspec = pl.BlockSpec((1, DIFF_HEAD_DIM), const)
    return pl.pallas_call(
        functools.partial(_attn_kernel, tq=tq, lam_init=lam_init),
        grid_spec=pltpu.PrefetchScalarGridSpec(
            num_scalar_prefetch=1,
            grid=(b, s // tq),
            in_specs=[qblk, kblk, vtblk, lamspec, lamspec, lamspec, lamspec, pl.BlockSpec((1, HEAD), const)],
            out_specs=qblk,
            scratch_shapes=[pltpu.VMEM((N_HEADS, 2, 1, tq), F32), pltpu.VMEM((N_HEADS, 2, SUBLANES, tq), F32),
                            pltpu.VMEM((N_HEADS, 2, HEAD, tq), F32)],
        ),
        out_shape=jax.ShapeDtypeStruct((b, s, w), BF16),
        compiler_params=_params(("arbitrary", "arbitrary")),
        name="diffattn",
    )(fast, q, k, vt, lq1, lk1, lq2, lk2, sub)


def _outproj_kernel(oh_ref, od_ref, h_ref, wo_ref, fg_ref, wr_ref, h1_ref, xn_ref, route_ref, *, tm):
    n_sub = 2 if tm % (2 * LANES) == 0 else 1
    sub = tm // n_sub
    for si in range(n_sub):
        rs = slice(si * sub, (si + 1) * sub)
        _outproj_rows(oh_ref[rs, :], od_ref[rs, :], h_ref[rs, :], wo_ref, fg_ref, wr_ref,
                      h1_ref.at[rs], xn_ref.at[pl.ds(si * sub * ROW_TILES, sub * ROW_TILES)], route_ref.at[:, rs], sub)


def _outproj_rows(oh, od, h, wo_ref, fg_ref, wr_ref, h1_ref, xn_ref, route_ref, tm):
    hw = HGRN_WIDTH
    h1 = h + _dot(oh, wo_ref[0:hw, :]) + _dot(od, wo_ref[hw:2 * hw, :])
    h1_ref[...] = h1
    ms = jnp.mean(h1 * h1, axis=-1, keepdims=True)
    xn = h1 * lax.rsqrt(ms + EPS) * fg_ref[...]
    _store_token_tiles(xn_ref, xn, tm)
    x_hi, x_lo = _split_bf16(xn)
    a = _dot(x_hi, wr_ref[...])
    logits = a + pltpu.roll(a, LANES - N_EXPERTS, axis=1) + _dot(x_lo, wr_ref[...])
    lt = logits.T[0:N_EXPERTS, :]
    ex = jnp.exp(lt - jnp.max(lt, axis=0, keepdims=True))
    aff = ex / jnp.sum(ex, axis=0, keepdims=True)
    rows = [aff[e:e + 1, :] for e in range(N_EXPERTS)]

    def top2(a, first):
        m1, i1 = a[0], jnp.full_like(a[0], first)
        for j in range(1, len(a)):
            better = a[j] > m1
            m1 = jnp.where(better, a[j], m1)
            i1 = jnp.where(better, float(first + j), i1)
        m2, i2 = jnp.full_like(m1, -1.0), jnp.zeros_like(m1)
        for j in range(len(a)):
            cand = jnp.where(i1 == float(first + j), -1.0, a[j])
            better = cand > m2
            m2 = jnp.where(better, cand, m2)
            i2 = jnp.where(better, float(first + j), i2)
        return m1, i1, m2, i2

    best = None
    for g in range(N_GROUPS):
        first = g * EXPERTS_PER_GROUP
        m1, i1, m2, i2 = top2(rows[first:first + EXPERTS_PER_GROUP], first)
        cand = (m1, i1 - first, m2, i2 - first, jnp.full_like(m1, float(g)))
        score = m1 + m2
        if best is None:
            best, best_score = cand, score
        else:
            better = score > best_score
            best = tuple(jnp.where(better, c, b) for c, b in zip(cand, best))
            best_score = jnp.where(better, score, best_score)
    m1, j1, m2, j2, gsel = best
    denom = m1 + m2
    w1, w2 = m1 / denom, m2 / denom
    rid = lax.broadcasted_iota(jnp.int32, (SUBLANES, tm), 0).astype(F32)
    r8 = jnp.where(rid == j1, w1, jnp.where(rid == j2, w2, 0.0))
    route_ref[...] = jnp.where(rid == float(EXPERTS_PER_GROUP), gsel, r8)


def _outproj(oh, od, h, wo_bf, fg, wr_pack, tm):
    t = h.shape[0]
    row = lambda i: (i, 0)
    const = lambda i: (0, 0)
    return pl.pallas_call(
        functools.partial(_outproj_kernel, tm=tm),
        grid=(t // tm,),
        in_specs=[
            pl.BlockSpec((tm, HGRN_WIDTH), row),
            pl.BlockSpec((tm, HGRN_WIDTH), row),
            pl.BlockSpec((tm, D_MODEL), row),
            pl.BlockSpec((D_MODEL, D_MODEL), const),
            pl.BlockSpec((1, D_MODEL), const),
            pl.BlockSpec((D_MODEL, LANES), const),
        ],
        out_specs=[pl.BlockSpec((tm, D_MODEL), row), pl.BlockSpec((tm * ROW_TILES, LANES), row),
                   pl.BlockSpec((SUBLANES, tm), lambda i: (0, i))],
        out_shape=[jax.ShapeDtypeStruct((t, D_MODEL), F32), jax.ShapeDtypeStruct((t * ROW_TILES, LANES), F32),
                   jax.ShapeDtypeStruct((SUBLANES, t), F32)],
        compiler_params=_params(("arbitrary",)),
        name="outproj_router",
    )(oh, od, h, wo_bf, fg, wr_pack)


def _tile_copy(src, src_row, dst, dst_row, sem):
    return pltpu.make_async_copy(src.at[pl.ds(pl.multiple_of(src_row * ROW_TILES, ROW_TILES), ROW_TILES)],
                                 dst.at[pl.ds(pl.multiple_of(dst_row * ROW_TILES, ROW_TILES), ROW_TILES)], sem)


def _gather_rows(idx_ref, n, src_hbm, dst, sem):
    def issue(jo, carry):
        for u in range(GATHER_UNROLL):
            j = jo * GATHER_UNROLL + u
            _tile_copy(src_hbm, idx_ref[0, 0, j], dst, j, sem).start()
        return carry
    lax.fori_loop(0, n // GATHER_UNROLL, issue, 0)


def _gather_rows_inline(idx_ref, n, src_hbm, dst, sem):
    for j in range(n):
        _tile_copy(src_hbm, idx_ref[0, 0, j], dst, j, sem).start()


def _wait_rows(n, src_hbm, dst, sem):
    pltpu.make_async_copy(src_hbm.at[pl.ds(0, n * ROW_TILES)], dst, sem).wait()


def _expert_kernel(item_blk_ref, item_g_ref, item_kind_ref, tok0_ref, tok1_ref, tok2_ref, xn_hbm, wrow_ref,
                   wg_ref, wu_ref, wd_ref, y_ref, xbuf, sem, *, n_blk):
    k = pl.program_id(0)
    i = item_blk_ref[k]
    group = item_g_ref[k]
    kind = item_kind_ref[k]
    slot = i % GATHER_SLOTS

    @pl.when(k == 0)
    def _():
        _gather_rows(tok0_ref, ROW_BLOCK, xn_hbm, xbuf.at[0], sem.at[0])
        _gather_rows(tok1_ref, ROW_BLOCK, xn_hbm, xbuf.at[1], sem.at[1])

    def group_rows(x):
        wrow = wrow_ref[...]
        in_group = wrow[:, EXPERTS_PER_GROUP:EXPERTS_PER_GROUP + 1] == group.astype(F32)
        y = None
        for j in range(EXPERTS_PER_GROUP):
            g = _dot(x, wg_ref[j])
            u = _dot(x, wu_ref[j])
            wj = jnp.where(in_group, wrow[:, j:j + 1], 0.0)
            mid = jnp.where(wj != 0.0, (g * jax.nn.sigmoid(g)) * u * wj, 0.0)
            part = _dot(mid.astype(BF16), wd_ref[j])
            y = part if y is None else y + part
        return y

    @pl.when(kind == 1)
    def _():
        ahead = (i + 2) % GATHER_SLOTS
        _wait_rows(ROW_BLOCK, xn_hbm, xbuf.at[slot], sem.at[slot])
        x = jnp.concatenate(_load_token_tiles(xbuf.at[slot], ROW_BLOCK), axis=1).astype(BF16)
        _gather_rows_inline(tok2_ref, ROW_BLOCK, xn_hbm, xbuf.at[ahead], sem.at[ahead])
        _store_token_tiles(y_ref, group_rows(x), ROW_BLOCK)

    @pl.when(kind == 2)
    def _():
        x = jnp.concatenate(_load_token_tiles(xbuf.at[slot], ROW_BLOCK), axis=1).astype(BF16)
        prev = jnp.concatenate(_load_token_tiles(y_ref, ROW_BLOCK), axis=1)
        _store_token_tiles(y_ref, prev + group_rows(x), ROW_BLOCK)

    @pl.when(k == pl.num_programs(0) - 1)
    def _():
        for extra in (0, 1):
            s_extra = (n_blk + extra) % GATHER_SLOTS
            _wait_rows(ROW_BLOCK, xn_hbm, xbuf.at[s_extra], sem.at[s_extra])


def _experts(items, row_tok, row_w, xn_tiles, wg, wu, wd, layer):
    n_blk = row_tok.shape[0]
    assert n_blk >= GATHER_SLOTS
    wsel = lambda k, ib, ig, ik: (layer * N_GROUPS + ig[k], 0, 0)
    blk = lambda k, ib, ig, ik: (ib[k], 0)

    def tok_spec(ahead):
        return pl.BlockSpec((1, 1, ROW_BLOCK), lambda k, ib, ig, ik: (jnp.minimum(ib[k] + ahead, n_blk - 1), 0, 0),
                            memory_space=pltpu.SMEM)

    return pl.pallas_call(
        functools.partial(_expert_kernel, n_blk=n_blk),
        grid_spec=pltpu.PrefetchScalarGridSpec(
            num_scalar_prefetch=3,
            grid=(items[0].shape[0],),
            in_specs=[tok_spec(0), tok_spec(1), tok_spec(2),
                      pl.BlockSpec(memory_space=pl.ANY),
                      pl.BlockSpec((ROW_BLOCK, SUBLANES), blk),
                      pl.BlockSpec((EXPERTS_PER_GROUP, D_MODEL, D_FF), wsel),
                      pl.BlockSpec((EXPERTS_PER_GROUP, D_MODEL, D_FF), wsel),
                      pl.BlockSpec((EXPERTS_PER_GROUP, D_FF, D_MODEL), wsel)],
            out_specs=pl.BlockSpec((ROW_BLOCK * ROW_TILES, LANES), blk),
            scratch_shapes=[pltpu.VMEM((GATHER_SLOTS, ROW_BLOCK * ROW_TILES, LANES), F32),
                            pltpu.SemaphoreType.DMA((GATHER_SLOTS,))],
        ),
        out_shape=jax.ShapeDtypeStruct((n_blk * ROW_BLOCK * ROW_TILES, LANES), F32),
        compiler_params=_params(("arbitrary",)),
        name="moe_experts",
    )(*items, row_tok, row_tok, row_tok, xn_tiles, row_w, wg, wu, wd)


def _combine_kernel(dest_ref, dest_next_ref, h1_ref, p_ref, pp_ref, pn_ref, pg_ref, y_hbm, o_ref,
                    ybuf, sem, *, tm):
    i = pl.program_id(0)
    slot = i % 2

    @pl.when(i == 0)
    def _():
        _gather_rows(dest_ref, tm, y_hbm, ybuf.at[0], sem.at[0])

    _gather_rows_inline(dest_next_ref, tm, y_hbm, ybuf.at[1 - slot], sem.at[1 - slot])

    e = _dot(p_ref[0].astype(BF16), pp_ref[...])
    ms = jnp.mean(e * e, axis=-1, keepdims=True)
    ple = e * lax.rsqrt(ms + EPS) * pn_ref[...]

    _wait_rows(tm, y_hbm, ybuf.at[slot], sem.at[slot])
    h2 = h1_ref[...] + jnp.concatenate(_load_token_tiles(ybuf.at[slot], tm), axis=1)
    gate = jax.nn.sigmoid(_dot(h2.astype(BF16), pg_ref[...]))
    o_ref[...] = h2 + ple * gate

    @pl.when(i == pl.num_programs(0) - 1)
    def _():
        _wait_rows(tm, y_hbm, ybuf.at[1 - slot], sem.at[1 - slot])


def _combine(dest3, h1, p, layer, pp_bf, pn, pg_bf, y_tiles, tm):
    t = h1.shape[0]
    n_tiles = t // tm
    row = lambda i: (i, 0)
    const = lambda i: (0, 0)
    n = tm
    return pl.pallas_call(
        functools.partial(_combine_kernel, tm=tm),
        grid=(n_tiles,),
        in_specs=[pl.BlockSpec((1, 1, n), lambda i: (i, 0, 0), memory_space=pltpu.SMEM),
                  pl.BlockSpec((1, 1, n), lambda i: (jnp.minimum(i + 1, n_tiles - 1), 0, 0), memory_space=pltpu.SMEM),
                  pl.BlockSpec((tm, D_MODEL), row),
                  pl.BlockSpec((1, tm, PLE_DIM), lambda i: (layer, i, 0)),
                  pl.BlockSpec((PLE_DIM, D_MODEL), const),
                  pl.BlockSpec((1, D_MODEL), const),
                  pl.BlockSpec((D_MODEL, D_MODEL), const),
                  pl.BlockSpec(memory_space=pl.ANY)],
        out_specs=pl.BlockSpec((tm, D_MODEL), row),
        out_shape=jax.ShapeDtypeStruct((t, D_MODEL), F32),
        scratch_shapes=[pltpu.VMEM((2, n * ROW_TILES, LANES), F32), pltpu.SemaphoreType.DMA((2,))],
        compiler_params=_params(("arbitrary",)),
        name="moe_combine_ple",
    )(dest3, dest3, h1, p, pp_bf, pn, pg_bf, y_tiles)


def _routing_tables(route):
    n_tok = route.shape[1]
    assert n_tok % ROW_BLOCK == 0
    tok_g = route[EXPERTS_PER_GROUP].astype(jnp.int32)
    iota = jnp.arange(n_tok, dtype=jnp.int32)
    weights = [route[j] for j in range(EXPERTS_PER_GROUP)]
    sorted_g, row_tok, *row_weights = lax.sort((tok_g, iota, *weights), num_keys=1, is_stable=True)
    _, dest = lax.sort((row_tok, iota), num_keys=1)
    zeros = jnp.zeros_like(row_weights[0])
    row_w = jnp.stack(row_weights + [sorted_g.astype(F32)] + [zeros] * (SUBLANES - EXPERTS_PER_GROUP - 1), axis=1)
    g_lo = sorted_g[0::ROW_BLOCK]
    g_hi = sorted_g[ROW_BLOCK - 1::ROW_BLOCK]
    n_blk = n_tok // ROW_BLOCK
    per_blk = g_hi - g_lo + 1
    first_item = jnp.cumsum(per_blk) - per_blk
    k = jnp.arange(n_blk + N_GROUPS - 1, dtype=jnp.int32)
    item_blk = (jnp.sum((first_item[None, :] <= k[:, None]).astype(jnp.int32), axis=1) - 1).astype(jnp.int32)
    nth = k - first_item[item_blk]
    valid = nth < per_blk[item_blk]
    item_g = jnp.minimum(g_lo[item_blk] + nth, g_hi[item_blk]).astype(jnp.int32)
    item_kind = jnp.where(valid, jnp.where(nth == 0, 1, 2), 0).astype(jnp.int32)
    return dest, row_tok, row_w, (item_blk, item_g, item_kind)


def _tile(n, pref):
    return pref if n % pref == 0 else n


def kernel(x, p, mix_norm, w_in, hgrn_lb, hgrn_out_norm, q_norm, k_norm, lam_q1, lam_k1, lam_q2, lam_k2,
           diff_subln, w_out, ffn_norm, w_router, w_gate, w_up, w_down, ple_proj, ple_norm, ple_gate):
    b, s, d = x.shape
    depth = w_in.shape[0]
    t = b * s
    tm = _tile(t, TOKEN_TILE)
    ts = _tile(s, SEQ_TILE)
    tq = _tile(s, TOKEN_TILE)
    tmc = _tile(t, TOKEN_TILE)

    lb_soft = jax.nn.softmax(hgrn_lb.astype(F32), axis=0)
    lower_bounds = jnp.cumsum(lb_soft, axis=0) - lb_soft[0]
    grp = np.kron(np.eye(HGRN_WIDTH // DIFF_HEAD_DIM), np.ones((DIFF_HEAD_DIM, DIFF_HEAD_DIM)))
    grp = jnp.asarray(grp, BF16)
    wr_hi, wr_lo = _split_bf16(w_router.astype(F32))
    wr_pack = jnp.pad(jnp.concatenate([wr_hi, wr_lo], axis=1), ((0, 0), (0, LANES - 2 * N_EXPERTS)))
    n_rep = HGRN_WIDTH // DIFF_HEAD_DIM

    wg_all = w_gate.astype(BF16).reshape(depth * N_EXPERTS, d, D_FF)
    wu_all = w_up.astype(BF16).reshape(depth * N_EXPERTS, d, D_FF)
    wd_all = w_down.astype(BF16).reshape(depth * N_EXPERTS, D_FF, d)
    p_all = p.reshape(depth, t, PLE_DIM)

    h = x.reshape(t, d)
    for i in range(depth):
        lam_init = 0.8 - 0.6 * math.exp(-0.3 * i)
        gq = (jnp.tile(q_norm[i], n_rep) * DIFF_HEAD_DIM ** -0.5).reshape(1, HGRN_WIDTH)
        gk = jnp.tile(k_norm[i], n_rep).reshape(1, HGRN_WIDTH)
        score_bound = 1.02 * DIFF_HEAD_DIM ** 0.5 * jnp.max(jnp.abs(q_norm[i])) * jnp.max(jnp.abs(k_norm[i]))
        fast = (score_bound <= MAX_UNSHIFTED_SCORE).astype(jnp.int32).reshape(1)
        w_bf = w_in[i].astype(BF16)
        n_main = IN_COLS - HGRN_WIDTH
        zq, zf, zi, zg, dq, dk, dvt = _inproj(h, mix_norm[i].reshape(1, d), w_bf[:, :n_main], w_bf[:, n_main:].T,
                                              gq, gk, grp, tq)
        r3 = lambda a: a.reshape(b, s, HGRN_WIDTH)
        o_hgrn = _hgrn(r3(zq), r3(zf), r3(zi), r3(zg), lower_bounds[i].reshape(1, HGRN_WIDTH),
                       hgrn_out_norm[i].reshape(1, HEAD), ts)
        row64 = lambda a: a.reshape(1, DIFF_HEAD_DIM)
        o_diff = _attn(fast, r3(dq), r3(dk), dvt, row64(lam_q1[i]), row64(lam_k1[i]), row64(lam_q2[i]),
                       row64(lam_k2[i]), diff_subln[i].reshape(1, HEAD), lam_init, tq)
        h1, xn_tiles, route = _outproj(o_hgrn.reshape(t, HGRN_WIDTH), o_diff.reshape(t, HGRN_WIDTH), h,
                                       w_out[i].astype(BF16), ffn_norm[i].reshape(1, d), wr_pack, tm)
        dest, row_tok, row_w, items = _routing_tables(route)
        y_tiles = _experts(items, row_tok.reshape(t // ROW_BLOCK, 1, ROW_BLOCK), row_w, xn_tiles,
                           wg_all, wu_all, wd_all, i)
        h = _combine(dest.reshape(t // tmc, 1, tmc), h1, p_all, i, ple_proj[i].astype(BF16),
                     ple_norm[i].reshape(1, d), ple_gate[i].astype(BF16), y_tiles, tmc)
    return h.reshape(b, s, d)
```

```python
import functools
import math

import numpy as np
import jax
import jax.numpy as jnp
from jax import lax
from jax.experimental import pallas as pl
from jax.experimental.pallas import tpu as pltpu

F32 = jnp.float32
BF16 = jnp.bfloat16

EPS = 1e-6
MIN_FORGET = 1e-6
MASK_VALUE = -1e30

D_MODEL = 1024
HGRN_WIDTH = 512
HEAD = 128
N_HEADS = 4
DIFF_HEAD_DIM = 64
PLE_DIM = 256
N_EXPERTS = 16
EXPERTS_PER_GROUP = 4
N_GROUPS = 4
D_FF = 512
IN_COLS = 4 * HGRN_WIDTH + 3 * HGRN_WIDTH

LANES = 128
SUBLANES = 8
ROW_TILES = D_MODEL // LANES
CHUNK = 128
ROW_BLOCK = 512
GATHER_UNROLL = 8
GATHER_SLOTS = 3
V7X_VMEM_BYTES = 64 * 1024 * 1024
VMEM_LIMIT = V7X_VMEM_BYTES * 3 // 4
TOKEN_TILE = 512
SEQ_TILE = 1024
MAX_UNSHIFTED_SCORE = 40.0

assert ROW_TILES == SUBLANES


def _dot(a, b):
    return jnp.dot(a, b, preferred_element_type=F32)


def _dot_nt(a, b):
    return lax.dot_general(a, b, (((1,), (1,)), ((), ())), preferred_element_type=F32)


def _split_bf16(x):
    hi = x.astype(BF16)
    lo = (x - hi.astype(F32)).astype(BF16)
    return hi, lo


def _params(sem, vmem=VMEM_LIMIT):
    return pltpu.CompilerParams(dimension_semantics=sem, vmem_limit_bytes=vmem)


def _store_token_tiles(ref, x, n):
    for a in range(ROW_TILES):
        ref[pl.ds(a, n, stride=ROW_TILES), :] = x[:, a * LANES:(a + 1) * LANES]


def _load_token_tiles(ref, n):
    return [ref[pl.ds(a, n, stride=ROW_TILES), :] for a in range(ROW_TILES)]


def _inproj_kernel(h_ref, g_ref, w_ref, wvt_ref, gq_ref, gk_ref, grp_ref,
                   zq_ref, zf_ref, zi_ref, zg_ref, dq_ref, dk_ref, dvt_ref):
    x = h_ref[...]
    ms = jnp.mean(x * x, axis=-1, keepdims=True)
    hn = (x * lax.rsqrt(ms + EPS) * g_ref[...]).astype(BF16)
    w = HGRN_WIDTH
    for j, o_ref in enumerate((zq_ref, zf_ref, zi_ref, zg_ref)):
        o_ref[...] = _dot(hn, w_ref[:, j * w:(j + 1) * w]).astype(BF16)
    for j, o_ref, gain_ref in ((4, dq_ref, gq_ref), (5, dk_ref, gk_ref)):
        z = _dot(hn, w_ref[:, j * w:(j + 1) * w])
        zz = (z * z).astype(BF16)
        gw = grp_ref.shape[0]
        ss = jnp.concatenate([_dot(zz[:, a:a + gw], grp_ref[...]) for a in range(0, w, gw)], axis=1)
        o_ref[...] = (z * lax.rsqrt(ss * (1.0 / DIFF_HEAD_DIM) + EPS) * gain_ref[...]).astype(BF16)
    dvt_ref[0] = _dot_nt(wvt_ref[...], hn).astype(BF16)


def _inproj(h, gain, w_main_bf, w_vt_bf, gq, gk, grp, tm):
    t = h.shape[0]
    w = HGRN_WIDTH
    row = lambda i: (i, 0)
    const = lambda i: (0, 0)
    out = jax.ShapeDtypeStruct((t, w), BF16)
    return pl.pallas_call(
        _inproj_kernel,
        grid=(t // tm,),
        in_specs=[
            pl.BlockSpec((tm, D_MODEL), row),
            pl.BlockSpec((1, D_MODEL), const),
            pl.BlockSpec((D_MODEL, IN_COLS - w), const),
            pl.BlockSpec((w, D_MODEL), const),
            pl.BlockSpec((1, w), const),
            pl.BlockSpec((1, w), const),
            pl.BlockSpec(grp.shape, const),
        ],
        out_specs=[pl.BlockSpec((tm, w), row)] * 6 + [pl.BlockSpec((1, w, tm), lambda i: (i, 0, 0))],
        out_shape=[out] * 6 + [jax.ShapeDtypeStruct((t // tm, w, tm), BF16)],
        compiler_params=_params(("arbitrary",)),
        name="inproj",
    )(h, gain, w_main_bf, w_vt_bf, gq, gk, grp)


_PAIR_HALVES = (8, 16, 32, 64)
_N_EXP = 3 + len(_PAIR_HALVES)


def _hgrn_constants():
    c = CHUNK
    t = np.arange(c)[:, None]
    j = np.arange(c)[None, :]
    mats = [(j <= t).astype(np.float32), (j > t).astype(np.float32)]
    mid = t - t % 8 + 3
    m0 = np.where((t > mid) & (j > mid) & (j <= t), 1.0, 0.0) - np.where((t < mid) & (j > t) & (j <= mid), 1.0, 0.0)
    mats.append(m0.astype(np.float32))
    lvl = np.full((c, c), -1, np.int32)
    tt, ss = np.broadcast_arrays(t, j)
    lvl[(tt // 8 == ss // 8) & (ss <= tt)] = 0
    for n, m in enumerate(_PAIR_HALVES):
        b = t - t % (2 * m) + m - 1
        mats.append(np.where(t > b, (j > b) & (j <= t), (j > t) & (j <= b)).astype(np.float32))
        sel = (tt // (2 * m) == ss // (2 * m)) & (tt % (2 * m) >= m) & (ss % (2 * m) < m)
        lvl[sel] = n + 1
    return np.concatenate(mats, axis=0), lvl


def _hgrn_kernel(zq_ref, zf_ref, zi_ref, zg_ref, lb_ref, og_ref, m_ref, lvl_ref, o_ref, state_ref, *, n_chunks):
    @pl.when(pl.program_id(1) == 0)
    def _():
        state_ref[...] = jnp.zeros_like(state_ref)

    lb = lb_ref[...]
    og = og_ref[...]
    lvl = lvl_ref[...]
    level_masks = [lvl == n for n in range(1 + len(_PAIR_HALVES))]
    c = CHUNK

    def chunk(ci, carry):
        sl = pl.ds(pl.multiple_of(ci * c, c), c)
        xq = zq_ref[0, sl, :].astype(F32)
        fl = zf_ref[0, sl, :].astype(F32)
        xg = zg_ref[0, sl, :].astype(F32)
        q_all = xq * jax.nn.sigmoid(xq)
        forget = lb + (1.0 - lb) * jax.nn.sigmoid(fl)
        logf = jnp.log(jnp.maximum(forget, MIN_FORGET))
        k_all = 1.0 - forget
        gate_all = xg * jax.nn.sigmoid(xg)
        e_all = _dot(m_ref[...], jnp.concatenate(_split_bf16(logf), axis=0))
        e0 = e_all[2 * c:3 * c]
        q_bf = q_all.astype(BF16)
        k_bf = k_all.astype(BF16)
        q_lv = [q_bf * jnp.exp(e0).astype(BF16)]
        k_lv = [k_bf * jnp.exp(-e0).astype(BF16)]
        for n in range(len(_PAIR_HALVES)):
            p = jnp.exp(e_all[(3 + n) * c:(4 + n) * c]).astype(BF16)
            q_lv.append(q_bf * p)
            k_lv.append(k_bf * p)
        p_cum = jnp.exp(e_all[0:c])
        q_in = q_bf * p_cum.astype(BF16)
        k_out = k_bf * jnp.exp(e_all[c:2 * c]).astype(BF16)
        heads = [slice(hd * HEAD, (hd + 1) * HEAD) for hd in range(N_HEADS)]
        sc = [[_dot_nt(q_lv[n][:, hs], k_lv[n][:, hs]) for n in range(len(q_lv))] for hs in heads]
        s_bf = []
        for per_level in sc:
            s = jnp.where(level_masks[0], per_level[0], 0.0)
            for n in range(1, len(per_level)):
                s = jnp.where(level_masks[n], per_level[n], s)
            s_bf.append(s.astype(BF16))
        v_bf = [zi_ref[0, sl, hs] for hs in heads]
        states = [state_ref[hd] for hd in range(N_HEADS)]
        outs = [_dot_nt(q_in[:, hs], states[hd].astype(BF16)) + _dot(s_bf[hd], v_bf[hd])
                for hd, hs in enumerate(heads)]
        for hd, hs in enumerate(heads):
            v_t = v_bf[hd].astype(F32).T.astype(BF16)
            state_ref[hd] = states[hd] * p_cum[c - 1:c, hs] + _dot(v_t, k_out[:, hs])
        for hd, hs in enumerate(heads):
            o = outs[hd]
            ms = jnp.mean(o * o, axis=-1, keepdims=True)
            y = o * lax.rsqrt(ms + EPS) * og * gate_all[:, hs]
            o_ref[0, sl, hs] = y.astype(BF16)
        return carry

    lax.fori_loop(0, n_chunks, chunk, 0, unroll=2)


def _hgrn(zq, zf, zi, zg, lb, og, ts):
    b, s, _ = zq.shape
    mats, lvl = _hgrn_constants()
    blk = pl.BlockSpec((1, ts, HGRN_WIDTH), lambda bi, si: (bi, si, 0))
    const = lambda bi, si: (0, 0)
    return pl.pallas_call(
        functools.partial(_hgrn_kernel, n_chunks=ts // CHUNK),
        grid=(b, s // ts),
        in_specs=[blk, blk, blk, blk,
                  pl.BlockSpec((1, HGRN_WIDTH), const),
                  pl.BlockSpec((1, HEAD), const),
                  pl.BlockSpec((_N_EXP * CHUNK, 2 * CHUNK), const),
                  pl.BlockSpec((CHUNK, CHUNK), const)],
        out_specs=blk,
        out_shape=jax.ShapeDtypeStruct((b, s, HGRN_WIDTH), BF16),
        scratch_shapes=[pltpu.VMEM((N_HEADS, HEAD, HEAD), F32)],
        compiler_params=_params(("arbitrary", "arbitrary")),
        name="hgrn2",
    )(zq, zf, zi, zg, lb, og, jnp.asarray(np.concatenate([mats, mats], axis=1), BF16), jnp.asarray(lvl))


def _attn_kernel(fast_ref, q_ref, k_ref, vt_ref, lq1_ref, lk1_ref, lq2_ref, lk2_ref, sub_ref, o_ref,
                 m_sc, l_sc, acc_sc, *, tq, lam_init):
    qi = pl.program_id(1)
    lane = lax.broadcasted_iota(jnp.int32, (tq, HEAD), 1)
    q_maps = []
    for hd in range(N_HEADS):
        q = q_ref[0, :, hd * HEAD:(hd + 1) * HEAD]
        zero = jnp.zeros_like(q)
        q_maps.append((jnp.where(lane < DIFF_HEAD_DIM, q, zero), jnp.where(lane >= DIFF_HEAD_DIM, q, zero)))

    def sublane_tile_sum(p):
        return jnp.sum(p.reshape(p.shape[0] // SUBLANES, SUBLANES, p.shape[1]), axis=0)

    def scores(q_map, kb, masked, first_key=0):
        s = _dot_nt(kb, q_map)
        if masked:
            key = lax.broadcasted_iota(jnp.int32, s.shape, 0) + first_key
            qry = lax.broadcasted_iota(jnp.int32, s.shape, 1)
            s = jnp.where(key <= qry, s, MASK_VALUE)
        return s

    def run(update):
        def step(ki, masked):
            k_all = k_ref[0, pl.ds(pl.multiple_of(ki * tq, tq), tq), :]
            vt_all = vt_ref[ki]
            for hd in range(N_HEADS):
                kb = k_all[:, hd * HEAD:(hd + 1) * HEAD]
                vt = vt_all[hd * HEAD:(hd + 1) * HEAD, :]
                for c in range(2):
                    update(hd, c, scores(q_maps[hd][c], kb, masked), vt)

        def body(ki, carry):
            step(ki, False)
            return carry
        lax.fori_loop(0, qi, body, 0)
        step(qi, True)

    @pl.when(fast_ref[0] == 1)
    def _():
        chains = [(hd, c) for hd in range(N_HEADS) for c in range(2)]
        heads = [slice(hd * HEAD, (hd + 1) * HEAD) for hd in range(N_HEADS)]
        half = tq // 2
        assert half % LANES == 0

        k_all = k_ref[0, pl.ds(pl.multiple_of(qi * tq, tq), tq), :]
        vt_all = vt_ref[qi]
        tops = [jnp.exp(scores(q_maps[hd][c], k_all[0:half, heads[hd]], True)) for hd, c in chains]
        bots = [jnp.exp(scores(q_maps[hd][c][half:], k_all[half:, heads[hd]], True)) for hd, c in chains]
        for (hd, c), p_top, p_bot in zip(chains, tops, bots):
            vt = vt_all[heads[hd], :]
            l_sc[hd, c] = sublane_tile_sum(p_top)
            l_sc[hd, c, :, half:] += sublane_tile_sum(p_bot)
            acc_sc[hd, c] = _dot(vt[:, 0:half], p_top.astype(BF16))
            acc_sc[hd, c, :, half:] += _dot(vt[:, half:], p_bot.astype(BF16))

        def body(ki, carry):
            k_all = k_ref[0, pl.ds(pl.multiple_of(ki * tq, tq), tq), :]
            vt_all = vt_ref[ki]
            ps = [jnp.exp(scores(q_maps[hd][c], k_all[:, heads[hd]], False)) for hd, c in chains]
            for (hd, c), p in zip(chains, ps):
                l_sc[hd, c] += sublane_tile_sum(p)
                acc_sc[hd, c] += _dot(vt_all[heads[hd], :], p.astype(BF16))
            return carry
        lax.fori_loop(0, qi, body, 0)

    @pl.when(fast_ref[0] != 1)
    def _():
        m_sc[...] = jnp.full_like(m_sc, MASK_VALUE)
        l_sc[...] = jnp.zeros_like(l_sc)
        acc_sc[...] = jnp.zeros_like(acc_sc)

        def update(hd, c, s, vt):
            m_old = m_sc[hd, c]
            m_new = jnp.maximum(m_old, jnp.max(s, axis=0, keepdims=True))
            alpha = jnp.exp(m_old - m_new)
            p = jnp.exp(s - m_new)
            l_sc[hd, c] = alpha * l_sc[hd, c] + sublane_tile_sum(p)
            acc_sc[hd, c] = alpha * acc_sc[hd, c] + _dot(vt, p.astype(BF16))
            m_sc[hd, c] = m_new
        run(update)

    lam = (jnp.exp(jnp.sum(lq1_ref[...] * lk1_ref[...], keepdims=True))
           - jnp.exp(jnp.sum(lq2_ref[...] * lk2_ref[...], keepdims=True)) + lam_init)
    for hd in range(N_HEADS):
        l0 = jnp.sum(l_sc[hd, 0], axis=0, keepdims=True)
        l1 = jnp.sum(l_sc[hd, 1], axis=0, keepdims=True)
        o_t = acc_sc[hd, 0] / l0 - lam * (acc_sc[hd, 1] / l1)
        ms = jnp.mean(o_t * o_t, axis=0, keepdims=True)
        o = (o_t * lax.rsqrt(ms + EPS)).T
        o_ref[0, :, hd * HEAD:(hd + 1) * HEAD] = (o * sub_ref[...] * (1.0 - lam_init)).astype(BF16)


def _attn(fast, q, k, vt, lq1, lk1, lq2, lk2, sub, lam_init, tq):
    b, s, w = q.shape
    n_kv = s // tq
    qblk = pl.BlockSpec((1, tq, w), lambda bi, qi, f: (bi, qi, 0))
    kblk = pl.BlockSpec((1, s, w), lambda bi, qi, f: (bi, 0, 0))
    vtblk = pl.BlockSpec((n_kv, w, tq), lambda bi, qi, f: (bi, 0, 0))
    const = lambda bi, qi, f: (0, 0)
    lamspec = pl.BlockSpec((1, DIFF_HEAD_DIM), const)
    return pl.pallas_call(
        functools.partial(_attn_kernel, tq=tq, lam_init=lam_init),
        grid_spec=pltpu.PrefetchScalarGridSpec(
            num_scalar_prefetch=1,
            grid=(b, s // tq),
            in_specs=[qblk, kblk, vtblk, lamspec, lamspec, lamspec, lamspec, pl.BlockSpec((1, HEAD), const)],
            out_specs=qblk,
            scratch_shapes=[pltpu.VMEM((N_HEADS, 2, 1, tq), F32), pltpu.VMEM((N_HEADS, 2, SUBLANES, tq), F32),
                            pltpu.VMEM((N_HEADS, 2, HEAD, tq), F32)],
        ),
        out_shape=jax.ShapeDtypeStruct((b, s, w), BF16),
        compiler_params=_params(("arbitrary", "arbitrary")),
        name="diffattn",
    )(fast, q, k, vt, lq1, lk1, lq2, lk2, sub)


def _outproj_kernel(oh_ref, od_ref, h_ref, wo_ref, fg_ref, wr_ref, h1_ref, xn_ref, route_ref, *, tm):
    n_sub = 2 if tm % (2 * LANES) == 0 else 1
    sub = tm // n_sub
    for si in range(n_sub):
        rs = slice(si * sub, (si + 1) * sub)
        _outproj_rows(oh_ref[rs, :], od_ref[rs, :], h_ref[rs, :], wo_ref, fg_ref, wr_ref,
                      h1_ref.at[rs], xn_ref.at[pl.ds(si * sub * ROW_TILES, sub * ROW_TILES)], route_ref.at[:, rs], sub)


def _outproj_rows(oh, od, h, wo_ref, fg_ref, wr_ref, h1_ref, xn_ref, route_ref, tm):
    hw = HGRN_WIDTH
    h1 = h + _dot(oh, wo_ref[0:hw, :]) + _dot(od, wo_ref[hw:2 * hw, :])
    h1_ref[...] = h1
    ms = jnp.mean(h1 * h1, axis=-1, keepdims=True)
    xn = h1 * lax.rsqrt(ms + EPS) * fg_ref[...]
    _store_token_tiles(xn_ref, xn, tm)
    x_hi, x_lo = _split_bf16(xn)
    a = _dot(x_hi, wr_ref[...])
    logits = a + pltpu.roll(a, LANES - N_EXPERTS, axis=1) + _dot(x_lo, wr_ref[...])
    lt = logits.T[0:N_EXPERTS, :]
    ex = jnp.exp(lt - jnp.max(lt, axis=0, keepdims=True))
    aff = ex / jnp.sum(ex, axis=0, keepdims=True)
    rows = [aff[e:e + 1, :] for e in range(N_EXPERTS)]

    def top2(a, first):
        m1, i1 = a[0], jnp.full_like(a[0], first)
        for j in range(1, len(a)):
            better = a[j] > m1
            m1 = jnp.where(better, a[j], m1)
            i1 = jnp.where(better, float(first + j), i1)
        m2, i2 = jnp.full_like(m1, -1.0), jnp.zeros_like(m1)
        for j in range(len(a)):
            cand = jnp.where(i1 == float(first + j), -1.0, a[j])
            better = cand > m2
            m2 = jnp.where(better, cand, m2)
            i2 = jnp.where(better, float(first + j), i2)
        return m1, i1, m2, i2

    best = None
    for g in range(N_GROUPS):
        first = g * EXPERTS_PER_GROUP
        m1, i1, m2, i2 = top2(rows[first:first + EXPERTS_PER_GROUP], first)
        cand = (m1, i1 - first, m2, i2 - first, jnp.full_like(m1, float(g)))
        score = m1 + m2
        if best is None:
            best, best_score = cand, score
        else:
            better = score > best_score
            best = tuple(jnp.where(better, c, b) for c, b in zip(cand, best))
            best_score = jnp.where(better, score, best_score)
    m1, j1, m2, j2, gsel = best
    denom = m1 + m2
    w1, w2 = m1 / denom, m2 / denom
    rid = lax.broadcasted_iota(jnp.int32, (SUBLANES, tm), 0).astype(F32)
    r8 = jnp.where(rid == j1, w1, jnp.where(rid == j2, w2, 0.0))
    route_ref[...] = jnp.where(rid == float(EXPERTS_PER_GROUP), gsel, r8)


def _outproj(oh, od, h, wo_bf, fg, wr_pack, tm):
    t = h.shape[0]
    row = lambda i: (i, 0)
    const = lambda i: (0, 0)
    return pl.pallas_call(
        functools.partial(_outproj_kernel, tm=tm),
        grid=(t // tm,),
        in_specs=[
            pl.BlockSpec((tm, HGRN_WIDTH), row),
            pl.BlockSpec((tm, HGRN_WIDTH), row),
            pl.BlockSpec((tm, D_MODEL), row),
            pl.BlockSpec((D_MODEL, D_MODEL), const),
            pl.BlockSpec((1, D_MODEL), const),
            pl.BlockSpec((D_MODEL, LANES), const),
        ],
        out_specs=[pl.BlockSpec((tm, D_MODEL), row), pl.BlockSpec((tm * ROW_TILES, LANES), row),
                   pl.BlockSpec((SUBLANES, tm), lambda i: (0, i))],
        out_shape=[jax.ShapeDtypeStruct((t, D_MODEL), F32), jax.ShapeDtypeStruct((t * ROW_TILES, LANES), F32),
                   jax.ShapeDtypeStruct((SUBLANES, t), F32)],
        compiler_params=_params(("arbitrary",)),
        name="outproj_router",
    )(oh, od, h, wo_bf, fg, wr_pack)


def _tile_copy(src, src_row, dst, dst_row, sem):
    return pltpu.make_async_copy(src.at[pl.ds(pl.multiple_of(src_row * ROW_TILES, ROW_TILES), ROW_TILES)],
                                 dst.at[pl.ds(pl.multiple_of(dst_row * ROW_TILES, ROW_TILES), ROW_TILES)], sem)


def _gather_rows(idx_ref, n, src_hbm, dst, sem):
    def issue(jo, carry):
        for u in range(GATHER_UNROLL):
            j = jo * GATHER_UNROLL + u
            _tile_copy(src_hbm, idx_ref[0, 0, j], dst, j, sem).start()
        return carry
    lax.fori_loop(0, n // GATHER_UNROLL, issue, 0)


def _gather_rows_inline(idx_ref, n, src_hbm, dst, sem):
    for j in range(n):
        _tile_copy(src_hbm, idx_ref[0, 0, j], dst, j, sem).start()


def _wait_rows(n, src_hbm, dst, sem):
    pltpu.make_async_copy(src_hbm.at[pl.ds(0, n * ROW_TILES)], dst, sem).wait()


def _expert_kernel(item_blk_ref, item_g_ref, item_kind_ref, tok0_ref, tok1_ref, tok2_ref, xn_hbm, wrow_ref,
                   wg_ref, wu_ref, wd_ref, y_ref, xbuf, sem, *, n_blk):
    k = pl.program_id(0)
    i = item_blk_ref[k]
    group = item_g_ref[k]
    kind = item_kind_ref[k]
    slot = i % GATHER_SLOTS

    @pl.when(k == 0)
    def _():
        _gather_rows(tok0_ref, ROW_BLOCK, xn_hbm, xbuf.at[0], sem.at[0])
        _gather_rows(tok1_ref, ROW_BLOCK, xn_hbm, xbuf.at[1], sem.at[1])

    def group_rows(x):
        wrow = wrow_ref[...]
        in_group = wrow[:, EXPERTS_PER_GROUP:EXPERTS_PER_GROUP + 1] == group.astype(F32)
        y = None
        for j in range(EXPERTS_PER_GROUP):
            g = _dot(x, wg_ref[j])
            u = _dot(x, wu_ref[j])
            wj = jnp.where(in_group, wrow[:, j:j + 1], 0.0)
            mid = jnp.where(wj != 0.0, (g * jax.nn.sigmoid(g)) * u * wj, 0.0)
            part = _dot(mid.astype(BF16), wd_ref[j])
            y = part if y is None else y + part
        return y

    @pl.when(kind == 1)
    def _():
        ahead = (i + 2) % GATHER_SLOTS
        _wait_rows(ROW_BLOCK, xn_hbm, xbuf.at[slot], sem.at[slot])
        x = jnp.concatenate(_load_token_tiles(xbuf.at[slot], ROW_BLOCK), axis=1).astype(BF16)
        _gather_rows_inline(tok2_ref, ROW_BLOCK, xn_hbm, xbuf.at[ahead], sem.at[ahead])
        _store_token_tiles(y_ref, group_rows(x), ROW_BLOCK)

    @pl.when(kind == 2)
    def _():
        x = jnp.concatenate(_load_token_tiles(xbuf.at[slot], ROW_BLOCK), axis=1).astype(BF16)
        prev = jnp.concatenate(_load_token_tiles(y_ref, ROW_BLOCK), axis=1)
        _store_token_tiles(y_ref, prev + group_rows(x), ROW_BLOCK)

    @pl.when(k == pl.num_programs(0) - 1)
    def _():
        for extra in (0, 1):
            s_extra = (n_blk + extra) % GATHER_SLOTS
            _wait_rows(ROW_BLOCK, xn_hbm, xbuf.at[s_extra], sem.at[s_extra])


def _experts(items, row_tok, row_w, xn_tiles, wg, wu, wd, layer):
    n_blk = row_tok.shape[0]
    assert n_blk >= GATHER_SLOTS
    wsel = lambda k, ib, ig, ik: (layer * N_GROUPS + ig[k], 0, 0)
    blk = lambda k, ib, ig, ik: (ib[k], 0)

    def tok_spec(ahead):
        return pl.BlockSpec((1, 1, ROW_BLOCK), lambda k, ib, ig, ik: (jnp.minimum(ib[k] + ahead, n_blk - 1), 0, 0),
                            memory_space=pltpu.SMEM)

    return pl.pallas_call(
        functools.partial(_expert_kernel, n_blk=n_blk),
        grid_spec=pltpu.PrefetchScalarGridSpec(
            num_scalar_prefetch=3,
            grid=(items[0].shape[0],),
            in_specs=[tok_spec(0), tok_spec(1), tok_spec(2),
                      pl.BlockSpec(memory_space=pl.ANY),
                      pl.BlockSpec((ROW_BLOCK, SUBLANES), blk),
                      pl.BlockSpec((EXPERTS_PER_GROUP, D_MODEL, D_FF), wsel),
                      pl.BlockSpec((EXPERTS_PER_GROUP, D_MODEL, D_FF), wsel),
                      pl.BlockSpec((EXPERTS_PER_GROUP, D_FF, D_MODEL), wsel)],
            out_specs=pl.BlockSpec((ROW_BLOCK * ROW_TILES, LANES), blk),
            scratch_shapes=[pltpu.VMEM((GATHER_SLOTS, ROW_BLOCK * ROW_TILES, LANES), F32),
                            pltpu.SemaphoreType.DMA((GATHER_SLOTS,))],
        ),
        out_shape=jax.ShapeDtypeStruct((n_blk * ROW_BLOCK * ROW_TILES, LANES), F32),
        compiler_params=_params(("arbitrary",)),
        name="moe_experts",
    )(*items, row_tok, row_tok, row_tok, xn_tiles, row_w, wg, wu, wd)


def _combine_kernel(dest_ref, dest_next_ref, h1_ref, p_ref, pp_ref, pn_ref, pg_ref, y_hbm, o_ref,
                    ybuf, sem, *, tm):
    i = pl.program_id(0)
    slot = i % 2

    @pl.when(i == 0)
    def _():
        _gather_rows(dest_ref, tm, y_hbm, ybuf.at[0], sem.at[0])

    _gather_rows_inline(dest_next_ref, tm, y_hbm, ybuf.at[1 - slot], sem.at[1 - slot])

    e = _dot(p_ref[0].astype(BF16), pp_ref[...])
    ms = jnp.mean(e * e, axis=-1, keepdims=True)
    ple = e * lax.rsqrt(ms + EPS) * pn_ref[...]

    _wait_rows(tm, y_hbm, ybuf.at[slot], sem.at[slot])
    h2 = h1_ref[...] + jnp.concatenate(_load_token_tiles(ybuf.at[slot], tm), axis=1)
    gate = jax.nn.sigmoid(_dot(h2.astype(BF16), pg_ref[...]))
    o_ref[...] = h2 + ple * gate

    @pl.when(i == pl.num_programs(0) - 1)
    def _():
        _wait_rows(tm, y_hbm, ybuf.at[1 - slot], sem.at[1 - slot])


def _combine(dest3, h1, p, layer, pp_bf, pn, pg_bf, y_tiles, tm):
    t = h1.shape[0]
    n_tiles = t // tm
    row = lambda i: (i, 0)
    const = lambda i: (0, 0)
    n = tm
    return pl.pallas_call(
        functools.partial(_combine_kernel, tm=tm),
        grid=(n_tiles,),
        in_specs=[pl.BlockSpec((1, 1, n), lambda i: (i, 0, 0), memory_space=pltpu.SMEM),
                  pl.BlockSpec((1, 1, n), lambda i: (jnp.minimum(i + 1, n_tiles - 1), 0, 0), memory_space=pltpu.SMEM),
                  pl.BlockSpec((tm, D_MODEL), row),
                  pl.BlockSpec((1, tm, PLE_DIM), lambda i: (layer, i, 0)),
                  pl.BlockSpec((PLE_DIM, D_MODEL), const),
                  pl.BlockSpec((1, D_MODEL), const),
                  pl.BlockSpec((D_MODEL, D_MODEL), const),
                  pl.BlockSpec(memory_space=pl.ANY)],
        out_specs=pl.BlockSpec((tm, D_MODEL), row),
        out_shape=jax.ShapeDtypeStruct((t, D_MODEL), F32),
        scratch_shapes=[pltpu.VMEM((2, n * ROW_TILES, LANES), F32), pltpu.SemaphoreType.DMA((2,))],
        compiler_params=_params(("arbitrary",)),
        name="moe_combine_ple",
    )(dest3, dest3, h1, p, pp_bf, pn, pg_bf, y_tiles)


def _routing_tables(route):
    n_tok = route.shape[1]
    assert n_tok % ROW_BLOCK == 0
    tok_g = route[EXPERTS_PER_GROUP].astype(jnp.int32)
    iota = jnp.arange(n_tok, dtype=jnp.int32)
    weights = [route[j] for j in range(EXPERTS_PER_GROUP)]
    sorted_g, row_tok, *row_weights = lax.sort((tok_g, iota, *weights), num_keys=1, is_stable=True)
    _, dest = lax.sort((row_tok, iota), num_keys=1)
    zeros = jnp.zeros_like(row_weights[0])
    row_w = jnp.stack(row_weights + [sorted_g.astype(F32)] + [zeros] * (SUBLANES - EXPERTS_PER_GROUP - 1), axis=1)
    g_lo = sorted_g[0::ROW_BLOCK]
    g_hi = sorted_g[ROW_BLOCK - 1::ROW_BLOCK]
    n_blk = n_tok // ROW_BLOCK
    per_blk = g_hi - g_lo + 1
    first_item = jnp.cumsum(per_blk) - per_blk
    k = jnp.arange(n_blk + N_GROUPS - 1, dtype=jnp.int32)
    item_blk = (jnp.sum((first_item[None, :] <= k[:, None]).astype(jnp.int32), axis=1) - 1).astype(jnp.int32)
    nth = k - first_item[item_blk]
    valid = nth < per_blk[item_blk]
    item_g = jnp.minimum(g_lo[item_blk] + nth, g_hi[item_blk]).astype(jnp.int32)
    item_kind = jnp.where(valid, jnp.where(nth == 0, 1, 2), 0).astype(jnp.int32)
    return dest, row_tok, row_w, (item_blk, item_g, item_kind)


def _tile(n, pref):
    return pref if n % pref == 0 else n


def kernel(x, p, mix_norm, w_in, hgrn_lb, hgrn_out_norm, q_norm, k_norm, lam_q1, lam_k1, lam_q2, lam_k2,
           diff_subln, w_out, ffn_norm, w_router, w_gate, w_up, w_down, ple_proj, ple_norm, ple_gate):
    b, s, d = x.shape
    depth = w_in.shape[0]
    t = b * s
    tm = _tile(t, TOKEN_TILE)
    ts = _tile(s, SEQ_TILE)
    tq = _tile(s, TOKEN_TILE)
    tmc = _tile(t, TOKEN_TILE)

    lb_soft = jax.nn.softmax(hgrn_lb.astype(F32), axis=0)
    lower_bounds = jnp.cumsum(lb_soft, axis=0) - lb_soft[0]
    grp = np.kron(np.eye(2 * LANES // DIFF_HEAD_DIM), np.ones((DIFF_HEAD_DIM, DIFF_HEAD_DIM)))
    grp = jnp.asarray(grp, BF16)
    wr_hi, wr_lo = _split_bf16(w_router.astype(F32))
    wr_pack = jnp.pad(jnp.concatenate([wr_hi, wr_lo], axis=1), ((0, 0), (0, LANES - 2 * N_EXPERTS)))
    n_rep = HGRN_WIDTH // DIFF_HEAD_DIM

    wg_all = w_gate.astype(BF16).reshape(depth * N_EXPERTS, d, D_FF)
    wu_all = w_up.astype(BF16).reshape(depth * N_EXPERTS, d, D_FF)
    wd_all = w_down.astype(BF16).reshape(depth * N_EXPERTS, D_FF, d)
    p_all = p.reshape(depth, t, PLE_DIM)

    h = x.reshape(t, d)
    for i in range(depth):
        lam_init = 0.8 - 0.6 * math.exp(-0.3 * i)
        gq = (jnp.tile(q_norm[i], n_rep) * DIFF_HEAD_DIM ** -0.5).reshape(1, HGRN_WIDTH)
        gk = jnp.tile(k_norm[i], n_rep).reshape(1, HGRN_WIDTH)
        score_bound = 1.02 * DIFF_HEAD_DIM ** 0.5 * jnp.max(jnp.abs(q_norm[i])) * jnp.max(jnp.abs(k_norm[i]))
        fast = (score_bound <= MAX_UNSHIFTED_SCORE).astype(jnp.int32).reshape(1)
        w_bf = w_in[i].astype(BF16)
        n_main = IN_COLS - HGRN_WIDTH
        zq, zf, zi, zg, dq, dk, dvt = _inproj(h, mix_norm[i].reshape(1, d), w_bf[:, :n_main], w_bf[:, n_main:].T,
                                              gq, gk, grp, tq)
        r3 = lambda a: a.reshape(b, s, HGRN_WIDTH)
        o_hgrn = _hgrn(r3(zq), r3(zf), r3(zi), r3(zg), lower_bounds[i].reshape(1, HGRN_WIDTH),
                       hgrn_out_norm[i].reshape(1, HEAD), ts)
        row64 = lambda a: a.reshape(1, DIFF_HEAD_DIM)
        o_diff = _attn(fast, r3(dq), r3(dk), dvt, row64(lam_q1[i]), row64(lam_k1[i]), row64(lam_q2[i]),
                       row64(lam_k2[i]), diff_subln[i].reshape(1, HEAD), lam_init, tq)
        h1, xn_tiles, route = _outproj(o_hgrn.reshape(t, HGRN_WIDTH), o_diff.reshape(t, HGRN_WIDTH), h,
                                       w_out[i].astype(BF16), ffn_norm[i].reshape(1, d), wr_pack, tm)
        dest, row_tok, row_w, items = _routing_tables(route)
        y_tiles = _experts(items, row_tok.reshape(t // ROW_BLOCK, 1, ROW_BLOCK), row_w, xn_tiles,
                           wg_all, wu_all, wd_all, i)
        h = _combine(dest.reshape(t // tmc, 1, tmc), h1, p_all, i, ple_proj[i].astype(BF16),
                     ple_norm[i].reshape(1, d), ple_gate[i].astype(BF16), y_tiles, tmc)
    return h.reshape(b, s, d)
```

```python
import functools
import math

import numpy as np
import jax
import jax.numpy as jnp
from jax import lax
from jax.experimental import pallas as pl
from jax.experimental.pallas import tpu as pltpu

F32 = jnp.float32
BF16 = jnp.bfloat16

EPS = 1e-6
MIN_FORGET = 1e-6
MASK_VALUE = -1e30

D_MODEL = 1024
HGRN_WIDTH = 512
HEAD = 128
N_HEADS = 4
DIFF_HEAD_DIM = 64
PLE_DIM = 256
N_EXPERTS = 16
EXPERTS_PER_GROUP = 4
N_GROUPS = 4
D_FF = 512
IN_COLS = 4 * HGRN_WIDTH + 3 * HGRN_WIDTH

LANES = 128
SUBLANES = 8
ROW_TILES = D_MODEL // LANES
CHUNK = 128
ROW_BLOCK = 512
GATHER_UNROLL = 8
GATHER_SLOTS = 3
V7X_VMEM_BYTES = 64 * 1024 * 1024
VMEM_LIMIT = V7X_VMEM_BYTES * 3 // 4
TOKEN_TILE = 512
SEQ_TILE = 1024
MAX_UNSHIFTED_SCORE = 40.0

assert ROW_TILES == SUBLANES


def _dot(a, b):
    return jnp.dot(a, b, preferred_element_type=F32)


def _dot_nt(a, b):
    return lax.dot_general(a, b, (((1,), (1,)), ((), ())), preferred_element_type=F32)


def _split_bf16(x):
    hi = x.astype(BF16)
    lo = (x - hi.astype(F32)).astype(BF16)
    return hi, lo


def _params(sem, vmem=VMEM_LIMIT):
    return pltpu.CompilerParams(dimension_semantics=sem, vmem_limit_bytes=vmem)


def _store_token_tiles(ref, x, n):
    for a in range(ROW_TILES):
        ref[pl.ds(a, n, stride=ROW_TILES), :] = x[:, a * LANES:(a + 1) * LANES]


def _load_token_tiles(ref, n):
    return [ref[pl.ds(a, n, stride=ROW_TILES), :] for a in range(ROW_TILES)]


def _inproj_kernel(h_ref, g_ref, w_ref, wvt_ref, gq_ref, gk_ref, grp_ref,
                   zq_ref, zf_ref, zi_ref, zg_ref, dq_ref, dk_ref, dvt_ref):
    x = h_ref[...]
    ms = jnp.mean(x * x, axis=-1, keepdims=True)
    hn = (x * lax.rsqrt(ms + EPS) * g_ref[...]).astype(BF16)
    w = HGRN_WIDTH
    for j, o_ref in enumerate((zq_ref, zf_ref, zi_ref, zg_ref)):
        o_ref[...] = _dot(hn, w_ref[:, j * w:(j + 1) * w]).astype(BF16)
    for j, o_ref, gain_ref in ((4, dq_ref, gq_ref), (5, dk_ref, gk_ref)):
        z = _dot(hn, w_ref[:, j * w:(j + 1) * w])
        zz = (z * z).astype(BF16)
        gw = grp_ref.shape[0]
        ss = jnp.concatenate([_dot(zz[:, a:a + gw], grp_ref[...]) for a in range(0, w, gw)], axis=1)
        o_ref[...] = (z * lax.rsqrt(ss * (1.0 / DIFF_HEAD_DIM) + EPS) * gain_ref[...]).astype(BF16)
    dvt_ref[0] = _dot_nt(wvt_ref[...], hn).astype(BF16)


def _inproj(h, gain, w_main_bf, w_vt_bf, gq, gk, grp, tm):
    t = h.shape[0]
    w = HGRN_WIDTH
    row = lambda i: (i, 0)
    const = lambda i: (0, 0)
    out = jax.ShapeDtypeStruct((t, w), BF16)
    return pl.pallas_call(
        _inproj_kernel,
        grid=(t // tm,),
        in_specs=[
            pl.BlockSpec((tm, D_MODEL), row),
            pl.BlockSpec((1, D_MODEL), const),
            pl.BlockSpec((D_MODEL, IN_COLS - w), const),
            pl.BlockSpec((w, D_MODEL), const),
            pl.BlockSpec((1, w), const),
            pl.BlockSpec((1, w), const),
            pl.BlockSpec(grp.shape, const),
        ],
        out_specs=[pl.BlockSpec((tm, w), row)] * 6 + [pl.BlockSpec((1, w, tm), lambda i: (i, 0, 0))],
        out_shape=[out] * 6 + [jax.ShapeDtypeStruct((t // tm, w, tm), BF16)],
        compiler_params=_params(("arbitrary",)),
        name="inproj",
    )(h, gain, w_main_bf, w_vt_bf, gq, gk, grp)


_PAIR_HALVES = (8, 16, 32, 64)
_N_EXP = 3 + len(_PAIR_HALVES)


def _hgrn_constants():
    c = CHUNK
    t = np.arange(c)[:, None]
    j = np.arange(c)[None, :]
    mats = [(j <= t).astype(np.float32), (j > t).astype(np.float32)]
    mid = t - t % 8 + 3
    m0 = np.where((t > mid) & (j > mid) & (j <= t), 1.0, 0.0) - np.where((t < mid) & (j > t) & (j <= mid), 1.0, 0.0)
    mats.append(m0.astype(np.float32))
    lvl = np.full((c, c), -1, np.int32)
    tt, ss = np.broadcast_arrays(t, j)
    lvl[(tt // 8 == ss // 8) & (ss <= tt)] = 0
    for n, m in enumerate(_PAIR_HALVES):
        b = t - t % (2 * m) + m - 1
        mats.append(np.where(t > b, (j > b) & (j <= t), (j > t) & (j <= b)).astype(np.float32))
        sel = (tt // (2 * m) == ss // (2 * m)) & (tt % (2 * m) >= m) & (ss % (2 * m) < m)
        lvl[sel] = n + 1
    return np.concatenate(mats, axis=0), lvl


def _hgrn_kernel(zq_ref, zf_ref, zi_ref, zg_ref, lb_ref, og_ref, m_ref, lvl_ref, o_ref, state_ref, *, n_chunks):
    @pl.when(pl.program_id(1) == 0)
    def _():
        state_ref[...] = jnp.zeros_like(state_ref)

    lb = lb_ref[...]
    og = og_ref[...]
    lvl = lvl_ref[...]
    level_masks = [lvl == n for n in range(1 + len(_PAIR_HALVES))]
    c = CHUNK

    def chunk(ci, carry):
        sl = pl.ds(pl.multiple_of(ci * c, c), c)
        xq = zq_ref[0, sl, :].astype(F32)
        fl = zf_ref[0, sl, :].astype(F32)
        xg = zg_ref[0, sl, :].astype(F32)
        q_all = xq * jax.nn.sigmoid(xq)
        forget = lb + (1.0 - lb) * jax.nn.sigmoid(fl)
        logf = jnp.log(jnp.maximum(forget, MIN_FORGET))
        k_all = 1.0 - forget
        gate_all = xg * jax.nn.sigmoid(xg)
        e_all = _dot(m_ref[...], jnp.concatenate(_split_bf16(logf), axis=0))
        e0 = e_all[2 * c:3 * c]
        q_bf = q_all.astype(BF16)
        k_bf = k_all.astype(BF16)
        q_lv = [q_bf * jnp.exp(e0).astype(BF16)]
        k_lv = [k_bf * jnp.exp(-e0).astype(BF16)]
        for n in range(len(_PAIR_HALVES)):
            p = jnp.exp(e_all[(3 + n) * c:(4 + n) * c]).astype(BF16)
            q_lv.append(q_bf * p)
            k_lv.append(k_bf * p)
        p_cum = jnp.exp(e_all[0:c])
        q_in = q_bf * p_cum.astype(BF16)
        k_out = k_bf * jnp.exp(e_all[c:2 * c]).astype(BF16)
        heads = [slice(hd * HEAD, (hd + 1) * HEAD) for hd in range(N_HEADS)]
        sc = [[_dot_nt(q_lv[n][:, hs], k_lv[n][:, hs]) for n in range(len(q_lv))] for hs in heads]
        s_bf = []
        for per_level in sc:
            s = jnp.where(level_masks[0], per_level[0], 0.0)
            for n in range(1, len(per_level)):
                s = jnp.where(level_masks[n], per_level[n], s)
            s_bf.append(s.astype(BF16))
        v_bf = [zi_ref[0, sl, hs] for hs in heads]
        states = [state_ref[hd] for hd in range(N_HEADS)]
        outs = [_dot_nt(q_in[:, hs], states[hd].astype(BF16)) + _dot(s_bf[hd], v_bf[hd])
                for hd, hs in enumerate(heads)]
        for hd, hs in enumerate(heads):
            v_t = v_bf[hd].astype(F32).T.astype(BF16)
            state_ref[hd] = states[hd] * p_cum[c - 1:c, hs] + _dot(v_t, k_out[:, hs])
        for hd, hs in enumerate(heads):
            o = outs[hd]
            ms = jnp.mean(o * o, axis=-1, keepdims=True)
            y = o * lax.rsqrt(ms + EPS) * og * gate_all[:, hs]
            o_ref[0, sl, hs] = y.astype(BF16)
        return carry

    lax.fori_loop(0, n_chunks, chunk, 0, unroll=4)


def _hgrn(zq, zf, zi, zg, lb, og, ts):
    b, s, _ = zq.shape
    mats, lvl = _hgrn_constants()
    blk = pl.BlockSpec((1, ts, HGRN_WIDTH), lambda bi, si: (bi, si, 0))
    const = lambda bi, si: (0, 0)
    return pl.pallas_call(
        functools.partial(_hgrn_kernel, n_chunks=ts // CHUNK),
        grid=(b, s // ts),
        in_specs=[blk, blk, blk, blk,
                  pl.BlockSpec((1, HGRN_WIDTH), const),
                  pl.BlockSpec((1, HEAD), const),
                  pl.BlockSpec((_N_EXP * CHUNK, 2 * CHUNK), const),
                  pl.BlockSpec((CHUNK, CHUNK), const)],
        out_specs=blk,
        out_shape=jax.ShapeDtypeStruct((b, s, HGRN_WIDTH), BF16),
        scratch_shapes=[pltpu.VMEM((N_HEADS, HEAD, HEAD), F32)],
        compiler_params=_params(("arbitrary", "arbitrary")),
        name="hgrn2",
    )(zq, zf, zi, zg, lb, og, jnp.asarray(np.concatenate([mats, mats], axis=1), BF16), jnp.asarray(lvl))


def _attn_kernel(fast_ref, q_ref, k_ref, vt_ref, lq1_ref, lk1_ref, lq2_ref, lk2_ref, sub_ref, o_ref,
                 m_sc, l_sc, acc_sc, *, tq, lam_init):
    qi = pl.program_id(1)
    lane = lax.broadcasted_iota(jnp.int32, (tq, HEAD), 1)
    q_maps = []
    for hd in range(N_HEADS):
        q = q_ref[0, :, hd * HEAD:(hd + 1) * HEAD]
        zero = jnp.zeros_like(q)
        q_maps.append((jnp.where(lane < DIFF_HEAD_DIM, q, zero), jnp.where(lane >= DIFF_HEAD_DIM, q, zero)))

    def sublane_tile_sum(p):
        return jnp.sum(p.reshape(p.shape[0] // SUBLANES, SUBLANES, p.shape[1]), axis=0)

    def scores(q_map, kb, masked, first_key=0):
        s = _dot_nt(kb, q_map)
        if masked:
            key = lax.broadcasted_iota(jnp.int32, s.shape, 0) + first_key
            qry = lax.broadcasted_iota(jnp.int32, s.shape, 1)
            s = jnp.where(key <= qry, s, MASK_VALUE)
        return s

    def run(update):
        def step(ki, masked):
            k_all = k_ref[0, pl.ds(pl.multiple_of(ki * tq, tq), tq), :]
            vt_all = vt_ref[ki]
            for hd in range(N_HEADS):
                kb = k_all[:, hd * HEAD:(hd + 1) * HEAD]
                vt = vt_all[hd * HEAD:(hd + 1) * HEAD, :]
                for c in range(2):
                    update(hd, c, scores(q_maps[hd][c], kb, masked), vt)

        def body(ki, carry):
            step(ki, False)
            return carry
        lax.fori_loop(0, qi, body, 0)
        step(qi, True)

    @pl.when(fast_ref[0] == 1)
    def _():
        chains = [(hd, c) for hd in range(N_HEADS) for c in range(2)]
        heads = [slice(hd * HEAD, (hd + 1) * HEAD) for hd in range(N_HEADS)]
        half = tq // 2
        assert half % LANES == 0

        k_all = k_ref[0, pl.ds(pl.multiple_of(qi * tq, tq), tq), :]
        vt_all = vt_ref[qi]
        tops = [jnp.exp(scores(q_maps[hd][c], k_all[0:half, heads[hd]], True)) for hd, c in chains]
        bots = [jnp.exp(scores(q_maps[hd][c][half:], k_all[half:, heads[hd]], True)) for hd, c in chains]
        for (hd, c), p_top, p_bot in zip(chains, tops, bots):
            vt = vt_all[heads[hd], :]
            l_sc[hd, c] = sublane_tile_sum(p_top)
            l_sc[hd, c, :, half:] += sublane_tile_sum(p_bot)
            acc_sc[hd, c] = _dot(vt[:, 0:half], p_top.astype(BF16))
            acc_sc[hd, c, :, half:] += _dot(vt[:, half:], p_bot.astype(BF16))

        def body(ki, carry):
            k_all = k_ref[0, pl.ds(pl.multiple_of(ki * tq, tq), tq), :]
            vt_all = vt_ref[ki]
            ps = [jnp.exp(scores(q_maps[hd][c], k_all[:, heads[hd]], False)) for hd, c in chains]
            for (hd, c), p in zip(chains, ps):
                l_sc[hd, c] += sublane_tile_sum(p)
                acc_sc[hd, c] += _dot(vt_all[heads[hd], :], p.astype(BF16))
            return carry
        lax.fori_loop(0, qi, body, 0)

    @pl.when(fast_ref[0] != 1)
    def _():
        m_sc[...] = jnp.full_like(m_sc, MASK_VALUE)
        l_sc[...] = jnp.zeros_like(l_sc)
        acc_sc[...] = jnp.zeros_like(acc_sc)

        def update(hd, c, s, vt):
            m_old = m_sc[hd, c]
            m_new = jnp.maximum(m_old, jnp.max(s, axis=0, keepdims=True))
            alpha = jnp.exp(m_old - m_new)
            p = jnp.exp(s - m_new)
            l_sc[hd, c] = alpha * l_sc[hd, c] + sublane_tile_sum(p)
            acc_sc[hd, c] = alpha * acc_sc[hd, c] + _dot(vt, p.astype(BF16))
            m_sc[hd, c] = m_new
        run(update)

    lam = (jnp.exp(jnp.sum(lq1_ref[...] * lk1_ref[...], keepdims=True))
           - jnp.exp(jnp.sum(lq2_ref[...] * lk2_ref[...], keepdims=True)) + lam_init)
    for hd in range(N_HEADS):
        l0 = jnp.sum(l_sc[hd, 0], axis=0, keepdims=True)
        l1 = jnp.sum(l_sc[hd, 1], axis=0, keepdims=True)
        o_t = acc_sc[hd, 0] / l0 - lam * (acc_sc[hd, 1] / l1)
        ms = jnp.mean(o_t * o_t, axis=0, keepdims=True)
        o = (o_t * lax.rsqrt(ms + EPS)).T
        o_ref[0, :, hd * HEAD:(hd + 1) * HEAD] = (o * sub_ref[...] * (1.0 - lam_init)).astype(BF16)


def _attn(fast, q, k, vt, lq1, lk1, lq2, lk2, sub, lam_init, tq):
    b, s, w = q.shape
    n_kv = s // tq
    qblk = pl.BlockSpec((1, tq, w), lambda bi, qi, f: (bi, qi, 0))
    kblk = pl.BlockSpec((1, s, w), lambda bi, qi, f: (bi, 0, 0))
    vtblk = pl.BlockSpec((n_kv, w, tq), lambda bi, qi, f: (bi, 0, 0))
    const = lambda bi, qi, f: (0, 0)
    lamspec = pl.BlockSpec((1, DIFF_HEAD_DIM), const)
    return pl.pallas_call(
        functools.partial(_attn_kernel, tq=tq, lam_init=lam_init),
        grid_spec=pltpu.PrefetchScalarGridSpec(
            num_scalar_prefetch=1,
            grid=(b, s // tq),
            in_specs=[qblk, kblk, vtblk, lamspec, lamspec, lamspec, lamspec, pl.BlockSpec((1, HEAD), const)],
            out_specs=qblk,
            scratch_shapes=[pltpu.VMEM((N_HEADS, 2, 1, tq), F32), pltpu.VMEM((N_HEADS, 2, SUBLANES, tq), F32),
                            pltpu.VMEM((N_HEADS, 2, HEAD, tq), F32)],
        ),
        out_shape=jax.ShapeDtypeStruct((b, s, w), BF16),
        compiler_params=_params(("arbitrary", "arbitrary")),
        name="diffattn",
    )(fast, q, k, vt, lq1, lk1, lq2, lk2, sub)


def _outproj_kernel(oh_ref, od_ref, h_ref, wo_ref, fg_ref, wr_ref, h1_ref, xn_ref, route_ref, *, tm):
    n_sub = 2 if tm % (2 * LANES) == 0 else 1
    sub = tm // n_sub
    for si in range(n_sub):
        rs = slice(si * sub, (si + 1) * sub)
        _outproj_rows(oh_ref[rs, :], od_ref[rs, :], h_ref[rs, :], wo_ref, fg_ref, wr_ref,
                      h1_ref.at[rs], xn_ref.at[pl.ds(si * sub * ROW_TILES, sub * ROW_TILES)], route_ref.at[:, rs], sub)


def _outproj_rows(oh, od, h, wo_ref, fg_ref, wr_ref, h1_ref, xn_ref, route_ref, tm):
    hw = HGRN_WIDTH
    h1 = h + _dot(oh, wo_ref[0:hw, :]) + _dot(od, wo_ref[hw:2 * hw, :])
    h1_ref[...] = h1
    ms = jnp.mean(h1 * h1, axis=-1, keepdims=True)
    xn = h1 * lax.rsqrt(ms + EPS) * fg_ref[...]
    _store_token_tiles(xn_ref, xn, tm)
    x_hi, x_lo = _split_bf16(xn)
    a = _dot(x_hi, wr_ref[...])
    logits = a + pltpu.roll(a, LANES - N_EXPERTS, axis=1) + _dot(x_lo, wr_ref[...])
    lt = logits.T[0:N_EXPERTS, :]
    ex = jnp.exp(lt - jnp.max(lt, axis=0, keepdims=True))
    aff = ex / jnp.sum(ex, axis=0, keepdims=True)
    rows = [aff[e:e + 1, :] for e in range(N_EXPERTS)]

    def top2(a, first):
        m1, i1 = a[0], jnp.full_like(a[0], first)
        for j in range(1, len(a)):
            better = a[j] > m1
            m1 = jnp.where(better, a[j], m1)
            i1 = jnp.where(better, float(first + j), i1)
        m2, i2 = jnp.full_like(m1, -1.0), jnp.zeros_like(m1)
        for j in range(len(a)):
            cand = jnp.where(i1 == float(first + j), -1.0, a[j])
            better = cand > m2
            m2 = jnp.where(better, cand, m2)
            i2 = jnp.where(better, float(first + j), i2)
        return m1, i1, m2, i2

    best = None
    for g in range(N_GROUPS):
        first = g * EXPERTS_PER_GROUP
        m1, i1, m2, i2 = top2(rows[first:first + EXPERTS_PER_GROUP], first)
        cand = (m1, i1 - first, m2, i2 - first, jnp.full_like(m1, float(g)))
        score = m1 + m2
        if best is None:
            best, best_score = cand, score
        else:
            better = score > best_score
            best = tuple(jnp.where(better, c, b) for c, b in zip(cand, best))
            best_score = jnp.where(better, score, best_score)
    m1, j1, m2, j2, gsel = best
    denom = m1 + m2
    w1, w2 = m1 / denom, m2 / denom
    rid = lax.broadcasted_iota(jnp.int32, (SUBLANES, tm), 0).astype(F32)
    r8 = jnp.where(rid == j1, w1, jnp.where(rid == j2, w2, 0.0))
    route_ref[...] = jnp.where(rid == float(EXPERTS_PER_GROUP), gsel, r8)


def _outproj(oh, od, h, wo_bf, fg, wr_pack, tm):
    t = h.shape[0]
    row = lambda i: (i, 0)
    const = lambda i: (0, 0)
    return pl.pallas_call(
        functools.partial(_outproj_kernel, tm=tm),
        grid=(t // tm,),
        in_specs=[
            pl.BlockSpec((tm, HGRN_WIDTH), row),
            pl.BlockSpec((tm, HGRN_WIDTH), row),
            pl.BlockSpec((tm, D_MODEL), row),
            pl.BlockSpec((D_MODEL, D_MODEL), const),
            pl.BlockSpec((1, D_MODEL), const),
            pl.BlockSpec((D_MODEL, LANES), const),
        ],
        out_specs=[pl.BlockSpec((tm, D_MODEL), row), pl.BlockSpec((tm * ROW_TILES, LANES), row),
                   pl.BlockSpec((SUBLANES, tm), lambda i: (0, i))],
        out_shape=[jax.ShapeDtypeStruct((t, D_MODEL), F32), jax.ShapeDtypeStruct((t * ROW_TILES, LANES), F32),
                   jax.ShapeDtypeStruct((SUBLANES, t), F32)],
        compiler_params=_params(("arbitrary",)),
        name="outproj_router",
    )(oh, od, h, wo_bf, fg, wr_pack)


def _tile_copy(src, src_row, dst, dst_row, sem):
    return pltpu.make_async_copy(src.at[pl.ds(pl.multiple_of(src_row * ROW_TILES, ROW_TILES), ROW_TILES)],
                                 dst.at[pl.ds(pl.multiple_of(dst_row * ROW_TILES, ROW_TILES), ROW_TILES)], sem)


def _gather_rows(idx_ref, n, src_hbm, dst, sem):
    def issue(jo, carry):
        for u in range(GATHER_UNROLL):
            j = jo * GATHER_UNROLL + u
            _tile_copy(src_hbm, idx_ref[0, 0, j], dst, j, sem).start()
        return carry
    lax.fori_loop(0, n // GATHER_UNROLL, issue, 0)


def _gather_rows_inline(idx_ref, n, src_hbm, dst, sem):
    for j in range(n):
        _tile_copy(src_hbm, idx_ref[0, 0, j], dst, j, sem).start()


def _wait_rows(n, src_hbm, dst, sem):
    pltpu.make_async_copy(src_hbm.at[pl.ds(0, n * ROW_TILES)], dst, sem).wait()


def _expert_kernel(item_blk_ref, item_g_ref, item_kind_ref, tok0_ref, tok1_ref, tok2_ref, xn_hbm, wrow_ref,
                   wg_ref, wu_ref, wd_ref, y_ref, xbuf, sem, *, n_blk):
    k = pl.program_id(0)
    i = item_blk_ref[k]
    group = item_g_ref[k]
    kind = item_kind_ref[k]
    slot = i % GATHER_SLOTS

    @pl.when(k == 0)
    def _():
        _gather_rows(tok0_ref, ROW_BLOCK, xn_hbm, xbuf.at[0], sem.at[0])
        _gather_rows(tok1_ref, ROW_BLOCK, xn_hbm, xbuf.at[1], sem.at[1])

    def group_rows(x):
        wrow = wrow_ref[...]
        in_group = wrow[:, EXPERTS_PER_GROUP:EXPERTS_PER_GROUP + 1] == group.astype(F32)
        y = None
        for j in range(EXPERTS_PER_GROUP):
            g = _dot(x, wg_ref[j])
            u = _dot(x, wu_ref[j])
            wj = jnp.where(in_group, wrow[:, j:j + 1], 0.0)
            mid = jnp.where(wj != 0.0, (g * jax.nn.sigmoid(g)) * u * wj, 0.0)
            part = _dot(mid.astype(BF16), wd_ref[j])
            y = part if y is None else y + part
        return y

    @pl.when(kind == 1)
    def _():
        ahead = (i + 2) % GATHER_SLOTS
        _wait_rows(ROW_BLOCK, xn_hbm, xbuf.at[slot], sem.at[slot])
        x = jnp.concatenate(_load_token_tiles(xbuf.at[slot], ROW_BLOCK), axis=1).astype(BF16)
        _gather_rows_inline(tok2_ref, ROW_BLOCK, xn_hbm, xbuf.at[ahead], sem.at[ahead])
        _store_token_tiles(y_ref, group_rows(x), ROW_BLOCK)

    @pl.when(kind == 2)
    def _():
        x = jnp.concatenate(_load_token_tiles(xbuf.at[slot], ROW_BLOCK), axis=1).astype(BF16)
        prev = jnp.concatenate(_load_token_tiles(y_ref, ROW_BLOCK), axis=1)
        _store_token_tiles(y_ref, prev + group_rows(x), ROW_BLOCK)

    @pl.when(k == pl.num_programs(0) - 1)
    def _():
        for extra in (0, 1):
            s_extra = (n_blk + extra) % GATHER_SLOTS
            _wait_rows(ROW_BLOCK, xn_hbm, xbuf.at[s_extra], sem.at[s_extra])


def _experts(items, row_tok, row_w, xn_tiles, wg, wu, wd, layer):
    n_blk = row_tok.shape[0]
    assert n_blk >= GATHER_SLOTS
    wsel = lambda k, ib, ig, ik: (layer * N_GROUPS + ig[k], 0, 0)
    blk = lambda k, ib, ig, ik: (ib[k], 0)

    def tok_spec(ahead):
        return pl.BlockSpec((1, 1, ROW_BLOCK), lambda k, ib, ig, ik: (jnp.minimum(ib[k] + ahead, n_blk - 1), 0, 0),
                            memory_space=pltpu.SMEM)

    return pl.pallas_call(
        functools.partial(_expert_kernel, n_blk=n_blk),
        grid_spec=pltpu.PrefetchScalarGridSpec(
            num_scalar_prefetch=3,
            grid=(items[0].shape[0],),
            in_specs=[tok_spec(0), tok_spec(1), tok_spec(2),
                      pl.BlockSpec(memory_space=pl.ANY),
                      pl.BlockSpec((ROW_BLOCK, SUBLANES), blk),
                      pl.BlockSpec((EXPERTS_PER_GROUP, D_MODEL, D_FF), wsel),
                      pl.BlockSpec((EXPERTS_PER_GROUP, D_MODEL, D_FF), wsel),
                      pl.BlockSpec((EXPERTS_PER_GROUP, D_FF, D_MODEL), wsel)],
            out_specs=pl.BlockSpec((ROW_BLOCK * ROW_TILES, LANES), blk),
            scratch_shapes=[pltpu.VMEM((GATHER_SLOTS, ROW_BLOCK * ROW_TILES, LANES), F32),
                            pltpu.SemaphoreType.DMA((GATHER_SLOTS,))],
        ),
        out_shape=jax.ShapeDtypeStruct((n_blk * ROW_BLOCK * ROW_TILES, LANES), F32),
        compiler_params=_params(("arbitrary",)),
        name="moe_experts",
    )(*items, row_tok, row_tok, row_tok, xn_tiles, row_w, wg, wu, wd)


def _combine_kernel(dest_ref, dest_next_ref, h1_ref, p_ref, pp_ref, pn_ref, pg_ref, y_hbm, o_ref,
                    ybuf, sem, *, tm):
    i = pl.program_id(0)
    slot = i % 2

    @pl.when(i == 0)
    def _():
        _gather_rows(dest_ref, tm, y_hbm, ybuf.at[0], sem.at[0])

    _gather_rows_inline(dest_next_ref, tm, y_hbm, ybuf.at[1 - slot], sem.at[1 - slot])

    e = _dot(p_ref[0].astype(BF16), pp_ref[...])
    ms = jnp.mean(e * e, axis=-1, keepdims=True)
    ple = e * lax.rsqrt(ms + EPS) * pn_ref[...]

    _wait_rows(tm, y_hbm, ybuf.at[slot], sem.at[slot])
    h2 = h1_ref[...] + jnp.concatenate(_load_token_tiles(ybuf.at[slot], tm), axis=1)
    gate = jax.nn.sigmoid(_dot(h2.astype(BF16), pg_ref[...]))
    o_ref[...] = h2 + ple * gate

    @pl.when(i == pl.num_programs(0) - 1)
    def _():
        _wait_rows(tm, y_hbm, ybuf.at[1 - slot], sem.at[1 - slot])


def _combine(dest3, h1, p, layer, pp_bf, pn, pg_bf, y_tiles, tm):
    t = h1.shape[0]
    n_tiles = t // tm
    row = lambda i: (i, 0)
    const = lambda i: (0, 0)
    n = tm
    return pl.pallas_call(
        functools.partial(_combine_kernel, tm=tm),
        grid=(n_tiles,),
        in_specs=[pl.BlockSpec((1, 1, n), lambda i: (i, 0, 0), memory_space=pltpu.SMEM),
                  pl.BlockSpec((1, 1, n), lambda i: (jnp.minimum(i + 1, n_tiles - 1), 0, 0), memory_space=pltpu.SMEM),
                  pl.BlockSpec((tm, D_MODEL), row),
                  pl.BlockSpec((1, tm, PLE_DIM), lambda i: (layer, i, 0)),
                  pl.BlockSpec((PLE_DIM, D_MODEL), const),
                  pl.BlockSpec((1, D_MODEL), const),
                  pl.BlockSpec((D_MODEL, D_MODEL), const),
                  pl.BlockSpec(memory_space=pl.ANY)],
        out_specs=pl.BlockSpec((tm, D_MODEL), row),
        out_shape=jax.ShapeDtypeStruct((t, D_MODEL), F32),
        scratch_shapes=[pltpu.VMEM((2, n * ROW_TILES, LANES), F32), pltpu.SemaphoreType.DMA((2,))],
        compiler_params=_params(("arbitrary",)),
        name="moe_combine_ple",
    )(dest3, dest3, h1, p, pp_bf, pn, pg_bf, y_tiles)


def _routing_tables(route):
    n_tok = route.shape[1]
    assert n_tok % ROW_BLOCK == 0
    tok_g = route[EXPERTS_PER_GROUP].astype(jnp.int32)
    iota = jnp.arange(n_tok, dtype=jnp.int32)
    weights = [route[j] for j in range(EXPERTS_PER_GROUP)]
    sorted_g, row_tok, *row_weights = lax.sort((tok_g, iota, *weights), num_keys=1, is_stable=True)
    _, dest = lax.sort((row_tok, iota), num_keys=1)
    zeros = jnp.zeros_like(row_weights[0])
    row_w = jnp.stack(row_weights + [sorted_g.astype(F32)] + [zeros] * (SUBLANES - EXPERTS_PER_GROUP - 1), axis=1)
    g_lo = sorted_g[0::ROW_BLOCK]
    g_hi = sorted_g[ROW_BLOCK - 1::ROW_BLOCK]
    n_blk = n_tok // ROW_BLOCK
    per_blk = g_hi - g_lo + 1
    first_item = jnp.cumsum(per_blk) - per_blk
    k = jnp.arange(n_blk + N_GROUPS - 1, dtype=jnp.int32)
    item_blk = (jnp.sum((first_item[None, :] <= k[:, None]).astype(jnp.int32), axis=1) - 1).astype(jnp.int32)
    nth = k - first_item[item_blk]
    valid = nth < per_blk[item_blk]
    item_g = jnp.minimum(g_lo[item_blk] + nth, g_hi[item_blk]).astype(jnp.int32)
    item_kind = jnp.where(valid, jnp.where(nth == 0, 1, 2), 0).astype(jnp.int32)
    return dest, row_tok, row_w, (item_blk, item_g, item_kind)


def _tile(n, pref):
    return pref if n % pref == 0 else n


def kernel(x, p, mix_norm, w_in, hgrn_lb, hgrn_out_norm, q_norm, k_norm, lam_q1, lam_k1, lam_q2, lam_k2,
           diff_subln, w_out, ffn_norm, w_router, w_gate, w_up, w_down, ple_proj, ple_norm, ple_gate):
    b, s, d = x.shape
    depth = w_in.shape[0]
    t = b * s
    tm = _tile(t, TOKEN_TILE)
    ts = _tile(s, SEQ_TILE)
    tq = _tile(s, TOKEN_TILE)
    tmc = _tile(t, TOKEN_TILE)

    lb_soft = jax.nn.softmax(hgrn_lb.astype(F32), axis=0)
    lower_bounds = jnp.cumsum(lb_soft, axis=0) - lb_soft[0]
    grp = np.kron(np.eye(2 * LANES // DIFF_HEAD_DIM), np.ones((DIFF_HEAD_DIM, DIFF_HEAD_DIM)))
    grp = jnp.asarray(grp, BF16)
    wr_hi, wr_lo = _split_bf16(w_router.astype(F32))
    wr_pack = jnp.pad(jnp.concatenate([wr_hi, wr_lo], axis=1), ((0, 0), (0, LANES - 2 * N_EXPERTS)))
    n_rep = HGRN_WIDTH // DIFF_HEAD_DIM

    wg_all = w_gate.astype(BF16).reshape(depth * N_EXPERTS, d, D_FF)
    wu_all = w_up.astype(BF16).reshape(depth * N_EXPERTS, d, D_FF)
    wd_all = w_down.astype(BF16).reshape(depth * N_EXPERTS, D_FF, d)
    p_all = p.reshape(depth, t, PLE_DIM)

    h = x.reshape(t, d)
    for i in range(depth):
        lam_init = 0.8 - 0.6 * math.exp(-0.3 * i)
        gq = (jnp.tile(q_norm[i], n_rep) * DIFF_HEAD_DIM ** -0.5).reshape(1, HGRN_WIDTH)
        gk = jnp.tile(k_norm[i], n_rep).reshape(1, HGRN_WIDTH)
        score_bound = 1.02 * DIFF_HEAD_DIM ** 0.5 * jnp.max(jnp.abs(q_norm[i])) * jnp.max(jnp.abs(k_norm[i]))
        fast = (score_bound <= MAX_UNSHIFTED_SCORE).astype(jnp.int32).reshape(1)
        w_bf = w_in[i].astype(BF16)
        n_main = IN_COLS - HGRN_WIDTH
        zq, zf, zi, zg, dq, dk, dvt = _inproj(h, mix_norm[i].reshape(1, d), w_bf[:, :n_main], w_bf[:, n_main:].T,
                                              gq, gk, grp, tq)
        r3 = lambda a: a.reshape(b, s, HGRN_WIDTH)
        o_hgrn = _hgrn(r3(zq), r3(zf), r3(zi), r3(zg), lower_bounds[i].reshape(1, HGRN_WIDTH),
                       hgrn_out_norm[i].reshape(1, HEAD), ts)
        row64 = lambda a: a.reshape(1, DIFF_HEAD_DIM)
        o_diff = _attn(fast, r3(dq), r3(dk), dvt, row64(lam_q1[i]), row64(lam_k1[i]), row64(lam_q2[i]),
                       row64(lam_k2[i]), diff_subln[i].reshape(1, HEAD), lam_init, tq)
        h1, xn_tiles, route = _outproj(o_hgrn.reshape(t, HGRN_WIDTH), o_diff.reshape(t, HGRN_WIDTH), h,
                                       w_out[i].astype(BF16), ffn_norm[i].reshape(1, d), wr_pack, tm)
        dest, row_tok, row_w, items = _routing_tables(route)
        y_tiles = _experts(items, row_tok.reshape(t // ROW_BLOCK, 1, ROW_BLOCK), row_w, xn_tiles,
                           wg_all, wu_all, wd_all, i)
        h = _combine(dest.reshape(t // tmc, 1, tmc), h1, p_all, i, ple_proj[i].astype(BF16),
                     ple_norm[i].reshape(1, d), ple_gate[i].astype(BF16), y_tiles, tmc)
    return h.reshape(b, s, d)
```
